```python
import math
import jax, jax.numpy as jnp
from jax import lax
import numpy as np

D_MODEL = 1024
BATCH = 4
SEQ = 4096
DEPTH = 1
DEC_BATCH = 128
DEC_SEQ = 8
PAST_LEN = 16384
PAGE_SIZE = 128

HEAD_DIM = 64
N_Q_HEADS = 16
N_KV_HEADS = 2
Q_PER_KV = N_Q_HEADS // N_KV_HEADS
WINDOW = 128
ATTN_SCALE = HEAD_DIM ** -0.5
NEG_INF = -1e30
N_BUCKETS = 32
BUCKET_MAX_EXACT = 16
BUCKET_MAX_DIST = 128
CHUNK = 128
N_SG_GROUPS = 4
SG_GROUP_DIM = 128
SG_WIDTH = N_SG_GROUPS * SG_GROUP_DIM
Q_W = N_Q_HEADS * HEAD_DIM
KV_W = N_KV_HEADS * HEAD_DIM
Q_END = Q_W
K_END = Q_END + KV_W
V_END = K_END + KV_W
U_END = V_END + SG_WIDTH
VS_END = U_END + SG_WIDTH
GA_END = VS_END + D_MODEL
IN_W = GA_END + D_MODEL
N_EXPERTS = 64
TOP_K = 8
N_EXPERT_GROUPS = 8
EXPERTS_PER_GROUP = N_EXPERTS // N_EXPERT_GROUPS
TOPK_GROUPS = 4
EXPERT_DIM = 256
SHARED_DIM = 256
ROUTED_SCALE = 2.5
ALPHA = (2 * DEPTH) ** 0.25
BETA = (8 * DEPTH) ** -0.25
LN_EPS = 1e-5

kernel_name = 'hybrid_swa_sink_gmlp_moe_deepnorm_step'


def layer_norm(x, g, b):
    xf = x.astype(jnp.float32)
    mu = jnp.mean(xf, -1, keepdims=True)
    var = jnp.mean(jnp.square(xf - mu), -1, keepdims=True)
    return ((xf - mu) * lax.rsqrt(var + LN_EPS) * g.astype(jnp.float32) + b.astype(jnp.float32)).astype(x.dtype)


def t5_bucket(dist):
    d = jnp.maximum(dist, 0)
    ratio = jnp.maximum(d, 1).astype(jnp.float32) / BUCKET_MAX_EXACT
    large = BUCKET_MAX_EXACT + (jnp.log(ratio) / math.log(BUCKET_MAX_DIST / BUCKET_MAX_EXACT)
                                * (N_BUCKETS - BUCKET_MAX_EXACT)).astype(jnp.int32)
    large = jnp.minimum(large, N_BUCKETS - 1)
    return jnp.where(d < BUCKET_MAX_EXACT, d, large)


def rel_bias(dist, table):
    return jnp.moveaxis(table[t5_bucket(dist)].astype(jnp.float32), -1, 0)


def sink_attend(logits, sink, v, eq):
    sink = sink.astype(jnp.float32)
    m = jnp.maximum(jnp.max(logits, -1, keepdims=True), sink)
    p = jnp.exp(logits - m)
    den = jnp.sum(p, -1, keepdims=True) + jnp.exp(sink - m)
    return jnp.einsum(eq, (p / den).astype(v.dtype), v)


def project(x, w_in, b_in):
    z = x @ w_in + b_in
    q, k, v, u, vs, ga, gb = jnp.split(z, [Q_END, K_END, V_END, U_END, VS_END, GA_END], axis=-1)
    lead = x.shape[:-1]
    q = q.reshape(*lead, N_KV_HEADS, Q_PER_KV, HEAD_DIM)
    k = k.reshape(*lead, N_KV_HEADS, HEAD_DIM)
    v = v.reshape(*lead, N_KV_HEADS, HEAD_DIM)
    return q, k, v, jax.nn.gelu(u), jax.nn.gelu(vs), ga, gb


def attend_prompt(q, k, v, sinks, table):
    B, S = q.shape[:2]
    L = WINDOW
    nb = S // L
    qb = q.reshape(B, nb, L, N_KV_HEADS, Q_PER_KV, HEAD_DIM)

    def band(t):
        tb = t.reshape(B, nb, L, N_KV_HEADS, HEAD_DIM)
        prev = jnp.concatenate([jnp.zeros_like(tb[:, :1]), tb[:, :-1]], axis=1)
        return jnp.concatenate([prev, tb], axis=2)

    kband, vband = band(k), band(v)
    i = jnp.arange(L)[:, None]
    j = jnp.arange(2 * L)[None, :]
    dist = i + L - j
    kpos = jnp.arange(nb)[:, None, None] * L - L + j[None]
    valid = (dist >= 0) & (dist <= WINDOW) & (kpos >= 0)
    bias = rel_bias(dist, table).reshape(N_KV_HEADS, Q_PER_KV, L, 2 * L)
    logits = jnp.einsum('bnqhrd,bnkhd->bnhrqk', qb, kband,
                        preferred_element_type=jnp.float32) * ATTN_SCALE + bias
    logits = jnp.where(valid[None, :, None, None], logits, NEG_INF)
    out = sink_attend(logits, sinks.reshape(1, 1, N_KV_HEADS, Q_PER_KV, 1, 1), vband,
                      'bnhrqk,bnkhd->bnqhrd')
    return out.reshape(B, S, Q_W)


def attend_sample(q, k, v, cache_k, cache_v, sinks, table):
    DB, DS = q.shape[:2]
    W = cache_k.shape[1]
    keys = jnp.concatenate([cache_k, k], axis=1)
    vals = jnp.concatenate([cache_v, v], axis=1)
    kpos = jnp.concatenate([jnp.arange(W) - W, jnp.arange(DS)])
    dist = jnp.arange(DS)[:, None] - kpos[None, :]
    valid = (dist >= 0) & (dist <= WINDOW)
    bias = rel_bias(dist, table).reshape(N_KV_HEADS, Q_PER_KV, DS, W + DS)
    logits = jnp.einsum('bqhrd,bkhd->bhrqk', q, keys,
                        preferred_element_type=jnp.float32) * ATTN_SCALE + bias
    logits = jnp.where(valid, logits, NEG_INF)
    out = sink_attend(logits, sinks.reshape(1, N_KV_HEADS, Q_PER_KV, 1, 1), vals,
                      'bhrqk,bkhd->bqhrd')
    return out.reshape(DB, DS, Q_W), keys[:, -W:], vals[:, -W:]


def spatial_gate_prompt(u, vs, sg_w, sg_b):
    B, S = u.shape[:2]
    nc = S // CHUNK
    vc = vs.reshape(B, nc, CHUNK, N_SG_GROUPS, SG_GROUP_DIM)
    s = jnp.einsum('gij,bnjgc->bnigc', jnp.tril(sg_w), vc) + sg_b.T[:, :, None]
    return (u.reshape(B, S, N_SG_GROUPS, SG_GROUP_DIM) * s.reshape(B, S, N_SG_GROUPS, SG_GROUP_DIM)).reshape(B, S, SG_WIDTH)


def spatial_gate_sample(u, vs, sg_w, sg_b):
    DB, DS = u.shape[:2]
    v4 = vs.reshape(DB, DS, N_SG_GROUPS, SG_GROUP_DIM)
    s = jnp.einsum('gij,bjgc->bigc', jnp.tril(sg_w[:, :DS, :DS]), v4) + sg_b[:, :DS].T[:, :, None]
    return (u.reshape(DB, DS, N_SG_GROUPS, SG_GROUP_DIM) * s).reshape(DB, DS, SG_WIDTH), v4


def moe(x, w_router, router_bias, w_gate_e, w_up_e, w_down_e, w_gate_s, w_up_s, w_down_s):
    shp = x.shape
    t = x.reshape(-1, D_MODEL)
    scores = jax.nn.sigmoid((t @ w_router).astype(jnp.float32))
    sel = scores + router_bias.astype(jnp.float32)
    grp = sel.reshape(-1, N_EXPERT_GROUPS, EXPERTS_PER_GROUP)
    grp_score = jnp.sum(lax.top_k(grp, 2)[0], -1)
    _, top_g = lax.top_k(grp_score, TOPK_GROUPS)
    gmask = jnp.sum(jax.nn.one_hot(top_g, N_EXPERT_GROUPS, dtype=jnp.float32), axis=1)
    emask = jnp.repeat(gmask, EXPERTS_PER_GROUP, axis=1) > 0
    _, top_e = lax.top_k(jnp.where(emask, sel, NEG_INF), TOP_K)
    w_sel = jnp.take_along_axis(scores, top_e, axis=1)
    w_sel = w_sel / jnp.sum(w_sel, -1, keepdims=True) * ROUTED_SCALE
    gates = jnp.sum(jax.nn.one_hot(top_e, N_EXPERTS, dtype=jnp.float32) * w_sel[..., None], axis=1)

    def expert_step(acc, xs):
        wg, wu, wd, g = xs
        h = jax.nn.silu(t @ wg) * (t @ wu)
        return acc + (h @ wd) * g[:, None].astype(t.dtype), None

    routed, _ = lax.scan(expert_step, jnp.zeros_like(t), (w_gate_e, w_up_e, w_down_e, gates.T))
    shared = (jax.nn.silu(t @ w_gate_s) * (t @ w_up_s)) @ w_down_s
    return (routed + shared).reshape(shp)


def merge_residual(x, a, s, ga, gb, lp):
    h = (jax.nn.sigmoid(ga) * (a @ lp['w_proj_a']) + jax.nn.sigmoid(gb) * (s @ lp['w_proj_b'])) @ lp['w_out']
    return layer_norm(ALPHA * x + h, lp['ln1_g'], lp['ln1_b'])


def ffn_residual(x, lp):
    y = moe(x, lp['w_router'], lp['router_bias'], lp['w_gate_e'], lp['w_up_e'], lp['w_down_e'],
            lp['w_gate_s'], lp['w_up_s'], lp['w_down_s'])
    return layer_norm(ALPHA * x + y, lp['ln2_g'], lp['ln2_b'])


def prompt_layer(x, lp, table):
    q, k, v, u, vs, ga, gb = project(x, lp['w_in'], lp['b_in'])
    vs = layer_norm(vs, lp['sg_ln_g'], lp['sg_ln_b'])
    a = attend_prompt(q, k, v, lp['attn_sinks'], table)
    s = spatial_gate_prompt(u, vs, lp['sg_w'], lp['sg_b'])
    x = ffn_residual(merge_residual(x, a, s, ga, gb, lp), lp)
    keep = min(WINDOW, x.shape[1])
    return x, k[:, -keep:], v[:, -keep:]


def sample_layer(x, ck, cv, lp, table):
    q, k, v, u, vs, ga, gb = project(x, lp['w_in'], lp['b_in'])
    vs = layer_norm(vs, lp['sg_ln_g'], lp['sg_ln_b'])
    a, nk, nv = attend_sample(q, k, v, ck, cv, lp['attn_sinks'], table)
    s, v_rows = spatial_gate_sample(u, vs, lp['sg_w'], lp['sg_b'])
    x = ffn_residual(merge_residual(x, a, s, ga, gb, lp), lp)
    return x, nk, nv, v_rows


def setup_inputs(seed: int = 0) -> dict:
    key = jax.random.key(seed)
    ks = jax.random.split(key, 32)
    f32 = jnp.float32

    def nrm(k, shape, scale):
        return jax.random.normal(k, shape, f32) * scale

    win_rows = min(WINDOW, PAST_LEN)
    w_in = nrm(ks[4], (DEPTH, D_MODEL, IN_W), D_MODEL ** -0.5)
    w_in = w_in.at[:, :, K_END:V_END].multiply(BETA)
    return {
        'x_prompt': nrm(ks[0], (BATCH, SEQ, D_MODEL), 1.0),
        'x_sample': nrm(ks[1], (DEC_BATCH, DEC_SEQ, D_MODEL), 1.0),
        'cache_win_k': nrm(ks[2], (DEPTH, DEC_BATCH, win_rows, N_KV_HEADS, HEAD_DIM), 1.0),
        'cache_win_v': nrm(ks[3], (DEPTH, DEC_BATCH, win_rows, N_KV_HEADS, HEAD_DIM), 1.0),
        'rel_bias_table': nrm(ks[7], (N_BUCKETS, N_Q_HEADS), 0.5),
        'w_in': w_in,
        'b_in': nrm(ks[5], (DEPTH, IN_W), 0.02),
        'attn_sinks': nrm(ks[6], (DEPTH, N_Q_HEADS), 0.5),
        'sg_ln_g': 1.0 + nrm(ks[8], (DEPTH, SG_WIDTH), 0.02),
        'sg_ln_b': nrm(ks[9], (DEPTH, SG_WIDTH), 0.02),
        'sg_w': nrm(ks[10], (DEPTH, N_SG_GROUPS, CHUNK, CHUNK), CHUNK ** -0.5),
        'sg_b': 1.0 + nrm(ks[11], (DEPTH, N_SG_GROUPS, CHUNK), 0.1),
        'w_proj_a': nrm(ks[12], (DEPTH, Q_W, D_MODEL), Q_W ** -0.5 * BETA),
        'w_proj_b': nrm(ks[13], (DEPTH, SG_WIDTH, D_MODEL), SG_WIDTH ** -0.5 * BETA),
        'w_out': nrm(ks[14], (DEPTH, D_MODEL, D_MODEL), D_MODEL ** -0.5 * BETA),
        'ln1_g': 1.0 + nrm(ks[15], (DEPTH, D_MODEL), 0.02),
        'ln1_b': nrm(ks[16], (DEPTH, D_MODEL), 0.02),
        'w_router': nrm(ks[17], (DEPTH, D_MODEL, N_EXPERTS), D_MODEL ** -0.5),
        'router_bias': nrm(ks[18], (DEPTH, N_EXPERTS), 0.01),
        'w_gate_e': nrm(ks[19], (DEPTH, N_EXPERTS, D_MODEL, EXPERT_DIM), D_MODEL ** -0.5),
        'w_up_e': nrm(ks[20], (DEPTH, N_EXPERTS, D_MODEL, EXPERT_DIM), D_MODEL ** -0.5),
        'w_down_e': nrm(ks[21], (DEPTH, N_EXPERTS, EXPERT_DIM, D_MODEL), EXPERT_DIM ** -0.5 * BETA),
        'w_gate_s': nrm(ks[22], (DEPTH, D_MODEL, SHARED_DIM), D_MODEL ** -0.5),
        'w_up_s': nrm(ks[23], (DEPTH, D_MODEL, SHARED_DIM), D_MODEL ** -0.5),
        'w_down_s': nrm(ks[24], (DEPTH, SHARED_DIM, D_MODEL), SHARED_DIM ** -0.5 * BETA),
        'ln2_g': 1.0 + nrm(ks[25], (DEPTH, D_MODEL), 0.02),
        'ln2_b': nrm(ks[26], (DEPTH, D_MODEL), 0.02),
    }


def reference(x_prompt, x_sample, cache_win_k, cache_win_v, rel_bias_table, w_in, b_in, attn_sinks,
              sg_ln_g, sg_ln_b, sg_w, sg_b, w_proj_a, w_proj_b, w_out, ln1_g, ln1_b, w_router,
              router_bias, w_gate_e, w_up_e, w_down_e, w_gate_s, w_up_s, w_down_s, ln2_g, ln2_b):
    xp, xs = x_prompt, x_sample
    kp_l, vp_l, ks_l, vs_l, cv_l = [], [], [], [], []
    for l in range(DEPTH):
        lp = {
            'w_in': w_in[l], 'b_in': b_in[l], 'attn_sinks': attn_sinks[l],
            'sg_ln_g': sg_ln_g[l], 'sg_ln_b': sg_ln_b[l], 'sg_w': sg_w[l], 'sg_b': sg_b[l],
            'w_proj_a': w_proj_a[l], 'w_proj_b': w_proj_b[l], 'w_out': w_out[l],
            'ln1_g': ln1_g[l], 'ln1_b': ln1_b[l], 'w_router': w_router[l], 'router_bias': router_bias[l],
            'w_gate_e': w_gate_e[l], 'w_up_e': w_up_e[l], 'w_down_e': w_down_e[l],
            'w_gate_s': w_gate_s[l], 'w_up_s': w_up_s[l], 'w_down_s': w_down_s[l],
            'ln2_g': ln2_g[l], 'ln2_b': ln2_b[l],
        }
        xp, kp, vp = prompt_layer(xp, lp, rel_bias_table)
        xs, nk, nv, cv = sample_layer(xs, cache_win_k[l], cache_win_v[l], lp, rel_bias_table)
        kp_l.append(kp); vp_l.append(vp); ks_l.append(nk); vs_l.append(nv); cv_l.append(cv)
    win_k_prompt = jnp.stack(kp_l)
    win_v_prompt = jnp.stack(vp_l)
    win_k_sample = jnp.stack(ks_l)
    win_v_sample = jnp.stack(vs_l)
    chunk_v_sample = jnp.stack(cv_l)
    return (xp, xs, win_k_prompt, win_v_prompt, win_k_sample, win_v_sample, chunk_v_sample)
```

```python
import functools
import math

import jax
import jax.numpy as jnp
import numpy as np
from jax import lax
from jax.experimental import pallas as pl
from jax.experimental.pallas import tpu as pltpu

F32 = jnp.float32
BF16 = jnp.bfloat16

D_MODEL = 1024
DEPTH = 1
HEAD_DIM = 64
N_Q_HEADS = 16
N_KV_HEADS = 2
Q_PER_KV = N_Q_HEADS // N_KV_HEADS
WINDOW = 128
ATTN_SCALE = HEAD_DIM ** -0.5
NEG_INF = -1e30
N_BUCKETS = 32
BUCKET_MAX_EXACT = 16
BUCKET_MAX_DIST = 128
CHUNK = 128
N_SG_GROUPS = 4
SG_GROUP_DIM = 128
SG_WIDTH = N_SG_GROUPS * SG_GROUP_DIM
Q_W = N_Q_HEADS * HEAD_DIM
KV_W = N_KV_HEADS * HEAD_DIM
Q_END = Q_W
K_END = Q_END + KV_W
V_END = K_END + KV_W
U_END = V_END + SG_WIDTH
VS_END = U_END + SG_WIDTH
GA_END = VS_END + D_MODEL
IN_W = GA_END + D_MODEL
N_EXPERTS = 64
TOP_K = 8
N_EXPERT_GROUPS = 8
EXPERTS_PER_GROUP = N_EXPERTS // N_EXPERT_GROUPS
TOPK_GROUPS = 4
EXPERT_DIM = 256
SHARED_DIM = 256
ROUTED_SCALE = 2.5
ALPHA = (2 * DEPTH) ** 0.25
LN_EPS = 1e-5

VMEM_LIMIT_BYTES = 56 * 1024 * 1024

PROMPT_STEP = 512
SAMPLE_SEQS_PER_STEP = 32
MOE_TILE = 1024


def _t5_bucket_np(dist):
    d = np.maximum(dist, 0)
    ratio = np.maximum(d, 1).astype(np.float32) / np.float32(BUCKET_MAX_EXACT)
    large = BUCKET_MAX_EXACT + (
        np.log(ratio) / np.float32(math.log(BUCKET_MAX_DIST / BUCKET_MAX_EXACT))
        * np.float32(N_BUCKETS - BUCKET_MAX_EXACT)).astype(np.int32)
    large = np.minimum(large, N_BUCKETS - 1)
    return np.where(d < BUCKET_MAX_EXACT, d, large).astype(np.int32)


def _layer_norm(x, g, b):
    mu = jnp.mean(x, -1, keepdims=True)
    xc = x - mu
    var = jnp.mean(xc * xc, -1, keepdims=True)
    return xc * lax.rsqrt(var + LN_EPS) * g + b


def _gelu(x):
    return jax.nn.gelu(x)


def _dot(a, b):
    return jnp.dot(a, b, preferred_element_type=F32)


def _dot_nt(a, b):
    return lax.dot_general(a, b, (((1,), (1,)), ((), ())), preferred_element_type=F32)


def _project(xb, w_in_ref, b_in_ref, lo, hi):
    return _dot(xb, w_in_ref[:, lo:hi]) + b_in_ref[:, lo:hi]


def _expand_bias(bucket, table_ref, head):
    acc = jnp.zeros(bucket.shape, F32)
    for b in range(N_BUCKETS):
        acc = jnp.where(bucket == b, table_ref[b, head], acc)
    return acc


def _merge_and_norm(x, a_bf, s_bf, ga, gb, wpa_ref, wpb_ref, wout_ref, g_ref, b_ref):
    pa = _dot(a_bf, wpa_ref[...])
    pb = _dot(s_bf, wpb_ref[...])
    hpre = jax.nn.sigmoid(ga) * pa + jax.nn.sigmoid(gb) * pb
    h = _dot(hpre.astype(BF16), wout_ref[...])
    return _layer_norm(ALPHA * x + h, g_ref[...], b_ref[...])


def _mixer_prompt_kernel(table_ref, bucket_ref, sink_ref, x_ref, w_in_ref, b_in_ref,
                         sgg_ref, sgb_ref, sgw_ref, sgbias_ref, wpa_ref, wpb_ref, wout_ref,
                         ln1g_ref, ln1b_ref,
                         x1_ref, wk_ref, wv_ref,
                         bias_scr, tril_scr, kprev_scr, vprev_scr, a_scr, s_scr):
    b_idx = pl.program_id(0)
    n_idx = pl.program_id(1)
    n_blocks = PROMPT_STEP // WINDOW

    @pl.when((b_idx == 0) & (n_idx == 0))
    def _init_tables():
        bucket = bucket_ref[...]

        def head_body(h, carry):
            bias_scr[h] = _expand_bias(bucket, table_ref, h)
            return carry

        lax.fori_loop(0, N_Q_HEADS, head_body, 0)
        row = lax.broadcasted_iota(jnp.int32, (CHUNK, CHUNK), 0)
        col = lax.broadcasted_iota(jnp.int32, (CHUNK, CHUNK), 1)
        for g in range(N_SG_GROUPS):
            tril_scr[g] = jnp.where(row >= col, sgw_ref[g], 0.0).astype(BF16)

    @pl.when(n_idx == 0)
    def _reset_carry():
        kprev_scr[...] = jnp.zeros_like(kprev_scr)
        vprev_scr[...] = jnp.zeros_like(vprev_scr)

    x = x_ref[...]
    xb = x.astype(BF16)
    q_bf = (_project(xb, w_in_ref, b_in_ref, 0, Q_END) * ATTN_SCALE).astype(BF16)
    k = _project(xb, w_in_ref, b_in_ref, Q_END, K_END)
    v = _project(xb, w_in_ref, b_in_ref, K_END, V_END)
    k_bf = k.astype(BF16)
    v_bf = v.astype(BF16)

    @pl.when(n_idx == pl.num_programs(1) - 1)
    def _emit_window():
        wk_ref[...] = k[PROMPT_STEP - WINDOW:, :]
        wv_ref[...] = v[PROMPT_STEP - WINDOW:, :]

    row = lax.broadcasted_iota(jnp.int32, (WINDOW, 2 * WINDOW), 0)
    col = lax.broadcasted_iota(jnp.int32, (WINDOW, 2 * WINDOW), 1)
    dist = row + WINDOW - col
    band_valid = (dist >= 0) & (dist <= WINDOW)
    first_valid = band_valid & ((col >= WINDOW) | (n_idx > 0))

    for j in range(n_blocks):
        r0, r1 = j * WINDOW, (j + 1) * WINDOW
        valid = first_valid if j == 0 else band_valid
        for g in range(N_KV_HEADS):
            c0, c1 = g * HEAD_DIM, (g + 1) * HEAD_DIM
            if j == 0:
                kp, vp = kprev_scr[:, c0:c1], vprev_scr[:, c0:c1]
            else:
                kp, vp = k_bf[r0 - WINDOW:r0, c0:c1], v_bf[r0 - WINDOW:r0, c0:c1]
            kb = jnp.concatenate([kp, k_bf[r0:r1, c0:c1]], axis=0)
            vb = jnp.concatenate([vp, v_bf[r0:r1, c0:c1]], axis=0)
            h0 = g * Q_PER_KV
            qs = jnp.concatenate(
                [q_bf[r0:r1, (h0 + r) * HEAD_DIM:(h0 + r + 1) * HEAD_DIM]
                 for r in range(Q_PER_KV)], axis=0)
            logits = _dot_nt(qs, kb).reshape(Q_PER_KV, WINDOW, 2 * WINDOW)
            logits = logits + bias_scr[h0:h0 + Q_PER_KV]
            logits = jnp.where(valid[None], logits, NEG_INF)
            sink = sink_ref[h0:h0 + Q_PER_KV]
            m = jnp.maximum(jnp.max(logits, -1, keepdims=True), sink)
            p = jnp.exp(logits - m)
            den = jnp.sum(p, -1, keepdims=True) + jnp.exp(sink - m)
            o = _dot(p.reshape(Q_PER_KV * WINDOW, 2 * WINDOW).astype(BF16), vb)
            o = o / den.reshape(Q_PER_KV * WINDOW, 1)
            for r in range(Q_PER_KV):
                a_scr[r0:r1, (h0 + r) * HEAD_DIM:(h0 + r + 1) * HEAD_DIM] = (
                    o[r * WINDOW:(r + 1) * WINDOW, :].astype(BF16))

    kprev_scr[...] = k_bf[PROMPT_STEP - WINDOW:, :]
    vprev_scr[...] = v_bf[PROMPT_STEP - WINDOW:, :]

    u = _gelu(_project(xb, w_in_ref, b_in_ref, V_END, U_END))
    vs = _gelu(_project(xb, w_in_ref, b_in_ref, U_END, VS_END))
    vs_bf = _layer_norm(vs, sgg_ref[...], sgb_ref[...]).astype(BF16)
    for j in range(n_blocks):
        r0, r1 = j * WINDOW, (j + 1) * WINDOW
        for g in range(N_SG_GROUPS):
            c0, c1 = g * SG_GROUP_DIM, (g + 1) * SG_GROUP_DIM
            sg = _dot(tril_scr[g], vs_bf[r0:r1, c0:c1]) + sgbias_ref[g]
            s_scr[r0:r1, c0:c1] = (u[r0:r1, c0:c1] * sg).astype(BF16)

    ga = _project(xb, w_in_ref, b_in_ref, VS_END, GA_END)
    gb = _project(xb, w_in_ref, b_in_ref, GA_END, IN_W)
    x1_ref[...] = _merge_and_norm(x, a_scr[...], s_scr[...], ga, gb, wpa_ref, wpb_ref,
                                  wout_ref, ln1g_ref, ln1b_ref)


def _const_spec(shape):
    zeros = (0,) * len(shape)
    return pl.BlockSpec(shape, lambda *_: zeros, pipeline_mode=pl.Buffered(1))


def _mixer_prompt(x, table, sinks3, w_in_bf, b_in, sgg, sgb, sgw, sgbias, wpa, wpb, wout,
                  ln1g, ln1b):
    batch, seq, _ = x.shape
    n_steps = seq // PROMPT_STEP
    dist = np.arange(WINDOW)[:, None] + WINDOW - np.arange(2 * WINDOW)[None, :]
    bucket = jnp.asarray(_t5_bucket_np(dist))
    smem = pl.BlockSpec(memory_space=pltpu.SMEM)
    in_specs = [
        smem,
        _const_spec((WINDOW, 2 * WINDOW)),
        _const_spec((N_Q_HEADS, 1, 1)),
        pl.BlockSpec((None, PROMPT_STEP, D_MODEL), lambda b, n: (b, n, 0)),
        _const_spec((D_MODEL, IN_W)),
        _const_spec((1, IN_W)),
        _const_spec((1, SG_WIDTH)),
        _const_spec((1, SG_WIDTH)),
        _const_spec((N_SG_GROUPS, CHUNK, CHUNK)),
        _const_spec((N_SG_GROUPS, CHUNK, 1)),
        _const_spec((Q_W, D_MODEL)),
        _const_spec((SG_WIDTH, D_MODEL)),
        _const_spec((D_MODEL, D_MODEL)),
        _const_spec((1, D_MODEL)),
        _const_spec((1, D_MODEL)),
    ]
    out_specs = [
        pl.BlockSpec((None, PROMPT_STEP, D_MODEL), lambda b, n: (b, n, 0)),
        pl.BlockSpec((None, WINDOW, KV_W), lambda b, n: (b, 0, 0)),
        pl.BlockSpec((None, WINDOW, KV_W), lambda b, n: (b, 0, 0)),
    ]
    out_shape = [
        jax.ShapeDtypeStruct((batch, seq, D_MODEL), F32),
        jax.ShapeDtypeStruct((batch, WINDOW, KV_W), F32),
        jax.ShapeDtypeStruct((batch, WINDOW, KV_W), F32),
    ]
    scratch = [
        pltpu.VMEM((N_Q_HEADS, WINDOW, 2 * WINDOW), F32),
        pltpu.VMEM((N_SG_GROUPS, CHUNK, CHUNK), BF16),
        pltpu.VMEM((WINDOW, KV_W), BF16),
        pltpu.VMEM((WINDOW, KV_W), BF16),
        pltpu.VMEM((PROMPT_STEP, Q_W), BF16),
        pltpu.VMEM((PROMPT_STEP, SG_WIDTH), BF16),
    ]
    return pl.pallas_call(
        _mixer_prompt_kernel,
        grid=(batch, n_steps),
        in_specs=in_specs,
        out_specs=out_specs,
        out_shape=out_shape,
        scratch_shapes=scratch,
        compiler_params=pltpu.CompilerParams(
            dimension_semantics=("arbitrary", "arbitrary"),
            vmem_limit_bytes=VMEM_LIMIT_BYTES),
        name="mixer_prompt",
    )(table, bucket, sinks3, x, w_in_bf, b_in, sgg, sgb, sgw, sgbias, wpa, wpb, wout,
      ln1g, ln1b)


def _mixer_sample_kernel(table_ref, bucket_c_ref, bucket_n_ref, sink_ref, x_ref, ck_ref, cv_ref,
                         w_in_ref, b_in_ref, sgg_ref, sgb_ref, sgw8_ref, sgbias8_ref,
                         wpa_ref, wpb_ref, wout_ref, ln1g_ref, ln1b_ref,
                         x1_ref, wk_ref, wv_ref, cvs_ref,
                         bias_c_scr, bias_n_scr, a_scr, s_scr):
    nseq, ds = SAMPLE_SEQS_PER_STEP, x_ref.shape[1]
    rows = nseq * ds
    win = ck_ref.shape[1]
    qrows = Q_PER_KV * ds

    @pl.when(pl.program_id(0) == 0)
    def _init_tables():
        bc = bucket_c_ref[...]
        bn = bucket_n_ref[...]
        for h in range(N_Q_HEADS):
            g, r = divmod(h, Q_PER_KV)
            bias_c_scr[g, r * ds:(r + 1) * ds, :] = _expand_bias(bc, table_ref, h)
            bias_n_scr[g, r * ds:(r + 1) * ds, :] = _expand_bias(bn, table_ref, h)

    x = x_ref[...].reshape(rows, D_MODEL)
    xb = x.astype(BF16)
    q = _project(xb, w_in_ref, b_in_ref, 0, Q_END) * ATTN_SCALE
    k = _project(xb, w_in_ref, b_in_ref, Q_END, K_END)
    v = _project(xb, w_in_ref, b_in_ref, K_END, V_END)
    ck = ck_ref[...]
    cv = cv_ref[...]
    wk_ref[:, :win - ds, :] = ck[:, ds:, :]
    wk_ref[:, win - ds:, :] = k.reshape(nseq, ds, KV_W)
    wv_ref[:, :win - ds, :] = cv[:, ds:, :]
    wv_ref[:, win - ds:, :] = v.reshape(nseq, ds, KV_W)
    nk = wk_ref[...]
    nv = wv_ref[...]

    q3 = q.reshape(nseq, ds, Q_W)
    t_q = lax.broadcasted_iota(jnp.int32, (qrows, win), 0) % ds
    col = lax.broadcasted_iota(jnp.int32, (qrows, win), 1)
    dist_c = t_q + win - col
    valid_c = (dist_c >= 0) & (dist_c <= WINDOW)
    dist_n = t_q - (col - (win - ds))
    valid_n = (col >= win - ds) & (dist_n >= 0) & (dist_n <= WINDOW)

    for g in range(N_KV_HEADS):
        c0, c1 = g * HEAD_DIM, (g + 1) * HEAD_DIM
        h0 = g * Q_PER_KV
        qs = jnp.concatenate(
            [q3[:, :, (h0 + r) * HEAD_DIM:(h0 + r + 1) * HEAD_DIM] for r in range(Q_PER_KV)],
            axis=1).astype(BF16)
        kc = ck[:, :, c0:c1].astype(BF16)
        vc = cv[:, :, c0:c1].astype(BF16)
        kn = nk[:, :, c0:c1].astype(BF16)
        vn = nv[:, :, c0:c1].astype(BF16)
        lc = jnp.einsum('bqd,bkd->bqk', qs, kc, preferred_element_type=F32)
        ln = jnp.einsum('bqd,bkd->bqk', qs, kn, preferred_element_type=F32)
        lc = jnp.where(valid_c[None], lc + bias_c_scr[g][None], NEG_INF)
        ln = jnp.where(valid_n[None], ln + bias_n_scr[g][None], NEG_INF)
        sink = sink_ref[g]
        m = jnp.maximum(jnp.maximum(jnp.max(lc, -1, keepdims=True),
                                    jnp.max(ln, -1, keepdims=True)), sink[None])
        pc = jnp.exp(lc - m)
        pn = jnp.exp(ln - m)
        den = (jnp.sum(pc, -1, keepdims=True) + jnp.sum(pn, -1, keepdims=True)
               + jnp.exp(sink[None] - m))
        o = (jnp.einsum('bqk,bkd->bqd', pc.astype(BF16), vc, preferred_element_type=F32)
             + jnp.einsum('bqk,bkd->bqd', pn.astype(BF16), vn, preferred_element_type=F32))
        o = o / den
        for r in range(Q_PER_KV):
            a_scr[:, :, (h0 + r) * HEAD_DIM:(h0 + r + 1) * HEAD_DIM] = (
                o[:, r * ds:(r + 1) * ds, :])

    u = _gelu(_project(xb, w_in_ref, b_in_ref, V_END, U_END))
    vs = _gelu(_project(xb, w_in_ref, b_in_ref, U_END, VS_END))
    vs_ln = _layer_norm(vs, sgg_ref[...], sgb_ref[...])
    cvs_ref[...] = vs_ln.reshape(nseq, ds, SG_WIDTH)
    vq = vs_ln.astype(BF16).astype(F32).reshape(nseq, ds, SG_WIDTH)
    u3 = u.reshape(nseq, ds, SG_WIDTH)
    i_row = lax.broadcasted_iota(jnp.int32, (ds, 1), 0)
    for g in range(N_SG_GROUPS):
        c0, c1 = g * SG_GROUP_DIM, (g + 1) * SG_GROUP_DIM
        acc = jnp.broadcast_to(sgbias8_ref[g][None], (nseq, ds, SG_GROUP_DIM))
        for j in range(ds):
            w_col = jnp.where(i_row >= j, sgw8_ref[g, j], 0.0)
            w_col = w_col.astype(BF16).astype(F32)
            acc = acc + w_col[None] * vq[:, j:j + 1, c0:c1]
        s_scr[:, :, c0:c1] = u3[:, :, c0:c1] * acc

    ga = _project(xb, w_in_ref, b_in_ref, VS_END, GA_END)
    gb = _project(xb, w_in_ref, b_in_ref, GA_END, IN_W)
    x1 = _merge_and_norm(x, a_scr[...].reshape(rows, Q_W).astype(BF16),
                         s_scr[...].reshape(rows, SG_WIDTH).astype(BF16),
                         ga, gb, wpa_ref, wpb_ref, wout_ref, ln1g_ref, ln1b_ref)
    x1_ref[...] = x1.reshape(nseq, ds, D_MODEL)


def _mixer_sample(x, cache_k, cache_v, table, sinks, w_in_bf, b_in, sgg, sgb, sgw, sgbias,
                  wpa, wpb, wout, ln1g, ln1b):
    nb, ds, _ = x.shape
    win = cache_k.shape[1]
    nseq = SAMPLE_SEQS_PER_STEP
    qrows = Q_PER_KV * ds
    t = np.arange(ds)[:, None]
    bucket_c = jnp.asarray(_t5_bucket_np(t + win - np.arange(win)[None, :]))
    bucket_n = jnp.asarray(_t5_bucket_np(t - (np.arange(win)[None, :] - (win - ds))))
    sink_rows = jnp.repeat(sinks.reshape(N_KV_HEADS, Q_PER_KV), ds, axis=1).reshape(
        N_KV_HEADS, qrows, 1)
    sgw8 = jnp.transpose(sgw[:, :ds, :ds], (0, 2, 1))[..., None]
    sgbias8 = sgbias[:, :ds, :]
    smem = pl.BlockSpec(memory_space=pltpu.SMEM)
    in_specs = [
        smem,
        _const_spec((ds, win)),
        _const_spec((ds, win)),
        _const_spec((N_KV_HEADS, qrows, 1)),
        pl.BlockSpec((nseq, ds, D_MODEL), lambda i: (i, 0, 0)),
        pl.BlockSpec((nseq, win, KV_W), lambda i: (i, 0, 0)),
        pl.BlockSpec((nseq, win, KV_W), lambda i: (i, 0, 0)),
        _const_spec((D_MODEL, IN_W)),
        _const_spec((1, IN_W)),
        _const_spec((1, SG_WIDTH)),
        _const_spec((1, SG_WIDTH)),
        _const_spec((N_SG_GROUPS, ds, ds, 1)),
        _const_spec((N_SG_GROUPS, ds, 1)),
        _const_spec((Q_W, D_MODEL)),
        _const_spec((SG_WIDTH, D_MODEL)),
        _const_spec((D_MODEL, D_MODEL)),
        _const_spec((1, D_MODEL)),
        _const_spec((1, D_MODEL)),
    ]
    out_specs = [
        pl.BlockSpec((nseq, ds, D_MODEL), lambda i: (i, 0, 0)),
        pl.BlockSpec((nseq, win, KV_W), lambda i: (i, 0, 0)),
        pl.BlockSpec((nseq, win, KV_W), lambda i: (i, 0, 0)),
        pl.BlockSpec((nseq, ds, SG_WIDTH), lambda i: (i, 0, 0)),
    ]
    out_shape = [
        jax.ShapeDtypeStruct((nb, ds, D_MODEL), F32),
        jax.ShapeDtypeStruct((nb, win, KV_W), F32),
        jax.ShapeDtypeStruct((nb, win, KV_W), F32),
        jax.ShapeDtypeStruct((nb, ds, SG_WIDTH), F32),
    ]
    scratch = [
        pltpu.VMEM((N_KV_HEADS, qrows, win), F32),
        pltpu.VMEM((N_KV_HEADS, qrows, win), F32),
        pltpu.VMEM((nseq, ds, Q_W), F32),
        pltpu.VMEM((nseq, ds, SG_WIDTH), F32),
    ]
    return pl.pallas_call(
        _mixer_sample_kernel,
        grid=(nb // nseq,),
        in_specs=in_specs,
        out_specs=out_specs,
        out_shape=out_shape,
        scratch_shapes=scratch,
        compiler_params=pltpu.CompilerParams(
            dimension_semantics=("arbitrary",),
            vmem_limit_bytes=VMEM_LIMIT_BYTES),
        name="mixer_sample",
    )(table, bucket_c, bucket_n, sink_rows, x, cache_k, cache_v, w_in_bf, b_in, sgg, sgb,
      sgw8, sgbias8, wpa, wpb, wout, ln1g, ln1b)


def _route(xb, wrt_ref, rbias_ref):
    n = xb.shape[0]
    logits = _dot_nt(wrt_ref[...].astype(BF16), xb)
    scores = jax.nn.sigmoid(logits)
    sel = scores + rbias_ref[...]
    shape3 = (N_EXPERT_GROUPS, EXPERTS_PER_GROUP, n)
    scores3 = scores.reshape(shape3)
    sel3 = sel.reshape(shape3)
    i_in = lax.broadcasted_iota(jnp.int32, shape3, 1)
    g_id = lax.broadcasted_iota(jnp.int32, shape3, 0)
    e_id = g_id * EXPERTS_PER_GROUP + i_in
    neg = -jnp.inf

    m1 = jnp.max(sel3, axis=1, keepdims=True)
    first = jnp.min(jnp.where(sel3 == m1, i_in, EXPERTS_PER_GROUP), axis=1, keepdims=True)
    m2 = jnp.max(jnp.where(i_in == first, neg, sel3), axis=1, keepdims=True)
    gscore = m1 + m2

    gsel = jnp.zeros(gscore.shape, jnp.bool_)
    gid1 = lax.broadcasted_iota(jnp.int32, gscore.shape, 0)
    for _ in range(TOPK_GROUPS):
        m = jnp.max(gscore, axis=0, keepdims=True)
        pick = jnp.min(jnp.where(gscore == m, gid1, N_EXPERT_GROUPS), axis=0, keepdims=True)
        chosen = gid1 == pick
        gsel = gsel | chosen
        gscore = jnp.where(chosen, neg, gscore)
    val = jnp.where(gsel, sel3, NEG_INF)

    esel = jnp.zeros(shape3, jnp.bool_)
    for _ in range(TOP_K):
        m = jnp.max(jnp.max(val, axis=0, keepdims=True), axis=1, keepdims=True)
        cand = jnp.where(val == m, e_id, N_EXPERTS)
        pick = jnp.min(jnp.min(cand, axis=0, keepdims=True), axis=1, keepdims=True)
        chosen = e_id == pick
        esel = esel | chosen
        val = jnp.where(chosen, neg, val)
    w_sel = jnp.where(esel, scores3, 0.0)
    total = jnp.sum(jnp.sum(w_sel, axis=0, keepdims=True), axis=1, keepdims=True)
    gates = w_sel / total * ROUTED_SCALE
    return gates.reshape(N_EXPERTS, n)


def _swiglu(xb, wg, wu):
    return jax.nn.silu(_dot(xb, wg)) * _dot(xb, wu)


def _moe_dense_kernel(x_ref, wrt_ref, rbias_ref, wg_ref, wu_ref, wd_ref, wgs_ref, wus_ref,
                      wds_ref, ln2g_ref, ln2b_ref, out_ref, xb_scr, acc_scr, gates_scr):
    e = pl.program_id(1)

    @pl.when(e == 0)
    def _start_tile():
        xb = x_ref[...].astype(BF16)
        xb_scr[...] = xb
        gates_t = _route(xb, wrt_ref, rbias_ref)
        pad = jnp.zeros((128 - N_EXPERTS, gates_t.shape[1]), F32)
        gates_scr[...] = jnp.concatenate([gates_t, pad], axis=0).T
        hs = _swiglu(xb, wgs_ref[...].astype(BF16), wus_ref[...].astype(BF16))
        acc_scr[...] = _dot(hs.astype(BF16), wds_ref[...].astype(BF16))

    xb = xb_scr[...]
    lane = lax.broadcasted_iota(jnp.int32, gates_scr.shape, 1)
    g_e = jnp.sum(jnp.where(lane == e, gates_scr[...], 0.0), axis=-1, keepdims=True)
    h = _swiglu(xb, wg_ref[...].astype(BF16), wu_ref[...].astype(BF16))
    acc_scr[...] += _dot((h * g_e).astype(BF16), wd_ref[...].astype(BF16))

    @pl.when(e == pl.num_programs(1) - 1)
    def _finish_tile():
        out_ref[...] = _layer_norm(ALPHA * x_ref[...] + acc_scr[...], ln2g_ref[...],
                                   ln2b_ref[...])


def _moe_dense(x1, wrt, rbias, wg, wu, wd, wgs, wus, wds, ln2g, ln2b):
    n_tok = x1.shape[0]
    tile = min(MOE_TILE, n_tok)
    assert n_tok % tile == 0
    in_specs = [
        pl.BlockSpec((tile, D_MODEL), lambda t, e: (t, 0)),
        _const_spec((N_EXPERTS, D_MODEL)),
        _const_spec((N_EXPERTS, 1)),
        pl.BlockSpec((None, D_MODEL, EXPERT_DIM), lambda t, e: (e, 0, 0)),
        pl.BlockSpec((None, D_MODEL, EXPERT_DIM), lambda t, e: (e, 0, 0)),
        pl.BlockSpec((None, EXPERT_DIM, D_MODEL), lambda t, e: (e, 0, 0)),
        _const_spec((D_MODEL, SHARED_DIM)),
        _const_spec((D_MODEL, SHARED_DIM)),
        _const_spec((SHARED_DIM, D_MODEL)),
        _const_spec((1, D_MODEL)),
        _const_spec((1, D_MODEL)),
    ]
    return pl.pallas_call(
        _moe_dense_kernel,
        grid=(n_tok // tile, N_EXPERTS),
        in_specs=in_specs,
        out_specs=pl.BlockSpec((tile, D_MODEL), lambda t, e: (t, 0)),
        out_shape=jax.ShapeDtypeStruct((n_tok, D_MODEL), F32),
        scratch_shapes=[
            pltpu.VMEM((tile, D_MODEL), BF16),
            pltpu.VMEM((tile, D_MODEL), F32),
            pltpu.VMEM((tile, 128), F32),
        ],
        compiler_params=pltpu.CompilerParams(
            dimension_semantics=("arbitrary", "arbitrary"),
            vmem_limit_bytes=VMEM_LIMIT_BYTES),
        name="moe_dense",
    )(x1, wrt, rbias, wg, wu, wd, wgs, wus, wds, ln2g, ln2b)


def kernel(x_prompt, x_sample, cache_win_k, cache_win_v, rel_bias_table, w_in, b_in, attn_sinks,
           sg_ln_g, sg_ln_b, sg_w, sg_b, w_proj_a, w_proj_b, w_out, ln1_g, ln1_b, w_router,
           router_bias, w_gate_e, w_up_e, w_down_e, w_gate_s, w_up_s, w_down_s, ln2_g, ln2_b):
    assert DEPTH == 1 and w_in.shape[0] == 1
    batch, seq, _ = x_prompt.shape
    nb, ds, _ = x_sample.shape
    win = cache_win_k.shape[2]

    w_in_bf = w_in[0].astype(BF16)
    wpa = w_proj_a[0].astype(BF16)
    wpb = w_proj_b[0].astype(BF16)
    wout = w_out[0].astype(BF16)
    b_in2 = b_in[0].reshape(1, IN_W)
    sgg = sg_ln_g[0].reshape(1, SG_WIDTH)
    sgb = sg_ln_b[0].reshape(1, SG_WIDTH)
    sgw = sg_w[0]
    sgbias = sg_b[0].reshape(N_SG_GROUPS, CHUNK, 1)
    ln1g = ln1_g[0].reshape(1, D_MODEL)
    ln1b = ln1_b[0].reshape(1, D_MODEL)
    sinks = attn_sinks[0]

    x1_p, wk_p, wv_p = _mixer_prompt(
        x_prompt, rel_bias_table, sinks.reshape(N_Q_HEADS, 1, 1), w_in_bf, b_in2, sgg, sgb, sgw,
        sgbias, wpa, wpb, wout, ln1g, ln1b)
    x1_s, wk_s, wv_s, cvs = _mixer_sample(
        x_sample, cache_win_k[0].reshape(nb, win, KV_W), cache_win_v[0].reshape(nb, win, KV_W),
        rel_bias_table, sinks, w_in_bf, b_in2, sgg, sgb, sgw, sgbias, wpa, wpb, wout, ln1g, ln1b)

    moe = functools.partial(
        _moe_dense,
        wrt=w_router[0].T, rbias=router_bias[0].reshape(N_EXPERTS, 1),
        wg=w_gate_e[0], wu=w_up_e[0], wd=w_down_e[0],
        wgs=w_gate_s[0], wus=w_up_s[0], wds=w_down_s[0],
        ln2g=ln2_g[0].reshape(1, D_MODEL), ln2b=ln2_b[0].reshape(1, D_MODEL))
    y_p = moe(x1_p.reshape(batch * seq, D_MODEL)).reshape(batch, seq, D_MODEL)
    y_s = moe(x1_s.reshape(nb * ds, D_MODEL)).reshape(nb, ds, D_MODEL)

    kv_shape = (1, -1, WINDOW, N_KV_HEADS, HEAD_DIM)
    return (y_p, y_s,
            wk_p.reshape(kv_shape), wv_p.reshape(kv_shape),
            wk_s.reshape(1, nb, win, N_KV_HEADS, HEAD_DIM),
            wv_s.reshape(1, nb, win, N_KV_HEADS, HEAD_DIM),
            cvs.reshape(1, nb, ds, N_SG_GROUPS, SG_GROUP_DIM))
```

```python
import functools
import math

import jax
import jax.numpy as jnp
import numpy as np
from jax import lax
from jax.experimental import pallas as pl
from jax.experimental.pallas import tpu as pltpu

F32 = jnp.float32
BF16 = jnp.bfloat16

D_MODEL = 1024
DEPTH = 1
HEAD_DIM = 64
N_Q_HEADS = 16
N_KV_HEADS = 2
Q_PER_KV = N_Q_HEADS // N_KV_HEADS
WINDOW = 128
ATTN_SCALE = HEAD_DIM ** -0.5
NEG_INF = -1e30
N_BUCKETS = 32
BUCKET_MAX_EXACT = 16
BUCKET_MAX_DIST = 128
CHUNK = 128
N_SG_GROUPS = 4
SG_GROUP_DIM = 128
SG_WIDTH = N_SG_GROUPS * SG_GROUP_DIM
Q_W = N_Q_HEADS * HEAD_DIM
KV_W = N_KV_HEADS * HEAD_DIM
Q_END = Q_W
K_END = Q_END + KV_W
V_END = K_END + KV_W
U_END = V_END + SG_WIDTH
VS_END = U_END + SG_WIDTH
GA_END = VS_END + D_MODEL
IN_W = GA_END + D_MODEL
N_EXPERTS = 64
TOP_K = 8
N_EXPERT_GROUPS = 8
EXPERTS_PER_GROUP = N_EXPERTS // N_EXPERT_GROUPS
TOPK_GROUPS = 4
EXPERT_DIM = 256
SHARED_DIM = 256
ROUTED_SCALE = 2.5
ALPHA = (2 * DEPTH) ** 0.25
LN_EPS = 1e-5

VMEM_LIMIT_BYTES = 56 * 1024 * 1024

PROMPT_STEP = 512
SAMPLE_SEQS_PER_STEP = 32
MOE_TILE = 256
RUN_ALIGN = 16
EXPERT_BLOCK = 512
TILE_SLOTS = -(-(MOE_TILE * TOP_K + N_EXPERTS * (RUN_ALIGN - 1)) // 512) * 512


def _t5_bucket_np(dist):
    d = np.maximum(dist, 0)
    ratio = np.maximum(d, 1).astype(np.float32) / np.float32(BUCKET_MAX_EXACT)
    large = BUCKET_MAX_EXACT + (
        np.log(ratio) / np.float32(math.log(BUCKET_MAX_DIST / BUCKET_MAX_EXACT))
        * np.float32(N_BUCKETS - BUCKET_MAX_EXACT)).astype(np.int32)
    large = np.minimum(large, N_BUCKETS - 1)
    return np.where(d < BUCKET_MAX_EXACT, d, large).astype(np.int32)


def _layer_norm(x, g, b):
    mu = jnp.mean(x, -1, keepdims=True)
    xc = x - mu
    var = jnp.mean(xc * xc, -1, keepdims=True)
    return xc * lax.rsqrt(var + LN_EPS) * g + b


def _gelu(x):
    return jax.nn.gelu(x)


def _dot(a, b):
    return jnp.dot(a, b, preferred_element_type=F32)


def _dot_nt(a, b):
    return lax.dot_general(a, b, (((1,), (1,)), ((), ())), preferred_element_type=F32)


def _project(xb, w_in_ref, b_in_ref, lo, hi):
    return _dot(xb, w_in_ref[:, lo:hi]) + b_in_ref[:, lo:hi]


def _expand_bias(bucket, table_ref, head):
    acc = jnp.zeros(bucket.shape, F32)
    for b in range(N_BUCKETS):
        acc = jnp.where(bucket == b, table_ref[b, head], acc)
    return acc


def _merge_and_norm(x, a_bf, s_bf, ga, gb, wpa_ref, wpb_ref, wout_ref, g_ref, b_ref):
    pa = _dot(a_bf, wpa_ref[...])
    pb = _dot(s_bf, wpb_ref[...])
    hpre = jax.nn.sigmoid(ga) * pa + jax.nn.sigmoid(gb) * pb
    h = _dot(hpre.astype(BF16), wout_ref[...])
    return _layer_norm(ALPHA * x + h, g_ref[...], b_ref[...])


def _mixer_prompt_kernel(table_ref, bucket_ref, sink_ref, x_ref, w_in_ref, b_in_ref,
                         sgg_ref, sgb_ref, sgw_ref, sgbias_ref, wpa_ref, wpb_ref, wout_ref,
                         ln1g_ref, ln1b_ref,
                         x1_ref, wk_ref, wv_ref,
                         bias_scr, tril_scr, kprev_scr, vprev_scr, a_scr, s_scr):
    b_idx = pl.program_id(0)
    n_idx = pl.program_id(1)
    n_blocks = PROMPT_STEP // WINDOW

    @pl.when((b_idx == 0) & (n_idx == 0))
    def _init_tables():
        bucket = bucket_ref[...]

        def head_body(h, carry):
            bias_scr[h] = _expand_bias(bucket, table_ref, h)
            return carry

        lax.fori_loop(0, N_Q_HEADS, head_body, 0)
        row = lax.broadcasted_iota(jnp.int32, (CHUNK, CHUNK), 0)
        col = lax.broadcasted_iota(jnp.int32, (CHUNK, CHUNK), 1)
        for g in range(N_SG_GROUPS):
            tril_scr[g] = jnp.where(row >= col, sgw_ref[g], 0.0).astype(BF16)

    @pl.when(n_idx == 0)
    def _reset_carry():
        kprev_scr[...] = jnp.zeros_like(kprev_scr)
        vprev_scr[...] = jnp.zeros_like(vprev_scr)

    x = x_ref[...]
    xb = x.astype(BF16)
    q_bf = (_project(xb, w_in_ref, b_in_ref, 0, Q_END) * ATTN_SCALE).astype(BF16)
    k = _project(xb, w_in_ref, b_in_ref, Q_END, K_END)
    v = _project(xb, w_in_ref, b_in_ref, K_END, V_END)
    k_bf = k.astype(BF16)
    v_bf = v.astype(BF16)

    @pl.when(n_idx == pl.num_programs(1) - 1)
    def _emit_window():
        wk_ref[...] = k[PROMPT_STEP - WINDOW:, :]
        wv_ref[...] = v[PROMPT_STEP - WINDOW:, :]

    row = lax.broadcasted_iota(jnp.int32, (WINDOW, 2 * WINDOW), 0)
    col = lax.broadcasted_iota(jnp.int32, (WINDOW, 2 * WINDOW), 1)
    dist = row + WINDOW - col
    band_valid = (dist >= 0) & (dist <= WINDOW)
    first_valid = band_valid & ((col >= WINDOW) | (n_idx > 0))

    for j in range(n_blocks):
        r0, r1 = j * WINDOW, (j + 1) * WINDOW
        valid = first_valid if j == 0 else band_valid
        for g in range(N_KV_HEADS):
            c0, c1 = g * HEAD_DIM, (g + 1) * HEAD_DIM
            if j == 0:
                kp, vp = kprev_scr[:, c0:c1], vprev_scr[:, c0:c1]
            else:
                kp, vp = k_bf[r0 - WINDOW:r0, c0:c1], v_bf[r0 - WINDOW:r0, c0:c1]
            kb = jnp.concatenate([kp, k_bf[r0:r1, c0:c1]], axis=0)
            vb = jnp.concatenate([vp, v_bf[r0:r1, c0:c1]], axis=0)
            h0 = g * Q_PER_KV
            qs = jnp.concatenate(
                [q_bf[r0:r1, (h0 + r) * HEAD_DIM:(h0 + r + 1) * HEAD_DIM]
                 for r in range(Q_PER_KV)], axis=0)
            logits = _dot_nt(qs, kb).reshape(Q_PER_KV, WINDOW, 2 * WINDOW)
            logits = logits + bias_scr[h0:h0 + Q_PER_KV]
            logits = jnp.where(valid[None], logits, NEG_INF)
            sink = sink_ref[h0:h0 + Q_PER_KV]
            m = jnp.maximum(jnp.max(logits, -1, keepdims=True), sink)
            p = jnp.exp(logits - m)
            den = jnp.sum(p, -1, keepdims=True) + jnp.exp(sink - m)
            o = _dot(p.reshape(Q_PER_KV * WINDOW, 2 * WINDOW).astype(BF16), vb)
            o = o / den.reshape(Q_PER_KV * WINDOW, 1)
            for r in range(Q_PER_KV):
                a_scr[r0:r1, (h0 + r) * HEAD_DIM:(h0 + r + 1) * HEAD_DIM] = (
                    o[r * WINDOW:(r + 1) * WINDOW, :].astype(BF16))

    kprev_scr[...] = k_bf[PROMPT_STEP - WINDOW:, :]
    vprev_scr[...] = v_bf[PROMPT_STEP - WINDOW:, :]

    u = _gelu(_project(xb, w_in_ref, b_in_ref, V_END, U_END))
    vs = _gelu(_project(xb, w_in_ref, b_in_ref, U_END, VS_END))
    vs_bf = _layer_norm(vs, sgg_ref[...], sgb_ref[...]).astype(BF16)
    for j in range(n_blocks):
        r0, r1 = j * WINDOW, (j + 1) * WINDOW
        for g in range(N_SG_GROUPS):
            c0, c1 = g * SG_GROUP_DIM, (g + 1) * SG_GROUP_DIM
            sg = _dot(tril_scr[g], vs_bf[r0:r1, c0:c1]) + sgbias_ref[g]
            s_scr[r0:r1, c0:c1] = (u[r0:r1, c0:c1] * sg).astype(BF16)

    ga = _project(xb, w_in_ref, b_in_ref, VS_END, GA_END)
    gb = _project(xb, w_in_ref, b_in_ref, GA_END, IN_W)
    x1_ref[...] = _merge_and_norm(x, a_scr[...], s_scr[...], ga, gb, wpa_ref, wpb_ref,
                                  wout_ref, ln1g_ref, ln1b_ref)


def _const_spec(shape):
    zeros = (0,) * len(shape)
    return pl.BlockSpec(shape, lambda *_: zeros, pipeline_mode=pl.Buffered(1))


def _mixer_prompt(x, table, sinks3, w_in_bf, b_in, sgg, sgb, sgw, sgbias, wpa, wpb, wout,
                  ln1g, ln1b):
    batch, seq, _ = x.shape
    n_steps = seq // PROMPT_STEP
    dist = np.arange(WINDOW)[:, None] + WINDOW - np.arange(2 * WINDOW)[None, :]
    bucket = jnp.asarray(_t5_bucket_np(dist))
    smem = pl.BlockSpec(memory_space=pltpu.SMEM)
    in_specs = [
        smem,
        _const_spec((WINDOW, 2 * WINDOW)),
        _const_spec((N_Q_HEADS, 1, 1)),
        pl.BlockSpec((None, PROMPT_STEP, D_MODEL), lambda b, n: (b, n, 0)),
        _const_spec((D_MODEL, IN_W)),
        _const_spec((1, IN_W)),
        _const_spec((1, SG_WIDTH)),
        _const_spec((1, SG_WIDTH)),
        _const_spec((N_SG_GROUPS, CHUNK, CHUNK)),
        _const_spec((N_SG_GROUPS, CHUNK, 1)),
        _const_spec((Q_W, D_MODEL)),
        _const_spec((SG_WIDTH, D_MODEL)),
        _const_spec((D_MODEL, D_MODEL)),
        _const_spec((1, D_MODEL)),
        _const_spec((1, D_MODEL)),
    ]
    out_specs = [
        pl.BlockSpec((None, PROMPT_STEP, D_MODEL), lambda b, n: (b, n, 0)),
        pl.BlockSpec((None, WINDOW, KV_W), lambda b, n: (b, 0, 0)),
        pl.BlockSpec((None, WINDOW, KV_W), lambda b, n: (b, 0, 0)),
    ]
    out_shape = [
        jax.ShapeDtypeStruct((batch, seq, D_MODEL), F32),
        jax.ShapeDtypeStruct((batch, WINDOW, KV_W), F32),
        jax.ShapeDtypeStruct((batch, WINDOW, KV_W), F32),
    ]
    scratch = [
        pltpu.VMEM((N_Q_HEADS, WINDOW, 2 * WINDOW), F32),
        pltpu.VMEM((N_SG_GROUPS, CHUNK, CHUNK), BF16),
        pltpu.VMEM((WINDOW, KV_W), BF16),
        pltpu.VMEM((WINDOW, KV_W), BF16),
        pltpu.VMEM((PROMPT_STEP, Q_W), BF16),
        pltpu.VMEM((PROMPT_STEP, SG_WIDTH), BF16),
    ]
    return pl.pallas_call(
        _mixer_prompt_kernel,
        grid=(batch, n_steps),
        in_specs=in_specs,
        out_specs=out_specs,
        out_shape=out_shape,
        scratch_shapes=scratch,
        compiler_params=pltpu.CompilerParams(
            dimension_semantics=("arbitrary", "arbitrary"),
            vmem_limit_bytes=VMEM_LIMIT_BYTES),
        name="mixer_prompt",
    )(table, bucket, sinks3, x, w_in_bf, b_in, sgg, sgb, sgw, sgbias, wpa, wpb, wout,
      ln1g, ln1b)


def _mixer_sample_kernel(table_ref, bucket_c_ref, bucket_n_ref, sink_ref, x_ref, ck_ref, cv_ref,
                         w_in_ref, b_in_ref, sgg_ref, sgb_ref, sgw8_ref, sgbias8_ref,
                         wpa_ref, wpb_ref, wout_ref, ln1g_ref, ln1b_ref,
                         x1_ref, wk_ref, wv_ref, cvs_ref,
                         bias_c_scr, bias_n_scr, a_scr, s_scr):
    nseq, ds = SAMPLE_SEQS_PER_STEP, x_ref.shape[1]
    rows = nseq * ds
    win = ck_ref.shape[1]
    qrows = Q_PER_KV * ds

    @pl.when(pl.program_id(0) == 0)
    def _init_tables():
        bc = bucket_c_ref[...]
        bn = bucket_n_ref[...]
        for h in range(N_Q_HEADS):
            g, r = divmod(h, Q_PER_KV)
            bias_c_scr[g, r * ds:(r + 1) * ds, :] = _expand_bias(bc, table_ref, h)
            bias_n_scr[g, r * ds:(r + 1) * ds, :] = _expand_bias(bn, table_ref, h)

    x = x_ref[...].reshape(rows, D_MODEL)
    xb = x.astype(BF16)
    q = _project(xb, w_in_ref, b_in_ref, 0, Q_END) * ATTN_SCALE
    k = _project(xb, w_in_ref, b_in_ref, Q_END, K_END)
    v = _project(xb, w_in_ref, b_in_ref, K_END, V_END)
    ck = ck_ref[...]
    cv = cv_ref[...]
    wk_ref[:, :win - ds, :] = ck[:, ds:, :]
    wk_ref[:, win - ds:, :] = k.reshape(nseq, ds, KV_W)
    wv_ref[:, :win - ds, :] = cv[:, ds:, :]
    wv_ref[:, win - ds:, :] = v.reshape(nseq, ds, KV_W)
    nk = wk_ref[...]
    nv = wv_ref[...]

    q3 = q.reshape(nseq, ds, Q_W)
    t_q = lax.broadcasted_iota(jnp.int32, (qrows, win), 0) % ds
    col = lax.broadcasted_iota(jnp.int32, (qrows, win), 1)
    dist_c = t_q + win - col
    valid_c = (dist_c >= 0) & (dist_c <= WINDOW)
    dist_n = t_q - (col - (win - ds))
    valid_n = (col >= win - ds) & (dist_n >= 0) & (dist_n <= WINDOW)

    for g in range(N_KV_HEADS):
        c0, c1 = g * HEAD_DIM, (g + 1) * HEAD_DIM
        h0 = g * Q_PER_KV
        qs = jnp.concatenate(
            [q3[:, :, (h0 + r) * HEAD_DIM:(h0 + r + 1) * HEAD_DIM] for r in range(Q_PER_KV)],
            axis=1).astype(BF16)
        kc = ck[:, :, c0:c1].astype(BF16)
        vc = cv[:, :, c0:c1].astype(BF16)
        kn = nk[:, :, c0:c1].astype(BF16)
        vn = nv[:, :, c0:c1].astype(BF16)
        lc = jnp.einsum('bqd,bkd->bqk', qs, kc, preferred_element_type=F32)
        ln = jnp.einsum('bqd,bkd->bqk', qs, kn, preferred_element_type=F32)
        lc = jnp.where(valid_c[None], lc + bias_c_scr[g][None], NEG_INF)
        ln = jnp.where(valid_n[None], ln + bias_n_scr[g][None], NEG_INF)
        sink = sink_ref[g]
        m = jnp.maximum(jnp.maximum(jnp.max(lc, -1, keepdims=True),
                                    jnp.max(ln, -1, keepdims=True)), sink[None])
        pc = jnp.exp(lc - m)
        pn = jnp.exp(ln - m)
        den = (jnp.sum(pc, -1, keepdims=True) + jnp.sum(pn, -1, keepdims=True)
               + jnp.exp(sink[None] - m))
        o = (jnp.einsum('bqk,bkd->bqd', pc.astype(BF16), vc, preferred_element_type=F32)
             + jnp.einsum('bqk,bkd->bqd', pn.astype(BF16), vn, preferred_element_type=F32))
        o = o / den
        for r in range(Q_PER_KV):
            a_scr[:, :, (h0 + r) * HEAD_DIM:(h0 + r + 1) * HEAD_DIM] = (
                o[:, r * ds:(r + 1) * ds, :])

    u = _gelu(_project(xb, w_in_ref, b_in_ref, V_END, U_END))
    vs = _gelu(_project(xb, w_in_ref, b_in_ref, U_END, VS_END))
    vs_ln = _layer_norm(vs, sgg_ref[...], sgb_ref[...])
    cvs_ref[...] = vs_ln.reshape(nseq, ds, SG_WIDTH)
    vq = vs_ln.astype(BF16).astype(F32).reshape(nseq, ds, SG_WIDTH)
    u3 = u.reshape(nseq, ds, SG_WIDTH)
    i_row = lax.broadcasted_iota(jnp.int32, (ds, 1), 0)
    for g in range(N_SG_GROUPS):
        c0, c1 = g * SG_GROUP_DIM, (g + 1) * SG_GROUP_DIM
        acc = jnp.broadcast_to(sgbias8_ref[g][None], (nseq, ds, SG_GROUP_DIM))
        for j in range(ds):
            w_col = jnp.where(i_row >= j, sgw8_ref[g, j], 0.0)
            w_col = w_col.astype(BF16).astype(F32)
            acc = acc + w_col[None] * vq[:, j:j + 1, c0:c1]
        s_scr[:, :, c0:c1] = u3[:, :, c0:c1] * acc

    ga = _project(xb, w_in_ref, b_in_ref, VS_END, GA_END)
    gb = _project(xb, w_in_ref, b_in_ref, GA_END, IN_W)
    x1 = _merge_and_norm(x, a_scr[...].reshape(rows, Q_W).astype(BF16),
                         s_scr[...].reshape(rows, SG_WIDTH).astype(BF16),
                         ga, gb, wpa_ref, wpb_ref, wout_ref, ln1g_ref, ln1b_ref)
    x1_ref[...] = x1.reshape(nseq, ds, D_MODEL)


def _mixer_sample(x, cache_k, cache_v, table, sinks, w_in_bf, b_in, sgg, sgb, sgw, sgbias,
                  wpa, wpb, wout, ln1g, ln1b):
    nb, ds, _ = x.shape
    win = cache_k.shape[1]
    nseq = SAMPLE_SEQS_PER_STEP
    qrows = Q_PER_KV * ds
    t = np.arange(ds)[:, None]
    bucket_c = jnp.asarray(_t5_bucket_np(t + win - np.arange(win)[None, :]))
    bucket_n = jnp.asarray(_t5_bucket_np(t - (np.arange(win)[None, :] - (win - ds))))
    sink_rows = jnp.repeat(sinks.reshape(N_KV_HEADS, Q_PER_KV), ds, axis=1).reshape(
        N_KV_HEADS, qrows, 1)
    sgw8 = jnp.transpose(sgw[:, :ds, :ds], (0, 2, 1))[..., None]
    sgbias8 = sgbias[:, :ds, :]
    smem = pl.BlockSpec(memory_space=pltpu.SMEM)
    in_specs = [
        smem,
        _const_spec((ds, win)),
        _const_spec((ds, win)),
        _const_spec((N_KV_HEADS, qrows, 1)),
        pl.BlockSpec((nseq, ds, D_MODEL), lambda i: (i, 0, 0)),
        pl.BlockSpec((nseq, win, KV_W), lambda i: (i, 0, 0)),
        pl.BlockSpec((nseq, win, KV_W), lambda i: (i, 0, 0)),
        _const_spec((D_MODEL, IN_W)),
        _const_spec((1, IN_W)),
        _const_spec((1, SG_WIDTH)),
        _const_spec((1, SG_WIDTH)),
        _const_spec((N_SG_GROUPS, ds, ds, 1)),
        _const_spec((N_SG_GROUPS, ds, 1)),
        _const_spec((Q_W, D_MODEL)),
        _const_spec((SG_WIDTH, D_MODEL)),
        _const_spec((D_MODEL, D_MODEL)),
        _const_spec((1, D_MODEL)),
        _const_spec((1, D_MODEL)),
    ]
    out_specs = [
        pl.BlockSpec((nseq, ds, D_MODEL), lambda i: (i, 0, 0)),
        pl.BlockSpec((nseq, win, KV_W), lambda i: (i, 0, 0)),
        pl.BlockSpec((nseq, win, KV_W), lambda i: (i, 0, 0)),
        pl.BlockSpec((nseq, ds, SG_WIDTH), lambda i: (i, 0, 0)),
    ]
    out_shape = [
        jax.ShapeDtypeStruct((nb, ds, D_MODEL), F32),
        jax.ShapeDtypeStruct((nb, win, KV_W), F32),
        jax.ShapeDtypeStruct((nb, win, KV_W), F32),
        jax.ShapeDtypeStruct((nb, ds, SG_WIDTH), F32),
    ]
    scratch = [
        pltpu.VMEM((N_KV_HEADS, qrows, win), F32),
        pltpu.VMEM((N_KV_HEADS, qrows, win), F32),
        pltpu.VMEM((nseq, ds, Q_W), F32),
        pltpu.VMEM((nseq, ds, SG_WIDTH), F32),
    ]
    return pl.pallas_call(
        _mixer_sample_kernel,
        grid=(nb // nseq,),
        in_specs=in_specs,
        out_specs=out_specs,
        out_shape=out_shape,
        scratch_shapes=scratch,
        compiler_params=pltpu.CompilerParams(
            dimension_semantics=("arbitrary",),
            vmem_limit_bytes=VMEM_LIMIT_BYTES),
        name="mixer_sample",
    )(table, bucket_c, bucket_n, sink_rows, x, cache_k, cache_v, w_in_bf, b_in, sgg, sgb,
      sgw8, sgbias8, wpa, wpb, wout, ln1g, ln1b)


def _route(xb, wrt_ref, rbias_ref):
    n = xb.shape[0]
    logits = _dot_nt(wrt_ref[...].astype(BF16), xb)
    scores = jax.nn.sigmoid(logits)
    sel = scores + rbias_ref[...]
    shape3 = (N_EXPERT_GROUPS, EXPERTS_PER_GROUP, n)
    scores3 = scores.reshape(shape3)
    sel3 = sel.reshape(shape3)
    i_in = lax.broadcasted_iota(jnp.int32, shape3, 1)
    g_id = lax.broadcasted_iota(jnp.int32, shape3, 0)
    e_id = g_id * EXPERTS_PER_GROUP + i_in
    neg = -jnp.inf

    m1 = jnp.max(sel3, axis=1, keepdims=True)
    first = jnp.min(jnp.where(sel3 == m1, i_in, EXPERTS_PER_GROUP), axis=1, keepdims=True)
    m2 = jnp.max(jnp.where(i_in == first, neg, sel3), axis=1, keepdims=True)
    gscore = m1 + m2

    gsel = jnp.zeros(gscore.shape, jnp.bool_)
    gid1 = lax.broadcasted_iota(jnp.int32, gscore.shape, 0)
    for _ in range(TOPK_GROUPS):
        m = jnp.max(gscore, axis=0, keepdims=True)
        pick = jnp.min(jnp.where(gscore == m, gid1, N_EXPERT_GROUPS), axis=0, keepdims=True)
        chosen = gid1 == pick
        gsel = gsel | chosen
        gscore = jnp.where(chosen, neg, gscore)
    val = jnp.where(gsel, sel3, NEG_INF)

    esel = jnp.zeros(shape3, jnp.bool_)
    for _ in range(TOP_K):
        m = jnp.max(jnp.max(val, axis=0, keepdims=True), axis=1, keepdims=True)
        cand = jnp.where(val == m, e_id, N_EXPERTS)
        pick = jnp.min(jnp.min(cand, axis=0, keepdims=True), axis=1, keepdims=True)
        chosen = e_id == pick
        esel = esel | chosen
        val = jnp.where(chosen, neg, val)
    w_sel = jnp.where(esel, scores3, 0.0)
    total = jnp.sum(jnp.sum(w_sel, axis=0, keepdims=True), axis=1, keepdims=True)
    gates = w_sel / total * ROUTED_SCALE
    return gates.reshape(N_EXPERTS, n)


def _swiglu(xb, wg, wu):
    return jax.nn.silu(_dot(xb, wg)) * _dot(xb, wu)


def _bf(mask):
    return jnp.where(mask, 1.0, 0.0).astype(BF16)


def _round_up_run(count):
    return jnp.ceil(count * (1.0 / RUN_ALIGN)) * RUN_ALIGN


def _run_layout_rows(sel_bf):
    n = sel_bf.shape[1]
    ones = jnp.ones((16, n), BF16)
    pc = _round_up_run(_dot_nt(ones, sel_bf))
    lower = lax.broadcasted_iota(jnp.int32, (N_EXPERTS, N_EXPERTS), 0)
    upper = lax.broadcasted_iota(jnp.int32, (N_EXPERTS, N_EXPERTS), 1)
    off = _dot(pc.astype(BF16), _bf(lower < upper))
    return pc, off


def _run_layout_cols(sel_f32):
    cnt = jnp.sum(sel_f32, axis=1, keepdims=True)
    pc = jnp.broadcast_to(_round_up_run(cnt), (N_EXPERTS, 128))
    row = lax.broadcasted_iota(jnp.int32, (N_EXPERTS, N_EXPERTS), 0)
    col = lax.broadcasted_iota(jnp.int32, (N_EXPERTS, N_EXPERTS), 1)
    off = _dot(_bf(col < row), pc.astype(BF16))
    return pc, off


def _rank_in_run(sel_bf):
    n = sel_bf.shape[1]
    m_id = lax.broadcasted_iota(jnp.int32, (n, n), 0)
    n_id = lax.broadcasted_iota(jnp.int32, (n, n), 1)
    return _dot(sel_bf, _bf(m_id < n_id))


def _route_kernel(x_ref, wrt_ref, rbias_ref, gates_ref, pc_ref):
    xb = x_ref[...].astype(BF16)
    gates = _route(xb, wrt_ref, rbias_ref)
    gates_ref[...] = gates
    sel_bf = _bf(gates > 0.0)
    pad = jnp.zeros((128 - N_EXPERTS, sel_bf.shape[1]), BF16)
    ones = jnp.ones((16, sel_bf.shape[1]), BF16)
    cnt = _dot_nt(ones, jnp.concatenate([sel_bf, pad], axis=0))
    pc_ref[...] = _round_up_run(cnt)[:1].astype(jnp.int32)


def _moe_route(x1, wrt, rbias):
    n_tok = x1.shape[0]
    n_tiles = n_tok // MOE_TILE
    return pl.pallas_call(
        _route_kernel,
        grid=(n_tiles,),
        in_specs=[
            pl.BlockSpec((MOE_TILE, D_MODEL), lambda t: (t, 0)),
            _const_spec((N_EXPERTS, D_MODEL)),
            _const_spec((N_EXPERTS, 1)),
        ],
        out_specs=[
            pl.BlockSpec((N_EXPERTS, MOE_TILE), lambda t: (0, t)),
            pl.BlockSpec((None, 1, 128), lambda t: (t, 0, 0)),
        ],
        out_shape=[
            jax.ShapeDtypeStruct((N_EXPERTS, n_tok), F32),
            jax.ShapeDtypeStruct((n_tiles, 1, 128), jnp.int32),
        ],
        compiler_params=pltpu.CompilerParams(dimension_semantics=("arbitrary",)),
        name="moe_route",
    )(x1, wrt, rbias)


def _plan_kernel(pc_ref, gbase_ref, tchunks_ref, blkexp_ref, gapstart_ref, gapn_ref, meta_ref):
    n_tiles = pc_ref.shape[0]
    n_blocks_max = blkexp_ref.shape[0]

    def zero_tile(t, c):
        tchunks_ref[t] = 0
        return c

    lax.fori_loop(0, n_tiles, zero_tile, 0)

    def expert_body(e, carry):
        g0, b0 = carry

        def tile_body(t, run):
            n = pc_ref[t, e]
            gbase_ref[t, e] = g0 + run
            tchunks_ref[t] = tchunks_ref[t] + n // RUN_ALIGN
            return run + n

        rows = lax.fori_loop(0, n_tiles, tile_body, 0)
        nb = (rows + EXPERT_BLOCK - 1) // EXPERT_BLOCK

        def block_body(j, c):
            blkexp_ref[b0 + j] = e
            return c

        lax.fori_loop(0, nb, block_body, 0)
        gapstart_ref[e] = g0 + rows
        gapn_ref[e] = (nb * EXPERT_BLOCK - rows) // RUN_ALIGN
        return g0 + nb * EXPERT_BLOCK, b0 + nb

    _, n_blocks = lax.fori_loop(0, N_EXPERTS, expert_body, (0, 0))
    last = blkexp_ref[jnp.maximum(n_blocks - 1, 0)]

    def tail_body(j, c):
        blkexp_ref[j] = last
        return c

    lax.fori_loop(n_blocks, n_blocks_max, tail_body, 0)
    meta_ref[0] = n_blocks


def _max_blocks(n_tiles):
    per_tile = MOE_TILE * TOP_K + N_EXPERTS * (RUN_ALIGN - 1)
    rows = n_tiles * per_tile + N_EXPERTS * (EXPERT_BLOCK - RUN_ALIGN)
    return -(-rows // EXPERT_BLOCK)


def _moe_plan(pc):
    n_tiles = pc.shape[0]
    smem = pl.BlockSpec(memory_space=pltpu.SMEM)
    i32 = jnp.int32
    return pl.pallas_call(
        _plan_kernel,
        in_specs=[smem],
        out_specs=[smem] * 6,
        out_shape=[
            jax.ShapeDtypeStruct((n_tiles, N_EXPERTS), i32),
            jax.ShapeDtypeStruct((n_tiles,), i32),
            jax.ShapeDtypeStruct((_max_blocks(n_tiles),), i32),
            jax.ShapeDtypeStruct((N_EXPERTS,), i32),
            jax.ShapeDtypeStruct((N_EXPERTS,), i32),
            jax.ShapeDtypeStruct((1,), i32),
        ],
        name="moe_plan",
    )(pc)


def _aligned(row):
    return row if isinstance(row, int) else pl.multiple_of(row, RUN_ALIGN)


def _chunk_copy(src_ref, src_row, dst_ref, dst_row, sem):
    return pltpu.make_async_copy(
        src_ref.at[pl.ds(_aligned(src_row), RUN_ALIGN), :],
        dst_ref.at[pl.ds(_aligned(dst_row), RUN_ALIGN), :],
        sem)


def _wait_chunks(src_ref, dst_ref, sem, count):
    def body(i, c):
        _chunk_copy(src_ref, 0, dst_ref, 0, sem).wait()
        return c

    lax.fori_loop(0, count, body, 0)


def _dispatch_kernel(pc_ref, gbase_ref, tchunks_ref, gapstart_ref, gapn_ref,
                     x_ref, gates_ref, xg_ref, xs_scr, zero_scr, sems):
    t = pl.program_id(0)
    n_tiles = pl.num_programs(0)
    slot = t % 2

    @pl.when(t >= 2)
    def _drain_slot():
        _wait_chunks(xs_scr.at[slot], xg_ref, sems.at[slot], tchunks_ref[t - 2])

    gates = gates_ref[...]
    sel = gates > 0.0
    sel_bf = _bf(sel)
    rankp = jnp.where(sel, _rank_in_run(sel_bf), -1.0).astype(BF16)
    pc_row, off_row = _run_layout_rows(sel_bf)
    pc_col, off_col = _run_layout_cols(jnp.where(sel, 1.0, 0.0))
    s_id = lax.broadcasted_iota(jnp.int32, (TILE_SLOTS, N_EXPERTS), 0).astype(F32)
    owner = _bf((s_id >= off_row[:1]) & (s_id < off_row[:1] + pc_row[:1]))
    rank_of_slot = _dot(owner, rankp)
    off_of_slot = _dot(owner, (off_col * (1.0 / RUN_ALIGN)).astype(BF16)) * RUN_ALIGN
    s_lane = lax.broadcasted_iota(jnp.int32, (TILE_SLOTS, 128), 0).astype(F32)
    onehot = jnp.concatenate(
        [_bf(rank_of_slot[:, c * 128:(c + 1) * 128] + off_of_slot == s_lane)
         for c in range(MOE_TILE // 128)], axis=1)
    xb = x_ref[...].astype(BF16)
    for c in range(TILE_SLOTS // 512):
        xs_scr[slot, c * 512:(c + 1) * 512, :] = _dot(
            onehot[c * 512:(c + 1) * 512, :], xb).astype(BF16)

    def expert_body(e, src):
        n = pc_ref[t, e]
        dst = gbase_ref[t, e]

        def chunk_body(k, c):
            _chunk_copy(xs_scr.at[slot], src + k * RUN_ALIGN, xg_ref, dst + k * RUN_ALIGN,
                        sems.at[slot]).start()
            return c

        lax.fori_loop(0, n // RUN_ALIGN, chunk_body, 0)
        return src + n

    lax.fori_loop(0, N_EXPERTS, expert_body, 0)

    @pl.when(t == n_tiles - 1)
    def _finish():
        zero_scr[...] = jnp.zeros_like(zero_scr)

        def gap_body(e, total):
            n = gapn_ref[e]

            def chunk_body(k, c):
                _chunk_copy(zero_scr, 0, xg_ref, gapstart_ref[e] + k * RUN_ALIGN,
                            sems.at[2]).start()
                return c

            lax.fori_loop(0, n, chunk_body, 0)
            return total + n

        n_gap = lax.fori_loop(0, N_EXPERTS, gap_body, 0)
        _wait_chunks(zero_scr, xg_ref, sems.at[2], n_gap)
        _wait_chunks(xs_scr.at[slot], xg_ref, sems.at[slot], tchunks_ref[t])

        @pl.when(t >= 1)
        def _drain_other():
            _wait_chunks(xs_scr.at[1 - slot], xg_ref, sems.at[1 - slot], tchunks_ref[t - 1])


def _moe_dispatch(x1, gates, pc, gbase, tchunks, gapstart, gapn, n_rows):
    n_tiles = x1.shape[0] // MOE_TILE
    smem = pl.BlockSpec(memory_space=pltpu.SMEM)
    return pl.pallas_call(
        _dispatch_kernel,
        grid=(n_tiles,),
        in_specs=[
            smem, smem, smem, smem, smem,
            pl.BlockSpec((MOE_TILE, D_MODEL), lambda t: (t, 0)),
            pl.BlockSpec((N_EXPERTS, MOE_TILE), lambda t: (0, t)),
        ],
        out_specs=pl.BlockSpec(memory_space=pl.ANY),
        out_shape=jax.ShapeDtypeStruct((n_rows, D_MODEL), BF16),
        scratch_shapes=[
            pltpu.VMEM((2, TILE_SLOTS, D_MODEL), BF16),
            pltpu.VMEM((RUN_ALIGN, D_MODEL), BF16),
            pltpu.SemaphoreType.DMA((3,)),
        ],
        compiler_params=pltpu.CompilerParams(
            dimension_semantics=("arbitrary",), vmem_limit_bytes=VMEM_LIMIT_BYTES,
            has_side_effects=True),
        name="moe_dispatch",
    )(pc, gbase, tchunks, gapstart, gapn, x1, gates)


def _experts_kernel(blkexp_ref, meta_ref, xg_ref, wg_ref, wu_ref, wd_ref, yg_ref,
                    wg_scr, wu_scr, wd_scr):
    b = pl.program_id(0)

    @pl.when(b < meta_ref[0])
    def _block():
        changed = (b == 0) | (blkexp_ref[b] != blkexp_ref[jnp.maximum(b - 1, 0)])

        @pl.when(changed)
        def _load_weights():
            wg_scr[...] = wg_ref[...].astype(BF16)
            wu_scr[...] = wu_ref[...].astype(BF16)
            wd_scr[...] = wd_ref[...].astype(BF16)

        h = _swiglu(xg_ref[...], wg_scr[...], wu_scr[...])
        yg_ref[...] = _dot(h.astype(BF16), wd_scr[...]).astype(BF16)


def _moe_experts(blkexp, meta, xg, wg, wu, wd):
    n_blocks = blkexp.shape[0]

    def row_map(b, blkexp_ref, meta_ref):
        return (jnp.minimum(b, meta_ref[0] - 1), 0)

    def w_map(b, blkexp_ref, meta_ref):
        return (blkexp_ref[b], 0, 0)

    grid_spec = pltpu.PrefetchScalarGridSpec(
        num_scalar_prefetch=2,
        grid=(n_blocks,),
        in_specs=[
            pl.BlockSpec((EXPERT_BLOCK, D_MODEL), row_map),
            pl.BlockSpec((None, D_MODEL, EXPERT_DIM), w_map),
            pl.BlockSpec((None, D_MODEL, EXPERT_DIM), w_map),
            pl.BlockSpec((None, EXPERT_DIM, D_MODEL), w_map),
        ],
        out_specs=pl.BlockSpec((EXPERT_BLOCK, D_MODEL), row_map),
        scratch_shapes=[
            pltpu.VMEM((D_MODEL, EXPERT_DIM), BF16),
            pltpu.VMEM((D_MODEL, EXPERT_DIM), BF16),
            pltpu.VMEM((EXPERT_DIM, D_MODEL), BF16),
        ],
    )
    return pl.pallas_call(
        _experts_kernel,
        grid_spec=grid_spec,
        out_shape=jax.ShapeDtypeStruct(xg.shape, BF16),
        compiler_params=pltpu.CompilerParams(
            dimension_semantics=("arbitrary",), vmem_limit_bytes=VMEM_LIMIT_BYTES),
        name="moe_experts",
    )(blkexp, meta, xg, wg, wu, wd)


def _combine_kernel(pc_ref, gbase_ref, tchunks_ref,
                    x_ref, gates_ref, yg_ref, wgs_ref, wus_ref, wds_ref, ln2g_ref, ln2b_ref,
                    out_ref, ys_scr, sems):
    t = pl.program_id(0)
    n_tiles = pl.num_programs(0)
    slot = t % 2

    def start_gather(tile, dst_slot):
        def expert_body(e, dst):
            n = pc_ref[tile, e]
            src = gbase_ref[tile, e]

            def chunk_body(k, c):
                _chunk_copy(yg_ref, src + k * RUN_ALIGN, ys_scr.at[dst_slot],
                            dst + k * RUN_ALIGN, sems.at[dst_slot]).start()
                return c

            lax.fori_loop(0, n // RUN_ALIGN, chunk_body, 0)
            return dst + n

        lax.fori_loop(0, N_EXPERTS, expert_body, 0)

    @pl.when(t == 0)
    def _prime():
        ys_scr[...] = jnp.zeros_like(ys_scr)
        start_gather(0, 0)

    @pl.when(t + 1 < n_tiles)
    def _prefetch():
        start_gather(t + 1, 1 - slot)

    gates = gates_ref[...]
    sel = gates > 0.0
    sel_bf = _bf(sel)
    rankp = jnp.where(sel, _rank_in_run(sel_bf), -1.0)
    pc_row, off_row = _run_layout_rows(sel_bf)
    pc_col, off_col = _run_layout_cols(jnp.where(sel, 1.0, 0.0))
    pad = jnp.zeros((128 - N_EXPERTS, MOE_TILE), F32)
    gates_t = jnp.concatenate([gates, pad], axis=0).T.astype(BF16)
    rankp_t = jnp.concatenate([rankp, pad], axis=0).T.astype(BF16)
    s_id = lax.broadcasted_iota(jnp.int32, (N_EXPERTS, TILE_SLOTS), 1).astype(F32)
    owner = _bf((s_id >= off_col[:, :1]) & (s_id < off_col[:, :1] + pc_col[:, :1]))
    owner = jnp.concatenate([owner, jnp.zeros((128 - N_EXPERTS, TILE_SLOTS), BF16)], axis=0)
    rank_of_slot = _dot(rankp_t, owner)
    gate_of_slot = _dot(gates_t, owner)
    o16 = jnp.concatenate([off_row * (1.0 / RUN_ALIGN), jnp.zeros((16, 128 - N_EXPERTS), F32)],
                          axis=1).astype(BF16)
    off_of_slot = _dot(o16, owner)[:1] * RUN_ALIGN
    s_row = lax.broadcasted_iota(jnp.int32, (1, TILE_SLOTS), 1).astype(F32)
    weights = jnp.where(rank_of_slot + off_of_slot == s_row, gate_of_slot, 0.0).astype(BF16)

    x = x_ref[...]
    xb = x.astype(BF16)
    hs = _swiglu(xb, wgs_ref[...].astype(BF16), wus_ref[...].astype(BF16))
    shared = _dot(hs.astype(BF16), wds_ref[...].astype(BF16))

    _wait_chunks(yg_ref, ys_scr.at[slot], sems.at[slot], tchunks_ref[t])
    routed = _dot(weights, ys_scr[slot])
    out_ref[...] = _layer_norm(ALPHA * x + (routed + shared), ln2g_ref[...], ln2b_ref[...])


def _moe_combine(x1, gates, yg, pc, gbase, tchunks, wgs, wus, wds, ln2g, ln2b):
    n_tok = x1.shape[0]
    smem = pl.BlockSpec(memory_space=pltpu.SMEM)
    return pl.pallas_call(
        _combine_kernel,
        grid=(n_tok // MOE_TILE,),
        in_specs=[
            smem, smem, smem,
            pl.BlockSpec((MOE_TILE, D_MODEL), lambda t: (t, 0)),
            pl.BlockSpec((N_EXPERTS, MOE_TILE), lambda t: (0, t)),
            pl.BlockSpec(memory_space=pl.ANY),
            _const_spec((D_MODEL, SHARED_DIM)),
            _const_spec((D_MODEL, SHARED_DIM)),
            _const_spec((SHARED_DIM, D_MODEL)),
            _const_spec((1, D_MODEL)),
            _const_spec((1, D_MODEL)),
        ],
        out_specs=pl.BlockSpec((MOE_TILE, D_MODEL), lambda t: (t, 0)),
        out_shape=jax.ShapeDtypeStruct((n_tok, D_MODEL), F32),
        scratch_shapes=[
            pltpu.VMEM((2, TILE_SLOTS, D_MODEL), BF16),
            pltpu.SemaphoreType.DMA((2,)),
        ],
        compiler_params=pltpu.CompilerParams(
            dimension_semantics=("arbitrary",), vmem_limit_bytes=VMEM_LIMIT_BYTES),
        name="moe_combine",
    )(pc, gbase, tchunks, x1, gates, yg, wgs, wus, wds, ln2g, ln2b)


def _moe(x1, wrt, rbias, wg, wu, wd, wgs, wus, wds, ln2g, ln2b):
    n_tiles = x1.shape[0] // MOE_TILE
    gates, pc3 = _moe_route(x1, wrt, rbias)
    pc = pc3.reshape(n_tiles, 128)
    gbase, tchunks, blkexp, gapstart, gapn, meta = _moe_plan(pc)
    n_rows = blkexp.shape[0] * EXPERT_BLOCK
    xg = _moe_dispatch(x1, gates, pc, gbase, tchunks, gapstart, gapn, n_rows)
    yg = _moe_experts(blkexp, meta, xg, wg, wu, wd)
    return _moe_combine(x1, gates, yg, pc, gbase, tchunks, wgs, wus, wds, ln2g, ln2b)


def kernel(x_prompt, x_sample, cache_win_k, cache_win_v, rel_bias_table, w_in, b_in, attn_sinks,
           sg_ln_g, sg_ln_b, sg_w, sg_b, w_proj_a, w_proj_b, w_out, ln1_g, ln1_b, w_router,
           router_bias, w_gate_e, w_up_e, w_down_e, w_gate_s, w_up_s, w_down_s, ln2_g, ln2_b):
    assert DEPTH == 1 and w_in.shape[0] == 1
    batch, seq, _ = x_prompt.shape
    nb, ds, _ = x_sample.shape
    win = cache_win_k.shape[2]

    w_in_bf = w_in[0].astype(BF16)
    wpa = w_proj_a[0].astype(BF16)
    wpb = w_proj_b[0].astype(BF16)
    wout = w_out[0].astype(BF16)
    b_in2 = b_in[0].reshape(1, IN_W)
    sgg = sg_ln_g[0].reshape(1, SG_WIDTH)
    sgb = sg_ln_b[0].reshape(1, SG_WIDTH)
    sgw = sg_w[0]
    sgbias = sg_b[0].reshape(N_SG_GROUPS, CHUNK, 1)
    ln1g = ln1_g[0].reshape(1, D_MODEL)
    ln1b = ln1_b[0].reshape(1, D_MODEL)
    sinks = attn_sinks[0]

    x1_p, wk_p, wv_p = _mixer_prompt(
        x_prompt, rel_bias_table, sinks.reshape(N_Q_HEADS, 1, 1), w_in_bf, b_in2, sgg, sgb, sgw,
        sgbias, wpa, wpb, wout, ln1g, ln1b)
    x1_s, wk_s, wv_s, cvs = _mixer_sample(
        x_sample, cache_win_k[0].reshape(nb, win, KV_W), cache_win_v[0].reshape(nb, win, KV_W),
        rel_bias_table, sinks, w_in_bf, b_in2, sgg, sgb, sgw, sgbias, wpa, wpb, wout, ln1g, ln1b)

    x1 = jnp.concatenate([x1_p.reshape(batch * seq, D_MODEL), x1_s.reshape(nb * ds, D_MODEL)])
    y = _moe(x1, wrt=w_router[0].T, rbias=router_bias[0].reshape(N_EXPERTS, 1),
             wg=w_gate_e[0], wu=w_up_e[0], wd=w_down_e[0],
             wgs=w_gate_s[0], wus=w_up_s[0], wds=w_down_s[0],
             ln2g=ln2_g[0].reshape(1, D_MODEL), ln2b=ln2_b[0].reshape(1, D_MODEL))
    y_p = y[:batch * seq].reshape(batch, seq, D_MODEL)
    y_s = y[batch * seq:].reshape(nb, ds, D_MODEL)

    kv_shape = (1, -1, WINDOW, N_KV_HEADS, HEAD_DIM)
    return (y_p, y_s,
            wk_p.reshape(kv_shape), wv_p.reshape(kv_shape),
            wk_s.reshape(1, nb, win, N_KV_HEADS, HEAD_DIM),
            wv_s.reshape(1, nb, win, N_KV_HEADS, HEAD_DIM),
            cvs.reshape(1, nb, ds, N_SG_GROUPS, SG_GROUP_DIM))
```

```python
import functools
import math

import jax
import jax.numpy as jnp
import numpy as np
from jax import lax
from jax.experimental import pallas as pl
from jax.experimental.pallas import tpu as pltpu

F32 = jnp.float32
BF16 = jnp.bfloat16

D_MODEL = 1024
DEPTH = 1
HEAD_DIM = 64
N_Q_HEADS = 16
N_KV_HEADS = 2
Q_PER_KV = N_Q_HEADS // N_KV_HEADS
WINDOW = 128
ATTN_SCALE = HEAD_DIM ** -0.5
NEG_INF = -1e30
N_BUCKETS = 32
BUCKET_MAX_EXACT = 16
BUCKET_MAX_DIST = 128
CHUNK = 128
N_SG_GROUPS = 4
SG_GROUP_DIM = 128
SG_WIDTH = N_SG_GROUPS * SG_GROUP_DIM
Q_W = N_Q_HEADS * HEAD_DIM
KV_W = N_KV_HEADS * HEAD_DIM
Q_END = Q_W
K_END = Q_END + KV_W
V_END = K_END + KV_W
U_END = V_END + SG_WIDTH
VS_END = U_END + SG_WIDTH
GA_END = VS_END + D_MODEL
IN_W = GA_END + D_MODEL
N_EXPERTS = 64
TOP_K = 8
N_EXPERT_GROUPS = 8
EXPERTS_PER_GROUP = N_EXPERTS // N_EXPERT_GROUPS
TOPK_GROUPS = 4
EXPERT_DIM = 256
SHARED_DIM = 256
ROUTED_SCALE = 2.5
ALPHA = (2 * DEPTH) ** 0.25
LN_EPS = 1e-5

VMEM_LIMIT_BYTES = 56 * 1024 * 1024

PROMPT_STEP = 512
SAMPLE_SEQS_PER_STEP = 32
MOE_TILE = 256
RUN_ALIGN = 16
EXPERT_BLOCK = 512
TILE_SLOTS = -(-(MOE_TILE * TOP_K + N_EXPERTS * (RUN_ALIGN - 1)) // 512) * 512


def _t5_bucket_np(dist):
    d = np.maximum(dist, 0)
    ratio = np.maximum(d, 1).astype(np.float32) / np.float32(BUCKET_MAX_EXACT)
    large = BUCKET_MAX_EXACT + (
        np.log(ratio) / np.float32(math.log(BUCKET_MAX_DIST / BUCKET_MAX_EXACT))
        * np.float32(N_BUCKETS - BUCKET_MAX_EXACT)).astype(np.int32)
    large = np.minimum(large, N_BUCKETS - 1)
    return np.where(d < BUCKET_MAX_EXACT, d, large).astype(np.int32)


def _layer_norm(x, g, b):
    mu = jnp.mean(x, -1, keepdims=True)
    xc = x - mu
    var = jnp.mean(xc * xc, -1, keepdims=True)
    return xc * lax.rsqrt(var + LN_EPS) * g + b


def _gelu(x):
    return jax.nn.gelu(x)


def _dot(a, b):
    return jnp.dot(a, b, preferred_element_type=F32)


def _dot_nt(a, b):
    return lax.dot_general(a, b, (((1,), (1,)), ((), ())), preferred_element_type=F32)


def _project(xb, w_in_ref, b_in_ref, lo, hi):
    return _dot(xb, w_in_ref[:, lo:hi]) + b_in_ref[:, lo:hi]


def _expand_bias(bucket, table_ref, head):
    acc = jnp.zeros(bucket.shape, F32)
    for b in range(N_BUCKETS):
        acc = jnp.where(bucket == b, table_ref[b, head], acc)
    return acc


def _merge_and_norm(x, a_bf, s_bf, ga, gb, wpa_ref, wpb_ref, wout_ref, g_ref, b_ref):
    pa = _dot(a_bf, wpa_ref[...])
    pb = _dot(s_bf, wpb_ref[...])
    hpre = jax.nn.sigmoid(ga) * pa + jax.nn.sigmoid(gb) * pb
    h = _dot(hpre.astype(BF16), wout_ref[...])
    return _layer_norm(ALPHA * x + h, g_ref[...], b_ref[...])


def _mixer_prompt_kernel(table_ref, bucket_ref, sink_ref, x_ref, w_in_ref, b_in_ref,
                         sgg_ref, sgb_ref, sgw_ref, sgbias_ref, wpa_ref, wpb_ref, wout_ref,
                         ln1g_ref, ln1b_ref,
                         x1_ref, wk_ref, wv_ref,
                         bias_scr, tril_scr, kprev_scr, vprev_scr, a_scr, s_scr):
    b_idx = pl.program_id(0)
    n_idx = pl.program_id(1)
    n_blocks = PROMPT_STEP // WINDOW

    @pl.when((b_idx == 0) & (n_idx == 0))
    def _init_tables():
        bucket = bucket_ref[...]
        for h in range(N_Q_HEADS):
            g, r = divmod(h, Q_PER_KV)
            pair, parity = divmod(r, 2)
            bias_scr[g, pair, :, parity * 2 * WINDOW:(parity + 1) * 2 * WINDOW] = (
                _expand_bias(bucket, table_ref, h))
        row = lax.broadcasted_iota(jnp.int32, (CHUNK, CHUNK), 0)
        col = lax.broadcasted_iota(jnp.int32, (CHUNK, CHUNK), 1)
        for g in range(N_SG_GROUPS):
            tril_scr[g] = jnp.where(row >= col, sgw_ref[g], 0.0).astype(BF16)

    @pl.when(n_idx == 0)
    def _reset_carry():
        kprev_scr[...] = jnp.zeros_like(kprev_scr)
        vprev_scr[...] = jnp.zeros_like(vprev_scr)

    x = x_ref[...]
    xb = x.astype(BF16)
    q_bf = (_project(xb, w_in_ref, b_in_ref, 0, Q_END) * ATTN_SCALE).astype(BF16)
    k = _project(xb, w_in_ref, b_in_ref, Q_END, K_END)
    v = _project(xb, w_in_ref, b_in_ref, K_END, V_END)

    @pl.when(n_idx == pl.num_programs(1) - 1)
    def _emit_window():
        wk_ref[...] = k[PROMPT_STEP - WINDOW:, :]
        wv_ref[...] = v[PROMPT_STEP - WINDOW:, :]

    low = lax.broadcasted_iota(jnp.int32, (PROMPT_STEP, KV_W), 1) < HEAD_DIM

    def lane_halves(t):
        t_sw = pltpu.roll(t, HEAD_DIM, axis=1)
        zero = jnp.zeros_like(t)
        return [[jnp.where(low, t, zero).astype(BF16), jnp.where(low, zero, t_sw).astype(BF16)],
                [jnp.where(low, t_sw, zero).astype(BF16), jnp.where(low, zero, t).astype(BF16)]]

    k_half = lane_halves(k)
    v_half = lane_halves(v)

    row = lax.broadcasted_iota(jnp.int32, (WINDOW, 4 * WINDOW), 0)
    col = lax.broadcasted_iota(jnp.int32, (WINDOW, 4 * WINDOW), 1) % (2 * WINDOW)
    dist = row + WINDOW - col
    band_valid = (dist >= 0) & (dist <= WINDOW)
    first_valid = band_valid & ((col >= WINDOW) | (n_idx > 0))
    lane_low = lax.broadcasted_iota(jnp.int32, (Q_PER_KV // 2, WINDOW, 2 * HEAD_DIM), 2) < HEAD_DIM
    ones_rows = lax.broadcasted_iota(jnp.int32, (4 * WINDOW, 2 * HEAD_DIM), 0) < 2 * WINDOW
    ones_cols = lax.broadcasted_iota(jnp.int32, (4 * WINDOW, 2 * HEAD_DIM), 1) < HEAD_DIM
    sum_block = _bf(ones_rows == ones_cols)

    for j in range(n_blocks):
        r0, r1 = j * WINDOW, (j + 1) * WINDOW
        valid = first_valid if j == 0 else band_valid
        for g in range(N_KV_HEADS):
            def band(cur, prev_scr):
                parts = []
                for s in range(2):
                    prev = prev_scr[2 * g + s] if j == 0 else cur[g][s][r0 - WINDOW:r0]
                    parts += [prev, cur[g][s][r0:r1]]
                return jnp.concatenate(parts, axis=0)

            kd = band(k_half, kprev_scr)
            vd = jnp.concatenate([band(v_half, vprev_scr), sum_block], axis=1)
            q0 = g * Q_PER_KV * HEAD_DIM
            qp = jnp.concatenate(
                [q_bf[r0:r1, q0 + pr * 2 * HEAD_DIM:q0 + (pr + 1) * 2 * HEAD_DIM]
                 for pr in range(Q_PER_KV // 2)], axis=0)
            logits = _dot_nt(qp, kd).reshape(Q_PER_KV // 2, WINDOW, 4 * WINDOW)
            logits = jnp.where(valid[None], logits + bias_scr[g], NEG_INF)
            probs, sink_terms = [], []
            for s in range(2):
                l_s = logits[:, :, s * 2 * WINDOW:(s + 1) * 2 * WINDOW]
                sink = sink_ref[s, g]
                m = jnp.maximum(jnp.max(l_s, -1, keepdims=True), sink)
                probs.append(jnp.exp(l_s - m))
                sink_terms.append(jnp.broadcast_to(jnp.exp(sink - m), lane_low.shape))
            p = jnp.concatenate(probs, axis=-1).reshape(4 * WINDOW, 4 * WINDOW).astype(BF16)
            out = _dot(p, vd).reshape(Q_PER_KV // 2, WINDOW, 4 * HEAD_DIM)
            den = out[:, :, 2 * HEAD_DIM:] + jnp.where(lane_low, sink_terms[0], sink_terms[1])
            o = out[:, :, :2 * HEAD_DIM] / den
            for pr in range(Q_PER_KV // 2):
                a_scr[r0:r1, q0 + pr * 2 * HEAD_DIM:q0 + (pr + 1) * 2 * HEAD_DIM] = (
                    o[pr].astype(BF16))

    for g in range(N_KV_HEADS):
        for s in range(2):
            kprev_scr[2 * g + s] = k_half[g][s][PROMPT_STEP - WINDOW:]
            vprev_scr[2 * g + s] = v_half[g][s][PROMPT_STEP - WINDOW:]

    u = _gelu(_project(xb, w_in_ref, b_in_ref, V_END, U_END))
    vs = _gelu(_project(xb, w_in_ref, b_in_ref, U_END, VS_END))
    vs_bf = _layer_norm(vs, sgg_ref[...], sgb_ref[...]).astype(BF16)
    for j in range(n_blocks):
        r0, r1 = j * WINDOW, (j + 1) * WINDOW
        for g in range(N_SG_GROUPS):
            c0, c1 = g * SG_GROUP_DIM, (g + 1) * SG_GROUP_DIM
            sg = _dot(tril_scr[g], vs_bf[r0:r1, c0:c1]) + sgbias_ref[g]
            s_scr[r0:r1, c0:c1] = (u[r0:r1, c0:c1] * sg).astype(BF16)

    ga = _project(xb, w_in_ref, b_in_ref, VS_END, GA_END)
    gb = _project(xb, w_in_ref, b_in_ref, GA_END, IN_W)
    x1_ref[...] = _merge_and_norm(x, a_scr[...], s_scr[...], ga, gb, wpa_ref, wpb_ref,
                                  wout_ref, ln1g_ref, ln1b_ref)


def _const_spec(shape):
    zeros = (0,) * len(shape)
    return pl.BlockSpec(shape, lambda *_: zeros, pipeline_mode=pl.Buffered(1))


def _mixer_prompt(x, extra_rows, table, sinks, w_in_bf, b_in, sgg, sgb, sgw, sgbias, wpa, wpb,
                  wout, ln1g, ln1b):
    batch, seq, _ = x.shape
    n_steps = seq // PROMPT_STEP
    dist = np.arange(WINDOW)[:, None] + WINDOW - np.arange(2 * WINDOW)[None, :]
    bucket = jnp.asarray(_t5_bucket_np(dist))
    sink_pairs = jnp.transpose(sinks.reshape(N_KV_HEADS, Q_PER_KV // 2, 2), (2, 0, 1)).reshape(
        2, N_KV_HEADS, Q_PER_KV // 2, 1, 1)
    smem = pl.BlockSpec(memory_space=pltpu.SMEM)
    in_specs = [
        smem,
        _const_spec((WINDOW, 2 * WINDOW)),
        _const_spec((2, N_KV_HEADS, Q_PER_KV // 2, 1, 1)),
        pl.BlockSpec((None, PROMPT_STEP, D_MODEL), lambda b, n: (b, n, 0)),
        _const_spec((D_MODEL, IN_W)),
        _const_spec((1, IN_W)),
        _const_spec((1, SG_WIDTH)),
        _const_spec((1, SG_WIDTH)),
        _const_spec((N_SG_GROUPS, CHUNK, CHUNK)),
        _const_spec((N_SG_GROUPS, CHUNK, 1)),
        _const_spec((Q_W, D_MODEL)),
        _const_spec((SG_WIDTH, D_MODEL)),
        _const_spec((D_MODEL, D_MODEL)),
        _const_spec((1, D_MODEL)),
        _const_spec((1, D_MODEL)),
    ]
    out_specs = [
        pl.BlockSpec((PROMPT_STEP, D_MODEL), lambda b, n: (b * n_steps + n, 0)),
        pl.BlockSpec((None, WINDOW, KV_W), lambda b, n: (b, 0, 0)),
        pl.BlockSpec((None, WINDOW, KV_W), lambda b, n: (b, 0, 0)),
    ]
    out_shape = [
        jax.ShapeDtypeStruct((batch * seq + extra_rows, D_MODEL), F32),
        jax.ShapeDtypeStruct((batch, WINDOW, KV_W), F32),
        jax.ShapeDtypeStruct((batch, WINDOW, KV_W), F32),
    ]
    scratch = [
        pltpu.VMEM((N_KV_HEADS, Q_PER_KV // 2, WINDOW, 4 * WINDOW), F32),
        pltpu.VMEM((N_SG_GROUPS, CHUNK, CHUNK), BF16),
        pltpu.VMEM((2 * N_KV_HEADS, WINDOW, KV_W), BF16),
        pltpu.VMEM((2 * N_KV_HEADS, WINDOW, KV_W), BF16),
        pltpu.VMEM((PROMPT_STEP, Q_W), BF16),
        pltpu.VMEM((PROMPT_STEP, SG_WIDTH), BF16),
    ]
    return pl.pallas_call(
        _mixer_prompt_kernel,
        grid=(batch, n_steps),
        in_specs=in_specs,
        out_specs=out_specs,
        out_shape=out_shape,
        scratch_shapes=scratch,
        compiler_params=pltpu.CompilerParams(
            dimension_semantics=("arbitrary", "arbitrary"),
            vmem_limit_bytes=VMEM_LIMIT_BYTES),
        name="mixer_prompt",
    )(table, bucket, sink_pairs, x, w_in_bf, b_in, sgg, sgb, sgw, sgbias, wpa, wpb, wout,
      ln1g, ln1b)


def _mixer_sample_kernel(table_ref, bucket_c_ref, bucket_n_ref, sink_ref, x_ref, ck_ref, cv_ref,
                         w_in_ref, b_in_ref, sgg_ref, sgb_ref, sgw8_ref, sgbias8_ref,
                         wpa_ref, wpb_ref, wout_ref, ln1g_ref, ln1b_ref, x1_in_ref,
                         x1_ref, wk_ref, wv_ref, cvs_ref,
                         bias_c_scr, bias_n_scr, a_scr, s_scr):
    del x1_in_ref
    nseq, ds = SAMPLE_SEQS_PER_STEP, x_ref.shape[1]
    rows = nseq * ds
    win = ck_ref.shape[1]
    qrows = Q_PER_KV * ds

    @pl.when(pl.program_id(0) == 0)
    def _init_tables():
        bc = bucket_c_ref[...]
        bn = bucket_n_ref[...]
        for h in range(N_Q_HEADS):
            g, r = divmod(h, Q_PER_KV)
            bias_c_scr[g, r * ds:(r + 1) * ds, :] = _expand_bias(bc, table_ref, h)
            bias_n_scr[g, r * ds:(r + 1) * ds, :] = _expand_bias(bn, table_ref, h)

    x = x_ref[...].reshape(rows, D_MODEL)
    xb = x.astype(BF16)
    q = _project(xb, w_in_ref, b_in_ref, 0, Q_END) * ATTN_SCALE
    k = _project(xb, w_in_ref, b_in_ref, Q_END, K_END)
    v = _project(xb, w_in_ref, b_in_ref, K_END, V_END)
    ck = ck_ref[...]
    cv = cv_ref[...]
    wk_ref[:, :win - ds, :] = ck[:, ds:, :]
    wk_ref[:, win - ds:, :] = k.reshape(nseq, ds, KV_W)
    wv_ref[:, :win - ds, :] = cv[:, ds:, :]
    wv_ref[:, win - ds:, :] = v.reshape(nseq, ds, KV_W)
    nk = wk_ref[...]
    nv = wv_ref[...]

    q3 = q.reshape(nseq, ds, Q_W)
    t_q = lax.broadcasted_iota(jnp.int32, (qrows, win), 0) % ds
    col = lax.broadcasted_iota(jnp.int32, (qrows, win), 1)
    dist_c = t_q + win - col
    valid_c = (dist_c >= 0) & (dist_c <= WINDOW)
    dist_n = t_q - (col - (win - ds))
    valid_n = (col >= win - ds) & (dist_n >= 0) & (dist_n <= WINDOW)

    for g in range(N_KV_HEADS):
        c0, c1 = g * HEAD_DIM, (g + 1) * HEAD_DIM
        h0 = g * Q_PER_KV
        qs = jnp.concatenate(
            [q3[:, :, (h0 + r) * HEAD_DIM:(h0 + r + 1) * HEAD_DIM] for r in range(Q_PER_KV)],
            axis=1).astype(BF16)
        kc = ck[:, :, c0:c1].astype(BF16)
        vc = cv[:, :, c0:c1].astype(BF16)
        kn = nk[:, :, c0:c1].astype(BF16)
        vn = nv[:, :, c0:c1].astype(BF16)
        lc = jnp.einsum('bqd,bkd->bqk', qs, kc, preferred_element_type=F32)
        ln = jnp.einsum('bqd,bkd->bqk', qs, kn, preferred_element_type=F32)
        lc = jnp.where(valid_c[None], lc + bias_c_scr[g][None], NEG_INF)
        ln = jnp.where(valid_n[None], ln + bias_n_scr[g][None], NEG_INF)
        sink = sink_ref[g]
        m = jnp.maximum(jnp.maximum(jnp.max(lc, -1, keepdims=True),
                                    jnp.max(ln, -1, keepdims=True)), sink[None])
        pc = jnp.exp(lc - m)
        pn = jnp.exp(ln - m)
        den = (jnp.sum(pc, -1, keepdims=True) + jnp.sum(pn, -1, keepdims=True)
               + jnp.exp(sink[None] - m))
        o = (jnp.einsum('bqk,bkd->bqd', pc.astype(BF16), vc, preferred_element_type=F32)
             + jnp.einsum('bqk,bkd->bqd', pn.astype(BF16), vn, preferred_element_type=F32))
        o = o / den
        for r in range(Q_PER_KV):
            a_scr[:, :, (h0 + r) * HEAD_DIM:(h0 + r + 1) * HEAD_DIM] = (
                o[:, r * ds:(r + 1) * ds, :])

    u = _gelu(_project(xb, w_in_ref, b_in_ref, V_END, U_END))
    vs = _gelu(_project(xb, w_in_ref, b_in_ref, U_END, VS_END))
    vs_ln = _layer_norm(vs, sgg_ref[...], sgb_ref[...])
    cvs_ref[...] = vs_ln.reshape(nseq, ds, SG_WIDTH)
    vq = vs_ln.astype(BF16).astype(F32).reshape(nseq, ds, SG_WIDTH)
    u3 = u.reshape(nseq, ds, SG_WIDTH)
    i_row = lax.broadcasted_iota(jnp.int32, (ds, 1), 0)
    for g in range(N_SG_GROUPS):
        c0, c1 = g * SG_GROUP_DIM, (g + 1) * SG_GROUP_DIM
        acc = jnp.broadcast_to(sgbias8_ref[g][None], (nseq, ds, SG_GROUP_DIM))
        for j in range(ds):
            w_col = jnp.where(i_row >= j, sgw8_ref[g, j], 0.0)
            w_col = w_col.astype(BF16).astype(F32)
            acc = acc + w_col[None] * vq[:, j:j + 1, c0:c1]
        s_scr[:, :, c0:c1] = u3[:, :, c0:c1] * acc

    ga = _project(xb, w_in_ref, b_in_ref, VS_END, GA_END)
    gb = _project(xb, w_in_ref, b_in_ref, GA_END, IN_W)
    x1 = _merge_and_norm(x, a_scr[...].reshape(rows, Q_W).astype(BF16),
                         s_scr[...].reshape(rows, SG_WIDTH).astype(BF16),
                         ga, gb, wpa_ref, wpb_ref, wout_ref, ln1g_ref, ln1b_ref)
    x1_ref[...] = x1


def _mixer_sample(x, cache_k, cache_v, x1_all, table, sinks, w_in_bf, b_in, sgg, sgb, sgw,
                  sgbias, wpa, wpb, wout, ln1g, ln1b):
    nb, ds, _ = x.shape
    first_block = (x1_all.shape[0] - nb * ds) // (SAMPLE_SEQS_PER_STEP * ds)
    win = cache_k.shape[1]
    nseq = SAMPLE_SEQS_PER_STEP
    qrows = Q_PER_KV * ds
    t = np.arange(ds)[:, None]
    bucket_c = jnp.asarray(_t5_bucket_np(t + win - np.arange(win)[None, :]))
    bucket_n = jnp.asarray(_t5_bucket_np(t - (np.arange(win)[None, :] - (win - ds))))
    sink_rows = jnp.repeat(sinks.reshape(N_KV_HEADS, Q_PER_KV), ds, axis=1).reshape(
        N_KV_HEADS, qrows, 1)
    sgw8 = jnp.transpose(sgw[:, :ds, :ds], (0, 2, 1))[..., None]
    sgbias8 = sgbias[:, :ds, :]
    smem = pl.BlockSpec(memory_space=pltpu.SMEM)
    in_specs = [
        smem,
        _const_spec((ds, win)),
        _const_spec((ds, win)),
        _const_spec((N_KV_HEADS, qrows, 1)),
        pl.BlockSpec((nseq, ds, D_MODEL), lambda i: (i, 0, 0)),
        pl.BlockSpec((nseq, win, KV_W), lambda i: (i, 0, 0)),
        pl.BlockSpec((nseq, win, KV_W), lambda i: (i, 0, 0)),
        _const_spec((D_MODEL, IN_W)),
        _const_spec((1, IN_W)),
        _const_spec((1, SG_WIDTH)),
        _const_spec((1, SG_WIDTH)),
        _const_spec((N_SG_GROUPS, ds, ds, 1)),
        _const_spec((N_SG_GROUPS, ds, 1)),
        _const_spec((Q_W, D_MODEL)),
        _const_spec((SG_WIDTH, D_MODEL)),
        _const_spec((D_MODEL, D_MODEL)),
        _const_spec((1, D_MODEL)),
        _const_spec((1, D_MODEL)),
        pl.BlockSpec(memory_space=pl.ANY),
    ]
    out_specs = [
        pl.BlockSpec((nseq * ds, D_MODEL), lambda i: (first_block + i, 0)),
        pl.BlockSpec((nseq, win, KV_W), lambda i: (i, 0, 0)),
        pl.BlockSpec((nseq, win, KV_W), lambda i: (i, 0, 0)),
        pl.BlockSpec((nseq, ds, SG_WIDTH), lambda i: (i, 0, 0)),
    ]
    out_shape = [
        jax.ShapeDtypeStruct(x1_all.shape, F32),
        jax.ShapeDtypeStruct((nb, win, KV_W), F32),
        jax.ShapeDtypeStruct((nb, win, KV_W), F32),
        jax.ShapeDtypeStruct((nb, ds, SG_WIDTH), F32),
    ]
    scratch = [
        pltpu.VMEM((N_KV_HEADS, qrows, win), F32),
        pltpu.VMEM((N_KV_HEADS, qrows, win), F32),
        pltpu.VMEM((nseq, ds, Q_W), F32),
        pltpu.VMEM((nseq, ds, SG_WIDTH), F32),
    ]
    return pl.pallas_call(
        _mixer_sample_kernel,
        grid=(nb // nseq,),
        in_specs=in_specs,
        out_specs=out_specs,
        out_shape=out_shape,
        scratch_shapes=scratch,
        input_output_aliases={len(in_specs) - 1: 0},
        compiler_params=pltpu.CompilerParams(
            dimension_semantics=("arbitrary",),
            vmem_limit_bytes=VMEM_LIMIT_BYTES),
        name="mixer_sample",
    )(table, bucket_c, bucket_n, sink_rows, x, cache_k, cache_v, w_in_bf, b_in, sgg, sgb,
      sgw8, sgbias8, wpa, wpb, wout, ln1g, ln1b, x1_all)


def _route(xb, wrt_ref, rbias_ref):
    n = xb.shape[0]
    logits = _dot_nt(wrt_ref[...].astype(BF16), xb)
    scores = jax.nn.sigmoid(logits)
    sel = scores + rbias_ref[...]
    shape3 = (N_EXPERT_GROUPS, EXPERTS_PER_GROUP, n)
    scores3 = scores.reshape(shape3)
    sel3 = sel.reshape(shape3)
    i_in = lax.broadcasted_iota(jnp.int32, shape3, 1)
    g_id = lax.broadcasted_iota(jnp.int32, shape3, 0)
    e_id = g_id * EXPERTS_PER_GROUP + i_in
    neg = -jnp.inf

    m1 = jnp.max(sel3, axis=1, keepdims=True)
    first = jnp.min(jnp.where(sel3 == m1, i_in, EXPERTS_PER_GROUP), axis=1, keepdims=True)
    m2 = jnp.max(jnp.where(i_in == first, neg, sel3), axis=1, keepdims=True)
    gscore = m1 + m2

    gsel = jnp.zeros(gscore.shape, jnp.bool_)
    gid1 = lax.broadcasted_iota(jnp.int32, gscore.shape, 0)
    for _ in range(TOPK_GROUPS):
        m = jnp.max(gscore, axis=0, keepdims=True)
        pick = jnp.min(jnp.where(gscore == m, gid1, N_EXPERT_GROUPS), axis=0, keepdims=True)
        chosen = gid1 == pick
        gsel = gsel | chosen
        gscore = jnp.where(chosen, neg, gscore)
    val = jnp.where(gsel, sel3, NEG_INF)

    esel = jnp.zeros(shape3, jnp.bool_)
    for _ in range(TOP_K):
        m = jnp.max(jnp.max(val, axis=0, keepdims=True), axis=1, keepdims=True)
        cand = jnp.where(val == m, e_id, N_EXPERTS)
        pick = jnp.min(jnp.min(cand, axis=0, keepdims=True), axis=1, keepdims=True)
        chosen = e_id == pick
        esel = esel | chosen
        val = jnp.where(chosen, neg, val)
    w_sel = jnp.where(esel, scores3, 0.0)
    total = jnp.sum(jnp.sum(w_sel, axis=0, keepdims=True), axis=1, keepdims=True)
    gates = w_sel / total * ROUTED_SCALE
    return gates.reshape(N_EXPERTS, n)


def _swiglu(xb, wg, wu):
    return jax.nn.silu(_dot(xb, wg)) * _dot(xb, wu)


def _bf(mask):
    return jnp.where(mask, 1.0, 0.0).astype(BF16)


def _round_up_run(count):
    return jnp.ceil(count * (1.0 / RUN_ALIGN)) * RUN_ALIGN


def _run_layout_rows(sel_bf):
    n = sel_bf.shape[1]
    ones = jnp.ones((16, n), BF16)
    pc = _round_up_run(_dot_nt(ones, sel_bf))
    lower = lax.broadcasted_iota(jnp.int32, (N_EXPERTS, N_EXPERTS), 0)
    upper = lax.broadcasted_iota(jnp.int32, (N_EXPERTS, N_EXPERTS), 1)
    off = _dot(pc.astype(BF16), _bf(lower < upper))
    return pc, off


def _run_layout_cols(sel_f32):
    cnt = jnp.sum(sel_f32, axis=1, keepdims=True)
    pc = jnp.broadcast_to(_round_up_run(cnt), (N_EXPERTS, 128))
    row = lax.broadcasted_iota(jnp.int32, (N_EXPERTS, N_EXPERTS), 0)
    col = lax.broadcasted_iota(jnp.int32, (N_EXPERTS, N_EXPERTS), 1)
    off = _dot(_bf(col < row), pc.astype(BF16))
    return pc, off


def _rank_in_run(sel_bf):
    n = sel_bf.shape[1]
    m_id = lax.broadcasted_iota(jnp.int32, (n, n), 0)
    n_id = lax.broadcasted_iota(jnp.int32, (n, n), 1)
    return _dot(sel_bf, _bf(m_id < n_id))


def _route_kernel(x_ref, wrt_ref, rbias_ref, gates_ref, pc_ref):
    xb = x_ref[...].astype(BF16)
    gates = _route(xb, wrt_ref, rbias_ref)
    gates_ref[...] = gates
    sel_bf = _bf(gates > 0.0)
    pad = jnp.zeros((128 - N_EXPERTS, sel_bf.shape[1]), BF16)
    ones = jnp.ones((16, sel_bf.shape[1]), BF16)
    cnt = _dot_nt(ones, jnp.concatenate([sel_bf, pad], axis=0))
    pc_ref[...] = _round_up_run(cnt)[:1].astype(jnp.int32)


def _moe_route(x1, wrt, rbias):
    n_tok = x1.shape[0]
    n_tiles = n_tok // MOE_TILE
    return pl.pallas_call(
        _route_kernel,
        grid=(n_tiles,),
        in_specs=[
            pl.BlockSpec((MOE_TILE, D_MODEL), lambda t: (t, 0)),
            _const_spec((N_EXPERTS, D_MODEL)),
            _const_spec((N_EXPERTS, 1)),
        ],
        out_specs=[
            pl.BlockSpec((N_EXPERTS, MOE_TILE), lambda t: (0, t)),
            pl.BlockSpec((None, 1, 128), lambda t: (t, 0, 0)),
        ],
        out_shape=[
            jax.ShapeDtypeStruct((N_EXPERTS, n_tok), F32),
            jax.ShapeDtypeStruct((n_tiles, 1, 128), jnp.int32),
        ],
        compiler_params=pltpu.CompilerParams(dimension_semantics=("arbitrary",)),
        name="moe_route",
    )(x1, wrt, rbias)


def _plan_kernel(pc_ref, gbase_ref, tchunks_ref, blkexp_ref, gapstart_ref, gapn_ref, meta_ref):
    n_tiles = pc_ref.shape[0]
    n_blocks_max = blkexp_ref.shape[0]

    def zero_tile(t, c):
        tchunks_ref[t] = 0
        return c

    lax.fori_loop(0, n_tiles, zero_tile, 0)

    def expert_body(e, carry):
        g0, b0 = carry

        def tile_body(t, run):
            n = pc_ref[t, e]
            gbase_ref[t, e] = g0 + run
            tchunks_ref[t] = tchunks_ref[t] + n // RUN_ALIGN
            return run + n

        rows = lax.fori_loop(0, n_tiles, tile_body, 0)
        nb = (rows + EXPERT_BLOCK - 1) // EXPERT_BLOCK

        def block_body(j, c):
            blkexp_ref[b0 + j] = e
            return c

        lax.fori_loop(0, nb, block_body, 0)
        gapstart_ref[e] = g0 + rows
        gapn_ref[e] = (nb * EXPERT_BLOCK - rows) // RUN_ALIGN
        return g0 + nb * EXPERT_BLOCK, b0 + nb

    _, n_blocks = lax.fori_loop(0, N_EXPERTS, expert_body, (0, 0))
    last = blkexp_ref[jnp.maximum(n_blocks - 1, 0)]

    def tail_body(j, c):
        blkexp_ref[j] = last
        return c

    lax.fori_loop(n_blocks, n_blocks_max, tail_body, 0)
    meta_ref[0] = n_blocks


def _max_blocks(n_tiles):
    per_tile = MOE_TILE * TOP_K + N_EXPERTS * (RUN_ALIGN - 1)
    rows = n_tiles * per_tile + N_EXPERTS * (EXPERT_BLOCK - RUN_ALIGN)
    return -(-rows // EXPERT_BLOCK)


def _moe_plan(pc):
    n_tiles = pc.shape[0]
    smem = pl.BlockSpec(memory_space=pltpu.SMEM)
    i32 = jnp.int32
    return pl.pallas_call(
        _plan_kernel,
        in_specs=[smem],
        out_specs=[smem] * 6,
        out_shape=[
            jax.ShapeDtypeStruct((n_tiles, N_EXPERTS), i32),
            jax.ShapeDtypeStruct((n_tiles,), i32),
            jax.ShapeDtypeStruct((_max_blocks(n_tiles),), i32),
            jax.ShapeDtypeStruct((N_EXPERTS,), i32),
            jax.ShapeDtypeStruct((N_EXPERTS,), i32),
            jax.ShapeDtypeStruct((1,), i32),
        ],
        name="moe_plan",
    )(pc)


def _aligned(row):
    return row if isinstance(row, int) else pl.multiple_of(row, RUN_ALIGN)


def _chunk_copy(src_ref, src_row, dst_ref, dst_row, sem):
    return pltpu.make_async_copy(
        src_ref.at[pl.ds(_aligned(src_row), RUN_ALIGN), :],
        dst_ref.at[pl.ds(_aligned(dst_row), RUN_ALIGN), :],
        sem)


def _wait_chunks(src_ref, dst_ref, sem, count):
    def body(i, c):
        _chunk_copy(src_ref, 0, dst_ref, 0, sem).wait()
        return c

    lax.fori_loop(0, count, body, 0)


def _dispatch_kernel(pc_ref, gbase_ref, tchunks_ref, gapstart_ref, gapn_ref,
                     x_ref, gates_ref, xg_ref, xs_scr, zero_scr, sems):
    t = pl.program_id(0)
    n_tiles = pl.num_programs(0)
    slot = t % 2

    @pl.when(t >= 2)
    def _drain_slot():
        _wait_chunks(xs_scr.at[slot], xg_ref, sems.at[slot], tchunks_ref[t - 2])

    gates = gates_ref[...]
    sel = gates > 0.0
    sel_bf = _bf(sel)
    rankp = jnp.where(sel, _rank_in_run(sel_bf), -1.0).astype(BF16)
    pc_row, off_row = _run_layout_rows(sel_bf)
    pc_col, off_col = _run_layout_cols(jnp.where(sel, 1.0, 0.0))
    s_id = lax.broadcasted_iota(jnp.int32, (TILE_SLOTS, N_EXPERTS), 0).astype(F32)
    owner = _bf((s_id >= off_row[:1]) & (s_id < off_row[:1] + pc_row[:1]))
    rank_of_slot = _dot(owner, rankp)
    off_of_slot = _dot(owner, (off_col * (1.0 / RUN_ALIGN)).astype(BF16)) * RUN_ALIGN
    s_lane = lax.broadcasted_iota(jnp.int32, (TILE_SLOTS, 128), 0).astype(F32)
    onehot = jnp.concatenate(
        [_bf(rank_of_slot[:, c * 128:(c + 1) * 128] + off_of_slot == s_lane)
         for c in range(MOE_TILE // 128)], axis=1)
    xb = x_ref[...].astype(BF16)
    for c in range(TILE_SLOTS // 512):
        xs_scr[slot, c * 512:(c + 1) * 512, :] = _dot(
            onehot[c * 512:(c + 1) * 512, :], xb).astype(BF16)

    def expert_body(e, src):
        n = pc_ref[t, e]
        dst = gbase_ref[t, e]

        def chunk_body(k, c):
            _chunk_copy(xs_scr.at[slot], src + k * RUN_ALIGN, xg_ref, dst + k * RUN_ALIGN,
                        sems.at[slot]).start()
            return c

        lax.fori_loop(0, n // RUN_ALIGN, chunk_body, 0)
        return src + n

    lax.fori_loop(0, N_EXPERTS, expert_body, 0)

    @pl.when(t == n_tiles - 1)
    def _finish():
        zero_scr[...] = jnp.zeros_like(zero_scr)

        def gap_body(e, total):
            n = gapn_ref[e]

            def chunk_body(k, c):
                _chunk_copy(zero_scr, 0, xg_ref, gapstart_ref[e] + k * RUN_ALIGN,
                            sems.at[2]).start()
                return c

            lax.fori_loop(0, n, chunk_body, 0)
            return total + n

        n_gap = lax.fori_loop(0, N_EXPERTS, gap_body, 0)
        _wait_chunks(zero_scr, xg_ref, sems.at[2], n_gap)
        _wait_chunks(xs_scr.at[slot], xg_ref, sems.at[slot], tchunks_ref[t])

        @pl.when(t >= 1)
        def _drain_other():
            _wait_chunks(xs_scr.at[1 - slot], xg_ref, sems.at[1 - slot], tchunks_ref[t - 1])


def _moe_dispatch(x1, gates, pc, gbase, tchunks, gapstart, gapn, n_rows):
    n_tiles = x1.shape[0] // MOE_TILE
    smem = pl.BlockSpec(memory_space=pltpu.SMEM)
    return pl.pallas_call(
        _dispatch_kernel,
        grid=(n_tiles,),
        in_specs=[
            smem, smem, smem, smem, smem,
            pl.BlockSpec((MOE_TILE, D_MODEL), lambda t: (t, 0)),
            pl.BlockSpec((N_EXPERTS, MOE_TILE), lambda t: (0, t)),
        ],
        out_specs=pl.BlockSpec(memory_space=pl.ANY),
        out_shape=jax.ShapeDtypeStruct((n_rows, D_MODEL), BF16),
        scratch_shapes=[
            pltpu.VMEM((2, TILE_SLOTS, D_MODEL), BF16),
            pltpu.VMEM((RUN_ALIGN, D_MODEL), BF16),
            pltpu.SemaphoreType.DMA((3,)),
        ],
        compiler_params=pltpu.CompilerParams(
            dimension_semantics=("arbitrary",), vmem_limit_bytes=VMEM_LIMIT_BYTES,
            has_side_effects=True),
        name="moe_dispatch",
    )(pc, gbase, tchunks, gapstart, gapn, x1, gates)


def _experts_kernel(blkexp_ref, meta_ref, xg_ref, wg_ref, wu_ref, wd_ref, yg_ref,
                    wg_scr, wu_scr, wd_scr):
    b = pl.program_id(0)

    @pl.when(b < meta_ref[0])
    def _block():
        changed = (b == 0) | (blkexp_ref[b] != blkexp_ref[jnp.maximum(b - 1, 0)])

        @pl.when(changed)
        def _load_weights():
            wg_scr[...] = wg_ref[...].astype(BF16)
            wu_scr[...] = wu_ref[...].astype(BF16)
            wd_scr[...] = wd_ref[...].astype(BF16)

        h = _swiglu(xg_ref[...], wg_scr[...], wu_scr[...])
        yg_ref[...] = _dot(h.astype(BF16), wd_scr[...]).astype(BF16)


def _moe_experts(blkexp, meta, xg, wg, wu, wd):
    n_blocks = blkexp.shape[0]

    def row_map(b, blkexp_ref, meta_ref):
        return (jnp.minimum(b, meta_ref[0] - 1), 0)

    def w_map(b, blkexp_ref, meta_ref):
        return (blkexp_ref[b], 0, 0)

    grid_spec = pltpu.PrefetchScalarGridSpec(
        num_scalar_prefetch=2,
        grid=(n_blocks,),
        in_specs=[
            pl.BlockSpec((EXPERT_BLOCK, D_MODEL), row_map),
            pl.BlockSpec((None, D_MODEL, EXPERT_DIM), w_map),
            pl.BlockSpec((None, D_MODEL, EXPERT_DIM), w_map),
            pl.BlockSpec((None, EXPERT_DIM, D_MODEL), w_map),
        ],
        out_specs=pl.BlockSpec((EXPERT_BLOCK, D_MODEL), row_map),
        scratch_shapes=[
            pltpu.VMEM((D_MODEL, EXPERT_DIM), BF16),
            pltpu.VMEM((D_MODEL, EXPERT_DIM), BF16),
            pltpu.VMEM((EXPERT_DIM, D_MODEL), BF16),
        ],
    )
    return pl.pallas_call(
        _experts_kernel,
        grid_spec=grid_spec,
        out_shape=jax.ShapeDtypeStruct(xg.shape, BF16),
        compiler_params=pltpu.CompilerParams(
            dimension_semantics=("arbitrary",), vmem_limit_bytes=VMEM_LIMIT_BYTES),
        name="moe_experts",
    )(blkexp, meta, xg, wg, wu, wd)


def _combine_kernel(pc_ref, gbase_ref, tchunks_ref,
                    x_ref, gates_ref, yg_ref, wgs_ref, wus_ref, wds_ref, ln2g_ref, ln2b_ref,
                    out_a_ref, out_b_ref, ys_scr, sems, *, tiles_a):
    t = pl.program_id(0)
    n_tiles = pl.num_programs(0)
    slot = t % 2

    def start_gather(tile, dst_slot):
        def expert_body(e, dst):
            n = pc_ref[tile, e]
            src = gbase_ref[tile, e]

            def chunk_body(k, c):
                _chunk_copy(yg_ref, src + k * RUN_ALIGN, ys_scr.at[dst_slot],
                            dst + k * RUN_ALIGN, sems.at[dst_slot]).start()
                return c

            lax.fori_loop(0, n // RUN_ALIGN, chunk_body, 0)
            return dst + n

        lax.fori_loop(0, N_EXPERTS, expert_body, 0)

    @pl.when(t == 0)
    def _prime():
        ys_scr[...] = jnp.zeros_like(ys_scr)
        start_gather(0, 0)

    @pl.when(t + 1 < n_tiles)
    def _prefetch():
        start_gather(t + 1, 1 - slot)

    gates = gates_ref[...]
    sel = gates > 0.0
    sel_bf = _bf(sel)
    rankp = jnp.where(sel, _rank_in_run(sel_bf), -1.0)
    pc_row, off_row = _run_layout_rows(sel_bf)
    pc_col, off_col = _run_layout_cols(jnp.where(sel, 1.0, 0.0))
    pad = jnp.zeros((128 - N_EXPERTS, MOE_TILE), F32)
    gates_t = jnp.concatenate([gates, pad], axis=0).T.astype(BF16)
    rankp_t = jnp.concatenate([rankp, pad], axis=0).T.astype(BF16)
    s_id = lax.broadcasted_iota(jnp.int32, (N_EXPERTS, TILE_SLOTS), 1).astype(F32)
    owner = _bf((s_id >= off_col[:, :1]) & (s_id < off_col[:, :1] + pc_col[:, :1]))
    owner = jnp.concatenate([owner, jnp.zeros((128 - N_EXPERTS, TILE_SLOTS), BF16)], axis=0)
    rank_of_slot = _dot(rankp_t, owner)
    gate_of_slot = _dot(gates_t, owner)
    o16 = jnp.concatenate([off_row * (1.0 / RUN_ALIGN), jnp.zeros((16, 128 - N_EXPERTS), F32)],
                          axis=1).astype(BF16)
    off_of_slot = _dot(o16, owner)[:1] * RUN_ALIGN
    s_row = lax.broadcasted_iota(jnp.int32, (1, TILE_SLOTS), 1).astype(F32)
    weights = jnp.where(rank_of_slot + off_of_slot == s_row, gate_of_slot, 0.0).astype(BF16)

    x = x_ref[...]
    xb = x.astype(BF16)
    hs = _swiglu(xb, wgs_ref[...].astype(BF16), wus_ref[...].astype(BF16))
    shared = _dot(hs.astype(BF16), wds_ref[...].astype(BF16))

    _wait_chunks(yg_ref, ys_scr.at[slot], sems.at[slot], tchunks_ref[t])
    routed = _dot(weights, ys_scr[slot])
    y = _layer_norm(ALPHA * x + (routed + shared), ln2g_ref[...], ln2b_ref[...])

    @pl.when(t < tiles_a)
    def _store_a():
        out_a_ref[...] = y

    @pl.when(t >= tiles_a)
    def _store_b():
        out_b_ref[...] = y


def _moe_combine(x1, gates, yg, pc, gbase, tchunks, wgs, wus, wds, ln2g, ln2b, rows_a):
    n_tok = x1.shape[0]
    tiles_a = rows_a // MOE_TILE
    smem = pl.BlockSpec(memory_space=pltpu.SMEM)
    return pl.pallas_call(
        functools.partial(_combine_kernel, tiles_a=tiles_a),
        grid=(n_tok // MOE_TILE,),
        in_specs=[
            smem, smem, smem,
            pl.BlockSpec((MOE_TILE, D_MODEL), lambda t: (t, 0)),
            pl.BlockSpec((N_EXPERTS, MOE_TILE), lambda t: (0, t)),
            pl.BlockSpec(memory_space=pl.ANY),
            _const_spec((D_MODEL, SHARED_DIM)),
            _const_spec((D_MODEL, SHARED_DIM)),
            _const_spec((SHARED_DIM, D_MODEL)),
            _const_spec((1, D_MODEL)),
            _const_spec((1, D_MODEL)),
        ],
        out_specs=[
            pl.BlockSpec((MOE_TILE, D_MODEL), lambda t: (jnp.minimum(t, tiles_a - 1), 0)),
            pl.BlockSpec((MOE_TILE, D_MODEL), lambda t: (jnp.maximum(t - tiles_a, 0), 0)),
        ],
        out_shape=[
            jax.ShapeDtypeStruct((rows_a, D_MODEL), F32),
            jax.ShapeDtypeStruct((n_tok - rows_a, D_MODEL), F32),
        ],
        scratch_shapes=[
            pltpu.VMEM((2, TILE_SLOTS, D_MODEL), BF16),
            pltpu.SemaphoreType.DMA((2,)),
        ],
        compiler_params=pltpu.CompilerParams(
            dimension_semantics=("arbitrary",), vmem_limit_bytes=VMEM_LIMIT_BYTES),
        name="moe_combine",
    )(pc, gbase, tchunks, x1, gates, yg, wgs, wus, wds, ln2g, ln2b)


def _moe(x1, rows_a, wrt, rbias, wg, wu, wd, wgs, wus, wds, ln2g, ln2b):
    n_tiles = x1.shape[0] // MOE_TILE
    gates, pc3 = _moe_route(x1, wrt, rbias)
    pc = pc3.reshape(n_tiles, 128)
    gbase, tchunks, blkexp, gapstart, gapn, meta = _moe_plan(pc)
    n_rows = blkexp.shape[0] * EXPERT_BLOCK
    xg = _moe_dispatch(x1, gates, pc, gbase, tchunks, gapstart, gapn, n_rows)
    yg = _moe_experts(blkexp, meta, xg, wg, wu, wd)
    return _moe_combine(x1, gates, yg, pc, gbase, tchunks, wgs, wus, wds, ln2g, ln2b, rows_a)


def kernel(x_prompt, x_sample, cache_win_k, cache_win_v, rel_bias_table, w_in, b_in, attn_sinks,
           sg_ln_g, sg_ln_b, sg_w, sg_b, w_proj_a, w_proj_b, w_out, ln1_g, ln1_b, w_router,
           router_bias, w_gate_e, w_up_e, w_down_e, w_gate_s, w_up_s, w_down_s, ln2_g, ln2_b):
    assert DEPTH == 1 and w_in.shape[0] == 1
    batch, seq, _ = x_prompt.shape
    nb, ds, _ = x_sample.shape
    win = cache_win_k.shape[2]

    w_in_bf = w_in[0].astype(BF16)
    wpa = w_proj_a[0].astype(BF16)
    wpb = w_proj_b[0].astype(BF16)
    wout = w_out[0].astype(BF16)
    b_in2 = b_in[0].reshape(1, IN_W)
    sgg = sg_ln_g[0].reshape(1, SG_WIDTH)
    sgb = sg_ln_b[0].reshape(1, SG_WIDTH)
    sgw = sg_w[0]
    sgbias = sg_b[0].reshape(N_SG_GROUPS, CHUNK, 1)
    ln1g = ln1_g[0].reshape(1, D_MODEL)
    ln1b = ln1_b[0].reshape(1, D_MODEL)
    sinks = attn_sinks[0]

    x1_p, wk_p, wv_p = _mixer_prompt(
        x_prompt, nb * ds, rel_bias_table, sinks, w_in_bf, b_in2, sgg, sgb, sgw, sgbias, wpa, wpb,
        wout, ln1g, ln1b)
    x1, wk_s, wv_s, cvs = _mixer_sample(
        x_sample, cache_win_k[0].reshape(nb, win, KV_W), cache_win_v[0].reshape(nb, win, KV_W),
        x1_p, rel_bias_table, sinks, w_in_bf, b_in2, sgg, sgb, sgw, sgbias, wpa, wpb, wout,
        ln1g, ln1b)

    y_p, y_s = _moe(x1, batch * seq, wrt=w_router[0].T,
                    rbias=router_bias[0].reshape(N_EXPERTS, 1),
                    wg=w_gate_e[0], wu=w_up_e[0], wd=w_down_e[0],
                    wgs=w_gate_s[0], wus=w_up_s[0], wds=w_down_s[0],
                    ln2g=ln2_g[0].reshape(1, D_MODEL), ln2b=ln2_b[0].reshape(1, D_MODEL))
    y_p = y_p.reshape(batch, seq, D_MODEL)
    y_s = y_s.reshape(nb, ds, D_MODEL)

    kv_shape = (1, -1, WINDOW, N_KV_HEADS, HEAD_DIM)
    return (y_p, y_s,
            wk_p.reshape(kv_shape), wv_p.reshape(kv_shape),
            wk_s.reshape(1, nb, win, N_KV_HEADS, HEAD_DIM),
            wv_s.reshape(1, nb, win, N_KV_HEADS, HEAD_DIM),
            cvs.reshape(1, nb, ds, N_SG_GROUPS, SG_GROUP_DIM))
```

```python
import functools
import math

import jax
import jax.numpy as jnp
import numpy as np
from jax import lax
from jax.experimental import pallas as pl
from jax.experimental.pallas import tpu as pltpu

F32 = jnp.float32
BF16 = jnp.bfloat16

D_MODEL = 1024
DEPTH = 1
HEAD_DIM = 64
N_Q_HEADS = 16
N_KV_HEADS = 2
Q_PER_KV = N_Q_HEADS // N_KV_HEADS
WINDOW = 128
ATTN_SCALE = HEAD_DIM ** -0.5
NEG_INF = -1e30
N_BUCKETS = 32
BUCKET_MAX_EXACT = 16
BUCKET_MAX_DIST = 128
CHUNK = 128
N_SG_GROUPS = 4
SG_GROUP_DIM = 128
SG_WIDTH = N_SG_GROUPS * SG_GROUP_DIM
Q_W = N_Q_HEADS * HEAD_DIM
KV_W = N_KV_HEADS * HEAD_DIM
Q_END = Q_W
K_END = Q_END + KV_W
V_END = K_END + KV_W
U_END = V_END + SG_WIDTH
VS_END = U_END + SG_WIDTH
GA_END = VS_END + D_MODEL
IN_W = GA_END + D_MODEL
N_EXPERTS = 64
TOP_K = 8
N_EXPERT_GROUPS = 8
EXPERTS_PER_GROUP = N_EXPERTS // N_EXPERT_GROUPS
TOPK_GROUPS = 4
EXPERT_DIM = 256
SHARED_DIM = 256
ROUTED_SCALE = 2.5
ALPHA = (2 * DEPTH) ** 0.25
LN_EPS = 1e-5

VMEM_LIMIT_BYTES = 56 * 1024 * 1024

PROMPT_STEP = 512
SAMPLE_SEQS_PER_STEP = 32
MOE_TILE = 256
RUN_ALIGN = 16
COPY_ROWS = 4 * RUN_ALIGN
EXPERT_BLOCK = 512
TILE_SLOTS = -(-(MOE_TILE * TOP_K + N_EXPERTS * (RUN_ALIGN - 1)) // 512) * 512


def _t5_bucket_np(dist):
    d = np.maximum(dist, 0)
    ratio = np.maximum(d, 1).astype(np.float32) / np.float32(BUCKET_MAX_EXACT)
    large = BUCKET_MAX_EXACT + (
        np.log(ratio) / np.float32(math.log(BUCKET_MAX_DIST / BUCKET_MAX_EXACT))
        * np.float32(N_BUCKETS - BUCKET_MAX_EXACT)).astype(np.int32)
    large = np.minimum(large, N_BUCKETS - 1)
    return np.where(d < BUCKET_MAX_EXACT, d, large).astype(np.int32)


def _layer_norm(x, g, b):
    mu = jnp.mean(x, -1, keepdims=True)
    xc = x - mu
    var = jnp.mean(xc * xc, -1, keepdims=True)
    return xc * lax.rsqrt(var + LN_EPS) * g + b


def _gelu(x):
    return jax.nn.gelu(x)


def _dot(a, b):
    return jnp.dot(a, b, preferred_element_type=F32)


def _dot_nt(a, b):
    return lax.dot_general(a, b, (((1,), (1,)), ((), ())), preferred_element_type=F32)


def _project(xb, w_in_ref, b_in_ref, lo, hi):
    return _dot(xb, w_in_ref[:, lo:hi]) + b_in_ref[:, lo:hi]


def _expand_bias(bucket, table_ref, head):
    acc = jnp.zeros(bucket.shape, F32)
    for b in range(N_BUCKETS):
        acc = jnp.where(bucket == b, table_ref[b, head], acc)
    return acc


def _merge_and_norm(x, a_bf, s_bf, ga, gb, wpa_ref, wpb_ref, wout_ref, g_ref, b_ref):
    pa = _dot(a_bf, wpa_ref[...])
    pb = _dot(s_bf, wpb_ref[...])
    hpre = jax.nn.sigmoid(ga) * pa + jax.nn.sigmoid(gb) * pb
    h = _dot(hpre.astype(BF16), wout_ref[...])
    return _layer_norm(ALPHA * x + h, g_ref[...], b_ref[...])


def _mixer_prompt_kernel(table_ref, bucket_ref, sink_ref, x_ref, w_in_ref, b_in_ref,
                         sgg_ref, sgb_ref, sgw_ref, sgbias_ref, wpa_ref, wpb_ref, wout_ref,
                         ln1g_ref, ln1b_ref,
                         x1_ref, wk_ref, wv_ref,
                         bias_scr, tril_scr, kprev_scr, vprev_scr, a_scr, s_scr):
    b_idx = pl.program_id(0)
    n_idx = pl.program_id(1)
    n_blocks = PROMPT_STEP // WINDOW

    @pl.when((b_idx == 0) & (n_idx == 0))
    def _init_tables():
        bucket = bucket_ref[...]
        for h in range(N_Q_HEADS):
            g, r = divmod(h, Q_PER_KV)
            pair, parity = divmod(r, 2)
            bias_scr[g, pair, :, parity * 2 * WINDOW:(parity + 1) * 2 * WINDOW] = (
                _expand_bias(bucket, table_ref, h))
        row = lax.broadcasted_iota(jnp.int32, (CHUNK, CHUNK), 0)
        col = lax.broadcasted_iota(jnp.int32, (CHUNK, CHUNK), 1)
        for g in range(N_SG_GROUPS):
            tril_scr[g] = jnp.where(row >= col, sgw_ref[g], 0.0).astype(BF16)

    @pl.when(n_idx == 0)
    def _reset_carry():
        kprev_scr[...] = jnp.zeros_like(kprev_scr)
        vprev_scr[...] = jnp.zeros_like(vprev_scr)

    x = x_ref[...]
    xb = x.astype(BF16)
    q_bf = (_project(xb, w_in_ref, b_in_ref, 0, Q_END) * ATTN_SCALE).astype(BF16)
    k = _project(xb, w_in_ref, b_in_ref, Q_END, K_END)
    v = _project(xb, w_in_ref, b_in_ref, K_END, V_END)

    @pl.when(n_idx == pl.num_programs(1) - 1)
    def _emit_window():
        wk_ref[...] = k[PROMPT_STEP - WINDOW:, :]
        wv_ref[...] = v[PROMPT_STEP - WINDOW:, :]

    low = lax.broadcasted_iota(jnp.int32, (PROMPT_STEP, KV_W), 1) < HEAD_DIM

    def lane_halves(t):
        t_sw = pltpu.roll(t, HEAD_DIM, axis=1)
        zero = jnp.zeros_like(t)
        return [[jnp.where(low, t, zero).astype(BF16), jnp.where(low, zero, t_sw).astype(BF16)],
                [jnp.where(low, t_sw, zero).astype(BF16), jnp.where(low, zero, t).astype(BF16)]]

    k_half = lane_halves(k)
    v_half = lane_halves(v)

    row = lax.broadcasted_iota(jnp.int32, (WINDOW, 4 * WINDOW), 0)
    col = lax.broadcasted_iota(jnp.int32, (WINDOW, 4 * WINDOW), 1) % (2 * WINDOW)
    dist = row + WINDOW - col
    band_valid = (dist >= 0) & (dist <= WINDOW)
    first_valid = band_valid & ((col >= WINDOW) | (n_idx > 0))
    lane_low = lax.broadcasted_iota(jnp.int32, (Q_PER_KV // 2, WINDOW, 2 * HEAD_DIM), 2) < HEAD_DIM
    ones_rows = lax.broadcasted_iota(jnp.int32, (4 * WINDOW, 2 * HEAD_DIM), 0) < 2 * WINDOW
    ones_cols = lax.broadcasted_iota(jnp.int32, (4 * WINDOW, 2 * HEAD_DIM), 1) < HEAD_DIM
    sum_block = _bf(ones_rows == ones_cols)

    for j in range(n_blocks):
        r0, r1 = j * WINDOW, (j + 1) * WINDOW
        valid = first_valid if j == 0 else band_valid
        for g in range(N_KV_HEADS):
            def band(cur, prev_scr):
                parts = []
                for s in range(2):
                    prev = prev_scr[2 * g + s] if j == 0 else cur[g][s][r0 - WINDOW:r0]
                    parts += [prev, cur[g][s][r0:r1]]
                return jnp.concatenate(parts, axis=0)

            kd = band(k_half, kprev_scr)
            vd = jnp.concatenate([band(v_half, vprev_scr), sum_block], axis=1)
            q0 = g * Q_PER_KV * HEAD_DIM
            qp = jnp.concatenate(
                [q_bf[r0:r1, q0 + pr * 2 * HEAD_DIM:q0 + (pr + 1) * 2 * HEAD_DIM]
                 for pr in range(Q_PER_KV // 2)], axis=0)
            logits = _dot_nt(qp, kd).reshape(Q_PER_KV // 2, WINDOW, 4 * WINDOW)
            logits = jnp.where(valid[None], logits + bias_scr[g], NEG_INF)
            probs, sink_terms = [], []
            for s in range(2):
                l_s = logits[:, :, s * 2 * WINDOW:(s + 1) * 2 * WINDOW]
                sink = sink_ref[s, g]
                m = jnp.maximum(jnp.max(l_s, -1, keepdims=True), sink)
                probs.append(jnp.exp(l_s - m))
                sink_terms.append(jnp.broadcast_to(jnp.exp(sink - m), lane_low.shape))
            p = jnp.concatenate(probs, axis=-1).reshape(4 * WINDOW, 4 * WINDOW).astype(BF16)
            out = _dot(p, vd).reshape(Q_PER_KV // 2, WINDOW, 4 * HEAD_DIM)
            den = out[:, :, 2 * HEAD_DIM:] + jnp.where(lane_low, sink_terms[0], sink_terms[1])
            o = out[:, :, :2 * HEAD_DIM] / den
            for pr in range(Q_PER_KV // 2):
                a_scr[r0:r1, q0 + pr * 2 * HEAD_DIM:q0 + (pr + 1) * 2 * HEAD_DIM] = (
                    o[pr].astype(BF16))

    for g in range(N_KV_HEADS):
        for s in range(2):
            kprev_scr[2 * g + s] = k_half[g][s][PROMPT_STEP - WINDOW:]
            vprev_scr[2 * g + s] = v_half[g][s][PROMPT_STEP - WINDOW:]

    u = _gelu(_project(xb, w_in_ref, b_in_ref, V_END, U_END))
    vs = _gelu(_project(xb, w_in_ref, b_in_ref, U_END, VS_END))
    vs_bf = _layer_norm(vs, sgg_ref[...], sgb_ref[...]).astype(BF16)
    for j in range(n_blocks):
        r0, r1 = j * WINDOW, (j + 1) * WINDOW
        for g in range(N_SG_GROUPS):
            c0, c1 = g * SG_GROUP_DIM, (g + 1) * SG_GROUP_DIM
            sg = _dot(tril_scr[g], vs_bf[r0:r1, c0:c1]) + sgbias_ref[g]
            s_scr[r0:r1, c0:c1] = (u[r0:r1, c0:c1] * sg).astype(BF16)

    ga = _project(xb, w_in_ref, b_in_ref, VS_END, GA_END)
    gb = _project(xb, w_in_ref, b_in_ref, GA_END, IN_W)
    x1_ref[...] = _merge_and_norm(x, a_scr[...], s_scr[...], ga, gb, wpa_ref, wpb_ref,
                                  wout_ref, ln1g_ref, ln1b_ref)


def _const_spec(shape):
    zeros = (0,) * len(shape)
    return pl.BlockSpec(shape, lambda *_: zeros, pipeline_mode=pl.Buffered(1))


def _mixer_prompt(x, extra_rows, table, sinks, w_in_bf, b_in, sgg, sgb, sgw, sgbias, wpa, wpb,
                  wout, ln1g, ln1b):
    batch, seq, _ = x.shape
    n_steps = seq // PROMPT_STEP
    dist = np.arange(WINDOW)[:, None] + WINDOW - np.arange(2 * WINDOW)[None, :]
    bucket = jnp.asarray(_t5_bucket_np(dist))
    sink_pairs = jnp.transpose(sinks.reshape(N_KV_HEADS, Q_PER_KV // 2, 2), (2, 0, 1)).reshape(
        2, N_KV_HEADS, Q_PER_KV // 2, 1, 1)
    smem = pl.BlockSpec(memory_space=pltpu.SMEM)
    in_specs = [
        smem,
        _const_spec((WINDOW, 2 * WINDOW)),
        _const_spec((2, N_KV_HEADS, Q_PER_KV // 2, 1, 1)),
        pl.BlockSpec((None, PROMPT_STEP, D_MODEL), lambda b, n: (b, n, 0)),
        _const_spec((D_MODEL, IN_W)),
        _const_spec((1, IN_W)),
        _const_spec((1, SG_WIDTH)),
        _const_spec((1, SG_WIDTH)),
        _const_spec((N_SG_GROUPS, CHUNK, CHUNK)),
        _const_spec((N_SG_GROUPS, CHUNK, 1)),
        _const_spec((Q_W, D_MODEL)),
        _const_spec((SG_WIDTH, D_MODEL)),
        _const_spec((D_MODEL, D_MODEL)),
        _const_spec((1, D_MODEL)),
        _const_spec((1, D_MODEL)),
    ]
    out_specs = [
        pl.BlockSpec((PROMPT_STEP, D_MODEL), lambda b, n: (b * n_steps + n, 0)),
        pl.BlockSpec((None, WINDOW, KV_W), lambda b, n: (b, 0, 0)),
        pl.BlockSpec((None, WINDOW, KV_W), lambda b, n: (b, 0, 0)),
    ]
    out_shape = [
        jax.ShapeDtypeStruct((batch * seq + extra_rows, D_MODEL), F32),
        jax.ShapeDtypeStruct((batch, WINDOW, KV_W), F32),
        jax.ShapeDtypeStruct((batch, WINDOW, KV_W), F32),
    ]
    scratch = [
        pltpu.VMEM((N_KV_HEADS, Q_PER_KV // 2, WINDOW, 4 * WINDOW), F32),
        pltpu.VMEM((N_SG_GROUPS, CHUNK, CHUNK), BF16),
        pltpu.VMEM((2 * N_KV_HEADS, WINDOW, KV_W), BF16),
        pltpu.VMEM((2 * N_KV_HEADS, WINDOW, KV_W), BF16),
        pltpu.VMEM((PROMPT_STEP, Q_W), BF16),
        pltpu.VMEM((PROMPT_STEP, SG_WIDTH), BF16),
    ]
    return pl.pallas_call(
        _mixer_prompt_kernel,
        grid=(batch, n_steps),
        in_specs=in_specs,
        out_specs=out_specs,
        out_shape=out_shape,
        scratch_shapes=scratch,
        compiler_params=pltpu.CompilerParams(
            dimension_semantics=("arbitrary", "arbitrary"),
            vmem_limit_bytes=VMEM_LIMIT_BYTES),
        name="mixer_prompt",
    )(table, bucket, sink_pairs, x, w_in_bf, b_in, sgg, sgb, sgw, sgbias, wpa, wpb, wout,
      ln1g, ln1b)


def _mixer_sample_kernel(table_ref, bucket_c_ref, bucket_n_ref, sink_ref, x_ref, ck_ref, cv_ref,
                         w_in_ref, b_in_ref, sgg_ref, sgb_ref, sgw8_ref, sgbias8_ref,
                         wpa_ref, wpb_ref, wout_ref, ln1g_ref, ln1b_ref, x1_in_ref,
                         x1_ref, wk_ref, wv_ref, cvs_ref,
                         bias_c_scr, bias_n_scr, a_scr, s_scr):
    del x1_in_ref
    nseq, ds = SAMPLE_SEQS_PER_STEP, x_ref.shape[1]
    rows = nseq * ds
    win = ck_ref.shape[1]
    qrows = Q_PER_KV * ds

    @pl.when(pl.program_id(0) == 0)
    def _init_tables():
        bc = bucket_c_ref[...]
        bn = bucket_n_ref[...]
        for h in range(N_Q_HEADS):
            g, r = divmod(h, Q_PER_KV)
            bias_c_scr[g, r * ds:(r + 1) * ds, :] = _expand_bias(bc, table_ref, h)
            bias_n_scr[g, r * ds:(r + 1) * ds, :] = _expand_bias(bn, table_ref, h)

    x = x_ref[...].reshape(rows, D_MODEL)
    xb = x.astype(BF16)
    q = _project(xb, w_in_ref, b_in_ref, 0, Q_END) * ATTN_SCALE
    k = _project(xb, w_in_ref, b_in_ref, Q_END, K_END)
    v = _project(xb, w_in_ref, b_in_ref, K_END, V_END)
    ck = ck_ref[...]
    cv = cv_ref[...]
    wk_ref[:, :win - ds, :] = ck[:, ds:, :]
    wk_ref[:, win - ds:, :] = k.reshape(nseq, ds, KV_W)
    wv_ref[:, :win - ds, :] = cv[:, ds:, :]
    wv_ref[:, win - ds:, :] = v.reshape(nseq, ds, KV_W)
    nk = wk_ref[...]
    nv = wv_ref[...]

    q3 = q.reshape(nseq, ds, Q_W)
    t_q = lax.broadcasted_iota(jnp.int32, (qrows, win), 0) % ds
    col = lax.broadcasted_iota(jnp.int32, (qrows, win), 1)
    dist_c = t_q + win - col
    valid_c = (dist_c >= 0) & (dist_c <= WINDOW)
    dist_n = t_q - (col - (win - ds))
    valid_n = (col >= win - ds) & (dist_n >= 0) & (dist_n <= WINDOW)

    for g in range(N_KV_HEADS):
        c0, c1 = g * HEAD_DIM, (g + 1) * HEAD_DIM
        h0 = g * Q_PER_KV
        qs = jnp.concatenate(
            [q3[:, :, (h0 + r) * HEAD_DIM:(h0 + r + 1) * HEAD_DIM] for r in range(Q_PER_KV)],
            axis=1).astype(BF16)
        kc = ck[:, :, c0:c1].astype(BF16)
        vc = cv[:, :, c0:c1].astype(BF16)
        kn = nk[:, :, c0:c1].astype(BF16)
        vn = nv[:, :, c0:c1].astype(BF16)
        lc = jnp.einsum('bqd,bkd->bqk', qs, kc, preferred_element_type=F32)
        ln = jnp.einsum('bqd,bkd->bqk', qs, kn, preferred_element_type=F32)
        lc = jnp.where(valid_c[None], lc + bias_c_scr[g][None], NEG_INF)
        ln = jnp.where(valid_n[None], ln + bias_n_scr[g][None], NEG_INF)
        sink = sink_ref[g]
        m = jnp.maximum(jnp.maximum(jnp.max(lc, -1, keepdims=True),
                                    jnp.max(ln, -1, keepdims=True)), sink[None])
        pc = jnp.exp(lc - m)
        pn = jnp.exp(ln - m)
        den = (jnp.sum(pc, -1, keepdims=True) + jnp.sum(pn, -1, keepdims=True)
               + jnp.exp(sink[None] - m))
        o = (jnp.einsum('bqk,bkd->bqd', pc.astype(BF16), vc, preferred_element_type=F32)
             + jnp.einsum('bqk,bkd->bqd', pn.astype(BF16), vn, preferred_element_type=F32))
        o = o / den
        for r in range(Q_PER_KV):
            a_scr[:, :, (h0 + r) * HEAD_DIM:(h0 + r + 1) * HEAD_DIM] = (
                o[:, r * ds:(r + 1) * ds, :])

    u = _gelu(_project(xb, w_in_ref, b_in_ref, V_END, U_END))
    vs = _gelu(_project(xb, w_in_ref, b_in_ref, U_END, VS_END))
    vs_ln = _layer_norm(vs, sgg_ref[...], sgb_ref[...])
    cvs_ref[...] = vs_ln.reshape(nseq, ds, SG_WIDTH)
    vq = vs_ln.astype(BF16).astype(F32).reshape(nseq, ds, SG_WIDTH)
    u3 = u.reshape(nseq, ds, SG_WIDTH)
    i_row = lax.broadcasted_iota(jnp.int32, (ds, 1), 0)
    for g in range(N_SG_GROUPS):
        c0, c1 = g * SG_GROUP_DIM, (g + 1) * SG_GROUP_DIM
        acc = jnp.broadcast_to(sgbias8_ref[g][None], (nseq, ds, SG_GROUP_DIM))
        for j in range(ds):
            w_col = jnp.where(i_row >= j, sgw8_ref[g, j], 0.0)
            w_col = w_col.astype(BF16).astype(F32)
            acc = acc + w_col[None] * vq[:, j:j + 1, c0:c1]
        s_scr[:, :, c0:c1] = u3[:, :, c0:c1] * acc

    ga = _project(xb, w_in_ref, b_in_ref, VS_END, GA_END)
    gb = _project(xb, w_in_ref, b_in_ref, GA_END, IN_W)
    x1 = _merge_and_norm(x, a_scr[...].reshape(rows, Q_W).astype(BF16),
                         s_scr[...].reshape(rows, SG_WIDTH).astype(BF16),
                         ga, gb, wpa_ref, wpb_ref, wout_ref, ln1g_ref, ln1b_ref)
    x1_ref[...] = x1


def _mixer_sample(x, cache_k, cache_v, x1_all, table, sinks, w_in_bf, b_in, sgg, sgb, sgw,
                  sgbias, wpa, wpb, wout, ln1g, ln1b):
    nb, ds, _ = x.shape
    first_block = (x1_all.shape[0] - nb * ds) // (SAMPLE_SEQS_PER_STEP * ds)
    win = cache_k.shape[1]
    nseq = SAMPLE_SEQS_PER_STEP
    qrows = Q_PER_KV * ds
    t = np.arange(ds)[:, None]
    bucket_c = jnp.asarray(_t5_bucket_np(t + win - np.arange(win)[None, :]))
    bucket_n = jnp.asarray(_t5_bucket_np(t - (np.arange(win)[None, :] - (win - ds))))
    sink_rows = jnp.repeat(sinks.reshape(N_KV_HEADS, Q_PER_KV), ds, axis=1).reshape(
        N_KV_HEADS, qrows, 1)
    sgw8 = jnp.transpose(sgw[:, :ds, :ds], (0, 2, 1))[..., None]
    sgbias8 = sgbias[:, :ds, :]
    smem = pl.BlockSpec(memory_space=pltpu.SMEM)
    in_specs = [
        smem,
        _const_spec((ds, win)),
        _const_spec((ds, win)),
        _const_spec((N_KV_HEADS, qrows, 1)),
        pl.BlockSpec((nseq, ds, D_MODEL), lambda i: (i, 0, 0)),
        pl.BlockSpec((nseq, win, KV_W), lambda i: (i, 0, 0)),
        pl.BlockSpec((nseq, win, KV_W), lambda i: (i, 0, 0)),
        _const_spec((D_MODEL, IN_W)),
        _const_spec((1, IN_W)),
        _const_spec((1, SG_WIDTH)),
        _const_spec((1, SG_WIDTH)),
        _const_spec((N_SG_GROUPS, ds, ds, 1)),
        _const_spec((N_SG_GROUPS, ds, 1)),
        _const_spec((Q_W, D_MODEL)),
        _const_spec((SG_WIDTH, D_MODEL)),
        _const_spec((D_MODEL, D_MODEL)),
        _const_spec((1, D_MODEL)),
        _const_spec((1, D_MODEL)),
        pl.BlockSpec(memory_space=pl.ANY),
    ]
    out_specs = [
        pl.BlockSpec((nseq * ds, D_MODEL), lambda i: (first_block + i, 0)),
        pl.BlockSpec((nseq, win, KV_W), lambda i: (i, 0, 0)),
        pl.BlockSpec((nseq, win, KV_W), lambda i: (i, 0, 0)),
        pl.BlockSpec((nseq, ds, SG_WIDTH), lambda i: (i, 0, 0)),
    ]
    out_shape = [
        jax.ShapeDtypeStruct(x1_all.shape, F32),
        jax.ShapeDtypeStruct((nb, win, KV_W), F32),
        jax.ShapeDtypeStruct((nb, win, KV_W), F32),
        jax.ShapeDtypeStruct((nb, ds, SG_WIDTH), F32),
    ]
    scratch = [
        pltpu.VMEM((N_KV_HEADS, qrows, win), F32),
        pltpu.VMEM((N_KV_HEADS, qrows, win), F32),
        pltpu.VMEM((nseq, ds, Q_W), F32),
        pltpu.VMEM((nseq, ds, SG_WIDTH), F32),
    ]
    return pl.pallas_call(
        _mixer_sample_kernel,
        grid=(nb // nseq,),
        in_specs=in_specs,
        out_specs=out_specs,
        out_shape=out_shape,
        scratch_shapes=scratch,
        input_output_aliases={len(in_specs) - 1: 0},
        compiler_params=pltpu.CompilerParams(
            dimension_semantics=("arbitrary",),
            vmem_limit_bytes=VMEM_LIMIT_BYTES),
        name="mixer_sample",
    )(table, bucket_c, bucket_n, sink_rows, x, cache_k, cache_v, w_in_bf, b_in, sgg, sgb,
      sgw8, sgbias8, wpa, wpb, wout, ln1g, ln1b, x1_all)


def _route(xb, wrt_ref, rbias_ref):
    n = xb.shape[0]
    logits = _dot_nt(wrt_ref[...].astype(BF16), xb)
    scores = jax.nn.sigmoid(logits)
    sel = scores + rbias_ref[...]
    shape3 = (N_EXPERT_GROUPS, EXPERTS_PER_GROUP, n)
    scores3 = scores.reshape(shape3)
    sel3 = sel.reshape(shape3)
    i_in = lax.broadcasted_iota(jnp.int32, shape3, 1)
    g_id = lax.broadcasted_iota(jnp.int32, shape3, 0)
    e_id = g_id * EXPERTS_PER_GROUP + i_in
    neg = -jnp.inf

    m1 = jnp.max(sel3, axis=1, keepdims=True)
    first = jnp.min(jnp.where(sel3 == m1, i_in, EXPERTS_PER_GROUP), axis=1, keepdims=True)
    m2 = jnp.max(jnp.where(i_in == first, neg, sel3), axis=1, keepdims=True)
    gscore = m1 + m2

    gsel = jnp.zeros(gscore.shape, jnp.bool_)
    gid1 = lax.broadcasted_iota(jnp.int32, gscore.shape, 0)
    for _ in range(TOPK_GROUPS):
        m = jnp.max(gscore, axis=0, keepdims=True)
        pick = jnp.min(jnp.where(gscore == m, gid1, N_EXPERT_GROUPS), axis=0, keepdims=True)
        chosen = gid1 == pick
        gsel = gsel | chosen
        gscore = jnp.where(chosen, neg, gscore)
    val = jnp.where(gsel, sel3, NEG_INF)

    esel = jnp.zeros(shape3, jnp.bool_)
    for _ in range(TOP_K):
        m = jnp.max(jnp.max(val, axis=0, keepdims=True), axis=1, keepdims=True)
        cand = jnp.where(val == m, e_id, N_EXPERTS)
        pick = jnp.min(jnp.min(cand, axis=0, keepdims=True), axis=1, keepdims=True)
        chosen = e_id == pick
        esel = esel | chosen
        val = jnp.where(chosen, neg, val)
    w_sel = jnp.where(esel, scores3, 0.0)
    total = jnp.sum(jnp.sum(w_sel, axis=0, keepdims=True), axis=1, keepdims=True)
    gates = w_sel / total * ROUTED_SCALE
    return gates.reshape(N_EXPERTS, n)


def _swiglu(xb, wg, wu):
    return jax.nn.silu(_dot(xb, wg)) * _dot(xb, wu)


def _bf(mask):
    return jnp.where(mask, 1.0, 0.0).astype(BF16)


def _round_up_run(count):
    return jnp.ceil(count * (1.0 / RUN_ALIGN)) * RUN_ALIGN


def _run_layout_rows(sel_bf):
    n = sel_bf.shape[1]
    ones = jnp.ones((16, n), BF16)
    pc = _round_up_run(_dot_nt(ones, sel_bf))
    lower = lax.broadcasted_iota(jnp.int32, (N_EXPERTS, N_EXPERTS), 0)
    upper = lax.broadcasted_iota(jnp.int32, (N_EXPERTS, N_EXPERTS), 1)
    off = _dot(pc.astype(BF16), _bf(lower < upper))
    return pc, off


def _run_layout_cols(sel_f32):
    cnt = jnp.sum(sel_f32, axis=1, keepdims=True)
    pc = jnp.broadcast_to(_round_up_run(cnt), (N_EXPERTS, 128))
    row = lax.broadcasted_iota(jnp.int32, (N_EXPERTS, N_EXPERTS), 0)
    col = lax.broadcasted_iota(jnp.int32, (N_EXPERTS, N_EXPERTS), 1)
    off = _dot(_bf(col < row), pc.astype(BF16))
    return pc, off


def _rank_in_run(sel_bf):
    n = sel_bf.shape[1]
    m_id = lax.broadcasted_iota(jnp.int32, (n, n), 0)
    n_id = lax.broadcasted_iota(jnp.int32, (n, n), 1)
    return _dot(sel_bf, _bf(m_id < n_id))


def _route_kernel(x_ref, wrt_ref, rbias_ref, gates_ref, pc_ref):
    xb = x_ref[...].astype(BF16)
    gates = _route(xb, wrt_ref, rbias_ref)
    gates_ref[...] = gates
    sel_bf = _bf(gates > 0.0)
    pad = jnp.zeros((128 - N_EXPERTS, sel_bf.shape[1]), BF16)
    ones = jnp.ones((16, sel_bf.shape[1]), BF16)
    cnt = _dot_nt(ones, jnp.concatenate([sel_bf, pad], axis=0))
    pc_ref[...] = _round_up_run(cnt)[:1].astype(jnp.int32)


def _moe_route(x1, wrt, rbias):
    n_tok = x1.shape[0]
    n_tiles = n_tok // MOE_TILE
    return pl.pallas_call(
        _route_kernel,
        grid=(n_tiles,),
        in_specs=[
            pl.BlockSpec((MOE_TILE, D_MODEL), lambda t: (t, 0)),
            _const_spec((N_EXPERTS, D_MODEL)),
            _const_spec((N_EXPERTS, 1)),
        ],
        out_specs=[
            pl.BlockSpec((N_EXPERTS, MOE_TILE), lambda t: (0, t)),
            pl.BlockSpec((None, 1, 128), lambda t: (t, 0, 0)),
        ],
        out_shape=[
            jax.ShapeDtypeStruct((N_EXPERTS, n_tok), F32),
            jax.ShapeDtypeStruct((n_tiles, 1, 128), jnp.int32),
        ],
        compiler_params=pltpu.CompilerParams(dimension_semantics=("arbitrary",)),
        name="moe_route",
    )(x1, wrt, rbias)


def _plan_kernel(pc_ref, gbase_ref, tchunks_ref, estart_ref, enum_ref, gapstart_ref, gapn_ref,
                 meta_ref):
    n_tiles = pc_ref.shape[0]

    def zero_tile(t, c):
        tchunks_ref[t] = 0
        return c

    lax.fori_loop(0, n_tiles, zero_tile, 0)

    def expert_body(e, carry):
        g0, b0 = carry

        def tile_body(t, run):
            n = pc_ref[t, e]
            gbase_ref[t, e] = g0 + run
            tchunks_ref[t] = tchunks_ref[t] + n // RUN_ALIGN
            return run + n

        rows = lax.fori_loop(0, n_tiles, tile_body, 0)
        nb = (rows + EXPERT_BLOCK - 1) // EXPERT_BLOCK
        estart_ref[e] = b0
        enum_ref[e] = nb
        gapstart_ref[e] = g0 + rows
        gapn_ref[e] = (nb * EXPERT_BLOCK - rows) // RUN_ALIGN
        return g0 + nb * EXPERT_BLOCK, b0 + nb

    _, n_blocks = lax.fori_loop(0, N_EXPERTS, expert_body, (0, 0))
    meta_ref[0] = n_blocks


def _max_blocks(n_tiles):
    per_tile = MOE_TILE * TOP_K + N_EXPERTS * (RUN_ALIGN - 1)
    rows = n_tiles * per_tile + N_EXPERTS * (EXPERT_BLOCK - RUN_ALIGN)
    return -(-rows // EXPERT_BLOCK)


def _moe_plan(pc):
    n_tiles = pc.shape[0]
    smem = pl.BlockSpec(memory_space=pltpu.SMEM)
    i32 = jnp.int32
    return pl.pallas_call(
        _plan_kernel,
        in_specs=[smem],
        out_specs=[smem] * 7,
        out_shape=[
            jax.ShapeDtypeStruct((n_tiles, N_EXPERTS), i32),
            jax.ShapeDtypeStruct((n_tiles,), i32),
            jax.ShapeDtypeStruct((N_EXPERTS,), i32),
            jax.ShapeDtypeStruct((N_EXPERTS,), i32),
            jax.ShapeDtypeStruct((N_EXPERTS,), i32),
            jax.ShapeDtypeStruct((N_EXPERTS,), i32),
            jax.ShapeDtypeStruct((1,), i32),
        ],
        name="moe_plan",
    )(pc)


def _aligned(row):
    return row if isinstance(row, int) else pl.multiple_of(row, RUN_ALIGN)


def _rows_copy(src_ref, src_row, dst_ref, dst_row, rows, sem):
    return pltpu.make_async_copy(
        src_ref.at[pl.ds(_aligned(src_row), rows), :],
        dst_ref.at[pl.ds(_aligned(dst_row), rows), :],
        sem)


def _start_run(src_ref, src_row, dst_ref, dst_row, n_rows, sem, src_advances=True):
    def src_at(done):
        return src_row + done if src_advances else src_row

    def big_body(k, c):
        _rows_copy(src_ref, src_at(k * COPY_ROWS), dst_ref, dst_row + k * COPY_ROWS,
                   COPY_ROWS, sem).start()
        return c

    n_big = n_rows // COPY_ROWS
    lax.fori_loop(0, n_big, big_body, 0)
    done = n_big * COPY_ROWS
    mid = n_rows & (2 * RUN_ALIGN)

    @pl.when(mid != 0)
    def _mid():
        _rows_copy(src_ref, src_at(done), dst_ref, dst_row + done, 2 * RUN_ALIGN, sem).start()

    @pl.when((n_rows & RUN_ALIGN) != 0)
    def _small():
        _rows_copy(src_ref, src_at(done + mid), dst_ref, dst_row + done + mid, RUN_ALIGN,
                   sem).start()


def _wait_rows(src_ref, dst_ref, sem, n_rows):
    def big_body(i, c):
        _rows_copy(src_ref, 0, dst_ref, 0, COPY_ROWS, sem).wait()
        return c

    def small_body(i, c):
        _rows_copy(src_ref, 0, dst_ref, 0, RUN_ALIGN, sem).wait()
        return c

    lax.fori_loop(0, n_rows // COPY_ROWS, big_body, 0)
    lax.fori_loop(0, (n_rows % COPY_ROWS) // RUN_ALIGN, small_body, 0)


def _dispatch_kernel(pc_ref, gbase_ref, tchunks_ref, gapstart_ref, gapn_ref,
                     x_ref, gates_ref, xg_ref, xs_scr, zero_scr, sems):
    t = pl.program_id(0)
    n_tiles = pl.num_programs(0)
    slot = t % 2

    @pl.when(t >= 2)
    def _drain_slot():
        _wait_rows(xs_scr.at[slot], xg_ref, sems.at[slot], tchunks_ref[t - 2] * RUN_ALIGN)

    gates = gates_ref[...]
    sel = gates > 0.0
    sel_bf = _bf(sel)
    rankp = jnp.where(sel, _rank_in_run(sel_bf), -1.0).astype(BF16)
    pc_row, off_row = _run_layout_rows(sel_bf)
    pc_col, off_col = _run_layout_cols(jnp.where(sel, 1.0, 0.0))
    s_id = lax.broadcasted_iota(jnp.int32, (TILE_SLOTS, N_EXPERTS), 0).astype(F32)
    owner = _bf((s_id >= off_row[:1]) & (s_id < off_row[:1] + pc_row[:1]))
    rank_of_slot = _dot(owner, rankp)
    off_of_slot = _dot(owner, (off_col * (1.0 / RUN_ALIGN)).astype(BF16)) * RUN_ALIGN
    s_lane = lax.broadcasted_iota(jnp.int32, (TILE_SLOTS, 128), 0).astype(F32)
    onehot = jnp.concatenate(
        [_bf(rank_of_slot[:, c * 128:(c + 1) * 128] + off_of_slot == s_lane)
         for c in range(MOE_TILE // 128)], axis=1)
    xb = x_ref[...].astype(BF16)
    for c in range(TILE_SLOTS // 512):
        xs_scr[slot, c * 512:(c + 1) * 512, :] = _dot(
            onehot[c * 512:(c + 1) * 512, :], xb).astype(BF16)

    def expert_body(e, src):
        n = pc_ref[t, e]
        _start_run(xs_scr.at[slot], src, xg_ref, gbase_ref[t, e], n, sems.at[slot])
        return src + n

    lax.fori_loop(0, N_EXPERTS, expert_body, 0)

    @pl.when(t == n_tiles - 1)
    def _finish():
        zero_scr[...] = jnp.zeros_like(zero_scr)

        def gap_body(e, total):
            n = gapn_ref[e] * RUN_ALIGN
            _start_run(zero_scr, 0, xg_ref, gapstart_ref[e], n, sems.at[2], src_advances=False)
            return total + n

        gap_rows = lax.fori_loop(0, N_EXPERTS, gap_body, 0)
        _wait_rows(zero_scr, xg_ref, sems.at[2], gap_rows)
        _wait_rows(xs_scr.at[slot], xg_ref, sems.at[slot], tchunks_ref[t] * RUN_ALIGN)

        @pl.when(t >= 1)
        def _drain_other():
            _wait_rows(xs_scr.at[1 - slot], xg_ref, sems.at[1 - slot],
                       tchunks_ref[t - 1] * RUN_ALIGN)


def _moe_dispatch(x1, gates, pc, gbase, tchunks, gapstart, gapn, n_rows):
    n_tiles = x1.shape[0] // MOE_TILE
    smem = pl.BlockSpec(memory_space=pltpu.SMEM)
    return pl.pallas_call(
        _dispatch_kernel,
        grid=(n_tiles,),
        in_specs=[
            smem, smem, smem, smem, smem,
            pl.BlockSpec((MOE_TILE, D_MODEL), lambda t: (t, 0)),
            pl.BlockSpec((N_EXPERTS, MOE_TILE), lambda t: (0, t)),
        ],
        out_specs=pl.BlockSpec(memory_space=pl.ANY),
        out_shape=jax.ShapeDtypeStruct((n_rows, D_MODEL), BF16),
        scratch_shapes=[
            pltpu.VMEM((2, TILE_SLOTS, D_MODEL), BF16),
            pltpu.VMEM((COPY_ROWS, D_MODEL), BF16),
            pltpu.SemaphoreType.DMA((3,)),
        ],
        compiler_params=pltpu.CompilerParams(
            dimension_semantics=("arbitrary",), vmem_limit_bytes=VMEM_LIMIT_BYTES),
        name="moe_dispatch",
    )(pc, gbase, tchunks, gapstart, gapn, x1, gates)


def _experts_kernel(estart_ref, enum_ref, meta_ref, wg_ref, wu_ref, wd_ref, xg_ref, yg_ref,
                    x_scr, y_scr, wg_scr, wu_scr, wd_scr, sem_in, sem_out):
    e = pl.program_id(0)
    n_blocks = meta_ref[0]

    def x_copy(g, slot):
        rows = pl.ds(pl.multiple_of(g * EXPERT_BLOCK, EXPERT_BLOCK), EXPERT_BLOCK)
        return pltpu.make_async_copy(xg_ref.at[rows, :], x_scr.at[slot], sem_in.at[slot])

    def y_copy(g, slot):
        rows = pl.ds(pl.multiple_of(g * EXPERT_BLOCK, EXPERT_BLOCK), EXPERT_BLOCK)
        return pltpu.make_async_copy(y_scr.at[slot], yg_ref.at[rows, :], sem_out.at[slot])

    @pl.when((e == 0) & (n_blocks > 0))
    def _first_fetch():
        x_copy(0, 0).start()

    @pl.when(enum_ref[e] > 0)
    def _load_weights():
        wg_scr[...] = wg_ref[...].astype(BF16)
        wu_scr[...] = wu_ref[...].astype(BF16)
        wd_scr[...] = wd_ref[...].astype(BF16)

    def block_body(i, c):
        g = estart_ref[e] + i
        slot = g % 2
        x_copy(g, slot).wait()

        @pl.when(g + 1 < n_blocks)
        def _prefetch():
            x_copy(g + 1, 1 - slot).start()

        @pl.when(g >= 2)
        def _free_out_buffer():
            y_copy(g - 2, slot).wait()

        h = _swiglu(x_scr[slot], wg_scr[...], wu_scr[...])
        y_scr[slot] = _dot(h.astype(BF16), wd_scr[...]).astype(BF16)
        y_copy(g, slot).start()
        return c

    lax.fori_loop(0, enum_ref[e], block_body, 0)

    @pl.when(e == pl.num_programs(0) - 1)
    def _drain():
        for back in (1, 2):
            @pl.when(n_blocks >= back)
            def _wait_out():
                g = n_blocks - back
                y_copy(g, g % 2).wait()


def _moe_experts(estart, enum, meta, xg, wg, wu, wd):
    def w_map(e, estart_ref, enum_ref, meta_ref):
        return (e, 0, 0)

    grid_spec = pltpu.PrefetchScalarGridSpec(
        num_scalar_prefetch=3,
        grid=(N_EXPERTS,),
        in_specs=[
            pl.BlockSpec((None, D_MODEL, EXPERT_DIM), w_map),
            pl.BlockSpec((None, D_MODEL, EXPERT_DIM), w_map),
            pl.BlockSpec((None, EXPERT_DIM, D_MODEL), w_map),
            pl.BlockSpec(memory_space=pl.ANY),
        ],
        out_specs=pl.BlockSpec(memory_space=pl.ANY),
        scratch_shapes=[
            pltpu.VMEM((2, EXPERT_BLOCK, D_MODEL), BF16),
            pltpu.VMEM((2, EXPERT_BLOCK, D_MODEL), BF16),
            pltpu.VMEM((D_MODEL, EXPERT_DIM), BF16),
            pltpu.VMEM((D_MODEL, EXPERT_DIM), BF16),
            pltpu.VMEM((EXPERT_DIM, D_MODEL), BF16),
            pltpu.SemaphoreType.DMA((2,)),
            pltpu.SemaphoreType.DMA((2,)),
        ],
    )
    return pl.pallas_call(
        _experts_kernel,
        grid_spec=grid_spec,
        out_shape=jax.ShapeDtypeStruct(xg.shape, BF16),
        compiler_params=pltpu.CompilerParams(
            dimension_semantics=("arbitrary",), vmem_limit_bytes=VMEM_LIMIT_BYTES),
        name="moe_experts",
    )(estart, enum, meta, wg, wu, wd, xg)


def _combine_kernel(pc_ref, gbase_ref, tchunks_ref,
                    x_ref, gates_ref, yg_ref, wgs_ref, wus_ref, wds_ref, ln2g_ref, ln2b_ref,
                    out_a_ref, out_b_ref, ys_scr, sems, *, tiles_a):
    t = pl.program_id(0)
    n_tiles = pl.num_programs(0)
    slot = t % 2

    def start_gather(tile, dst_slot):
        def expert_body(e, dst):
            n = pc_ref[tile, e]
            _start_run(yg_ref, gbase_ref[tile, e], ys_scr.at[dst_slot], dst, n,
                       sems.at[dst_slot])
            return dst + n

        lax.fori_loop(0, N_EXPERTS, expert_body, 0)

    @pl.when(t == 0)
    def _prime():
        ys_scr[...] = jnp.zeros_like(ys_scr)
        start_gather(0, 0)

    @pl.when(t + 1 < n_tiles)
    def _prefetch():
        start_gather(t + 1, 1 - slot)

    gates = gates_ref[...]
    sel = gates > 0.0
    sel_bf = _bf(sel)
    rankp = jnp.where(sel, _rank_in_run(sel_bf), -1.0)
    pc_row, off_row = _run_layout_rows(sel_bf)
    pc_col, off_col = _run_layout_cols(jnp.where(sel, 1.0, 0.0))
    pad = jnp.zeros((128 - N_EXPERTS, MOE_TILE), F32)
    gates_t = jnp.concatenate([gates, pad], axis=0).T.astype(BF16)
    rankp_t = jnp.concatenate([rankp, pad], axis=0).T.astype(BF16)
    s_id = lax.broadcasted_iota(jnp.int32, (N_EXPERTS, TILE_SLOTS), 1).astype(F32)
    owner = _bf((s_id >= off_col[:, :1]) & (s_id < off_col[:, :1] + pc_col[:, :1]))
    owner = jnp.concatenate([owner, jnp.zeros((128 - N_EXPERTS, TILE_SLOTS), BF16)], axis=0)
    rank_of_slot = _dot(rankp_t, owner)
    gate_of_slot = _dot(gates_t, owner)
    o16 = jnp.concatenate([off_row * (1.0 / RUN_ALIGN), jnp.zeros((16, 128 - N_EXPERTS), F32)],
                          axis=1).astype(BF16)
    off_of_slot = _dot(o16, owner)[:1] * RUN_ALIGN
    s_row = lax.broadcasted_iota(jnp.int32, (1, TILE_SLOTS), 1).astype(F32)
    weights = jnp.where(rank_of_slot + off_of_slot == s_row, gate_of_slot, 0.0).astype(BF16)

    x = x_ref[...]
    xb = x.astype(BF16)
    hs = _swiglu(xb, wgs_ref[...].astype(BF16), wus_ref[...].astype(BF16))
    shared = _dot(hs.astype(BF16), wds_ref[...].astype(BF16))

    _wait_rows(yg_ref, ys_scr.at[slot], sems.at[slot], tchunks_ref[t] * RUN_ALIGN)
    routed = _dot(weights, ys_scr[slot])
    y = _layer_norm(ALPHA * x + (routed + shared), ln2g_ref[...], ln2b_ref[...])

    @pl.when(t < tiles_a)
    def _store_a():
        out_a_ref[...] = y

    @pl.when(t >= tiles_a)
    def _store_b():
        out_b_ref[...] = y


def _moe_combine(x1, gates, yg, pc, gbase, tchunks, wgs, wus, wds, ln2g, ln2b, rows_a):
    n_tok = x1.shape[0]
    tiles_a = rows_a // MOE_TILE
    smem = pl.BlockSpec(memory_space=pltpu.SMEM)
    return pl.pallas_call(
        functools.partial(_combine_kernel, tiles_a=tiles_a),
        grid=(n_tok // MOE_TILE,),
        in_specs=[
            smem, smem, smem,
            pl.BlockSpec((MOE_TILE, D_MODEL), lambda t: (t, 0)),
            pl.BlockSpec((N_EXPERTS, MOE_TILE), lambda t: (0, t)),
            pl.BlockSpec(memory_space=pl.ANY),
            _const_spec((D_MODEL, SHARED_DIM)),
            _const_spec((D_MODEL, SHARED_DIM)),
            _const_spec((SHARED_DIM, D_MODEL)),
            _const_spec((1, D_MODEL)),
            _const_spec((1, D_MODEL)),
        ],
        out_specs=[
            pl.BlockSpec((MOE_TILE, D_MODEL), lambda t: (jnp.minimum(t, tiles_a - 1), 0)),
            pl.BlockSpec((MOE_TILE, D_MODEL), lambda t: (jnp.maximum(t - tiles_a, 0), 0)),
        ],
        out_shape=[
            jax.ShapeDtypeStruct((rows_a, D_MODEL), F32),
            jax.ShapeDtypeStruct((n_tok - rows_a, D_MODEL), F32),
        ],
        scratch_shapes=[
            pltpu.VMEM((2, TILE_SLOTS, D_MODEL), BF16),
            pltpu.SemaphoreType.DMA((2,)),
        ],
        compiler_params=pltpu.CompilerParams(
            dimension_semantics=("arbitrary",), vmem_limit_bytes=VMEM_LIMIT_BYTES),
        name="moe_combine",
    )(pc, gbase, tchunks, x1, gates, yg, wgs, wus, wds, ln2g, ln2b)


def _moe(x1, rows_a, wrt, rbias, wg, wu, wd, wgs, wus, wds, ln2g, ln2b):
    n_tiles = x1.shape[0] // MOE_TILE
    gates, pc3 = _moe_route(x1, wrt, rbias)
    pc = pc3.reshape(n_tiles, 128)
    gbase, tchunks, estart, enum, gapstart, gapn, meta = _moe_plan(pc)
    n_rows = _max_blocks(n_tiles) * EXPERT_BLOCK
    xg = _moe_dispatch(x1, gates, pc, gbase, tchunks, gapstart, gapn, n_rows)
    yg = _moe_experts(estart, enum, meta, xg, wg, wu, wd)
    return _moe_combine(x1, gates, yg, pc, gbase, tchunks, wgs, wus, wds, ln2g, ln2b, rows_a)


def kernel(x_prompt, x_sample, cache_win_k, cache_win_v, rel_bias_table, w_in, b_in, attn_sinks,
           sg_ln_g, sg_ln_b, sg_w, sg_b, w_proj_a, w_proj_b, w_out, ln1_g, ln1_b, w_router,
           router_bias, w_gate_e, w_up_e, w_down_e, w_gate_s, w_up_s, w_down_s, ln2_g, ln2_b):
    assert DEPTH == 1 and w_in.shape[0] == 1
    batch, seq, _ = x_prompt.shape
    nb, ds, _ = x_sample.shape
    win = cache_win_k.shape[2]

    w_in_bf = w_in[0].astype(BF16)
    wpa = w_proj_a[0].astype(BF16)
    wpb = w_proj_b[0].astype(BF16)
    wout = w_out[0].astype(BF16)
    b_in2 = b_in[0].reshape(1, IN_W)
    sgg = sg_ln_g[0].reshape(1, SG_WIDTH)
    sgb = sg_ln_b[0].reshape(1, SG_WIDTH)
    sgw = sg_w[0]
    sgbias = sg_b[0].reshape(N_SG_GROUPS, CHUNK, 1)
    ln1g = ln1_g[0].reshape(1, D_MODEL)
    ln1b = ln1_b[0].reshape(1, D_MODEL)
    sinks = attn_sinks[0]

    x1_p, wk_p, wv_p = _mixer_prompt(
        x_prompt, nb * ds, rel_bias_table, sinks, w_in_bf, b_in2, sgg, sgb, sgw, sgbias, wpa, wpb,
        wout, ln1g, ln1b)
    x1, wk_s, wv_s, cvs = _mixer_sample(
        x_sample, cache_win_k[0].reshape(nb, win, KV_W), cache_win_v[0].reshape(nb, win, KV_W),
        x1_p, rel_bias_table, sinks, w_in_bf, b_in2, sgg, sgb, sgw, sgbias, wpa, wpb, wout,
        ln1g, ln1b)

    y_p, y_s = _moe(x1, batch * seq, wrt=w_router[0].T,
                    rbias=router_bias[0].reshape(N_EXPERTS, 1),
                    wg=w_gate_e[0], wu=w_up_e[0], wd=w_down_e[0],
                    wgs=w_gate_s[0], wus=w_up_s[0], wds=w_down_s[0],
                    ln2g=ln2_g[0].reshape(1, D_MODEL), ln2b=ln2_b[0].reshape(1, D_MODEL))
    y_p = y_p.reshape(batch, seq, D_MODEL)
    y_s = y_s.reshape(nb, ds, D_MODEL)

    kv_shape = (1, -1, WINDOW, N_KV_HEADS, HEAD_DIM)
    return (y_p, y_s,
            wk_p.reshape(kv_shape), wv_p.reshape(kv_shape),
            wk_s.reshape(1, nb, win, N_KV_HEADS, HEAD_DIM),
            wv_s.reshape(1, nb, win, N_KV_HEADS, HEAD_DIM),
            cvs.reshape(1, nb, ds, N_SG_GROUPS, SG_GROUP_DIM))
```

```python
import functools
import math

import jax
import jax.numpy as jnp
import numpy as np
from jax import lax
from jax.experimental import pallas as pl
from jax.experimental.pallas import tpu as pltpu

F32 = jnp.float32
BF16 = jnp.bfloat16

D_MODEL = 1024
DEPTH = 1
HEAD_DIM = 64
N_Q_HEADS = 16
N_KV_HEADS = 2
Q_PER_KV = N_Q_HEADS // N_KV_HEADS
WINDOW = 128
ATTN_SCALE = HEAD_DIM ** -0.5
NEG_INF = -1e30
N_BUCKETS = 32
BUCKET_MAX_EXACT = 16
BUCKET_MAX_DIST = 128
CHUNK = 128
N_SG_GROUPS = 4
SG_GROUP_DIM = 128
SG_WIDTH = N_SG_GROUPS * SG_GROUP_DIM
Q_W = N_Q_HEADS * HEAD_DIM
KV_W = N_KV_HEADS * HEAD_DIM
Q_END = Q_W
K_END = Q_END + KV_W
V_END = K_END + KV_W
U_END = V_END + SG_WIDTH
VS_END = U_END + SG_WIDTH
GA_END = VS_END + D_MODEL
IN_W = GA_END + D_MODEL
N_EXPERTS = 64
TOP_K = 8
N_EXPERT_GROUPS = 8
EXPERTS_PER_GROUP = N_EXPERTS // N_EXPERT_GROUPS
TOPK_GROUPS = 4
EXPERT_DIM = 256
SHARED_DIM = 256
ROUTED_SCALE = 2.5
ALPHA = (2 * DEPTH) ** 0.25
LN_EPS = 1e-5

VMEM_LIMIT_BYTES = 56 * 1024 * 1024

PROMPT_STEP = 512
SAMPLE_SEQS_PER_STEP = 32
MOE_TILE = 256
RUN_ALIGN = 16
COPY_ROWS = 4 * RUN_ALIGN
EXPERT_BLOCK = 1024
TILE_SLOTS = -(-(MOE_TILE * TOP_K + N_EXPERTS * (RUN_ALIGN - 1)) // 512) * 512


def _t5_bucket_np(dist):
    d = np.maximum(dist, 0)
    ratio = np.maximum(d, 1).astype(np.float32) / np.float32(BUCKET_MAX_EXACT)
    large = BUCKET_MAX_EXACT + (
        np.log(ratio) / np.float32(math.log(BUCKET_MAX_DIST / BUCKET_MAX_EXACT))
        * np.float32(N_BUCKETS - BUCKET_MAX_EXACT)).astype(np.int32)
    large = np.minimum(large, N_BUCKETS - 1)
    return np.where(d < BUCKET_MAX_EXACT, d, large).astype(np.int32)


def _layer_norm(x, g, b):
    mu = jnp.mean(x, -1, keepdims=True)
    xc = x - mu
    var = jnp.mean(xc * xc, -1, keepdims=True)
    return xc * lax.rsqrt(var + LN_EPS) * g + b


def _gelu(x):
    return jax.nn.gelu(x)


def _dot(a, b):
    return jnp.dot(a, b, preferred_element_type=F32)


def _dot_nt(a, b):
    return lax.dot_general(a, b, (((1,), (1,)), ((), ())), preferred_element_type=F32)


def _project(xb, w_in_ref, b_in_ref, lo, hi):
    return _dot(xb, w_in_ref[:, lo:hi]) + b_in_ref[:, lo:hi]


def _expand_bias(bucket, table_ref, head):
    acc = jnp.zeros(bucket.shape, F32)
    for b in range(N_BUCKETS):
        acc = jnp.where(bucket == b, table_ref[b, head], acc)
    return acc


def _merge_and_norm(x, a_bf, s_bf, ga, gb, wpa_ref, wpb_ref, wout_ref, g_ref, b_ref):
    pa = _dot(a_bf, wpa_ref[...])
    pb = _dot(s_bf, wpb_ref[...])
    hpre = jax.nn.sigmoid(ga) * pa + jax.nn.sigmoid(gb) * pb
    h = _dot(hpre.astype(BF16), wout_ref[...])
    return _layer_norm(ALPHA * x + h, g_ref[...], b_ref[...])


def _mixer_prompt_kernel(table_ref, bucket_ref, sink_ref, x_ref, w_in_ref, b_in_ref,
                         sgg_ref, sgb_ref, sgw_ref, sgbias_ref, wpa_ref, wpb_ref, wout_ref,
                         ln1g_ref, ln1b_ref,
                         x1_ref, wk_ref, wv_ref,
                         bias_scr, tril_scr, kprev_scr, vprev_scr, a_scr, s_scr):
    b_idx = pl.program_id(0)
    n_idx = pl.program_id(1)
    n_blocks = PROMPT_STEP // WINDOW

    @pl.when((b_idx == 0) & (n_idx == 0))
    def _init_tables():
        bucket = bucket_ref[...]
        for h in range(N_Q_HEADS):
            g, r = divmod(h, Q_PER_KV)
            pair, parity = divmod(r, 2)
            bias_scr[g, pair, :, parity * 2 * WINDOW:(parity + 1) * 2 * WINDOW] = (
                _expand_bias(bucket, table_ref, h))
        row = lax.broadcasted_iota(jnp.int32, (CHUNK, CHUNK), 0)
        col = lax.broadcasted_iota(jnp.int32, (CHUNK, CHUNK), 1)
        for g in range(N_SG_GROUPS):
            tril_scr[g] = jnp.where(row >= col, sgw_ref[g], 0.0).astype(BF16)

    @pl.when(n_idx == 0)
    def _reset_carry():
        kprev_scr[...] = jnp.zeros_like(kprev_scr)
        vprev_scr[...] = jnp.zeros_like(vprev_scr)

    x = x_ref[...]
    xb = x.astype(BF16)
    q_bf = (_project(xb, w_in_ref, b_in_ref, 0, Q_END) * ATTN_SCALE).astype(BF16)
    k = _project(xb, w_in_ref, b_in_ref, Q_END, K_END)
    v = _project(xb, w_in_ref, b_in_ref, K_END, V_END)

    @pl.when(n_idx == pl.num_programs(1) - 1)
    def _emit_window():
        wk_ref[...] = k[PROMPT_STEP - WINDOW:, :]
        wv_ref[...] = v[PROMPT_STEP - WINDOW:, :]

    low = lax.broadcasted_iota(jnp.int32, (PROMPT_STEP, KV_W), 1) < HEAD_DIM

    def lane_halves(t):
        t_sw = pltpu.roll(t, HEAD_DIM, axis=1)
        zero = jnp.zeros_like(t)
        return [[jnp.where(low, t, zero).astype(BF16), jnp.where(low, zero, t_sw).astype(BF16)],
                [jnp.where(low, t_sw, zero).astype(BF16), jnp.where(low, zero, t).astype(BF16)]]

    k_half = lane_halves(k)
    v_half = lane_halves(v)

    row = lax.broadcasted_iota(jnp.int32, (WINDOW, 4 * WINDOW), 0)
    col = lax.broadcasted_iota(jnp.int32, (WINDOW, 4 * WINDOW), 1) % (2 * WINDOW)
    dist = row + WINDOW - col
    band_valid = (dist >= 0) & (dist <= WINDOW)
    first_valid = band_valid & ((col >= WINDOW) | (n_idx > 0))
    lane_low = lax.broadcasted_iota(jnp.int32, (Q_PER_KV // 2, WINDOW, 2 * HEAD_DIM), 2) < HEAD_DIM
    ones_rows = lax.broadcasted_iota(jnp.int32, (4 * WINDOW, 2 * HEAD_DIM), 0) < 2 * WINDOW
    ones_cols = lax.broadcasted_iota(jnp.int32, (4 * WINDOW, 2 * HEAD_DIM), 1) < HEAD_DIM
    sum_block = _bf(ones_rows == ones_cols)

    for j in range(n_blocks):
        r0, r1 = j * WINDOW, (j + 1) * WINDOW
        valid = first_valid if j == 0 else band_valid
        for g in range(N_KV_HEADS):
            def band(cur, prev_scr):
                parts = []
                for s in range(2):
                    prev = prev_scr[2 * g + s] if j == 0 else cur[g][s][r0 - WINDOW:r0]
                    parts += [prev, cur[g][s][r0:r1]]
                return jnp.concatenate(parts, axis=0)

            kd = band(k_half, kprev_scr)
            vd = jnp.concatenate([band(v_half, vprev_scr), sum_block], axis=1)
            q0 = g * Q_PER_KV * HEAD_DIM
            qp = jnp.concatenate(
                [q_bf[r0:r1, q0 + pr * 2 * HEAD_DIM:q0 + (pr + 1) * 2 * HEAD_DIM]
                 for pr in range(Q_PER_KV // 2)], axis=0)
            logits = _dot_nt(qp, kd).reshape(Q_PER_KV // 2, WINDOW, 4 * WINDOW)
            logits = jnp.where(valid[None], logits + bias_scr[g], NEG_INF)
            probs, sink_terms = [], []
            for s in range(2):
                l_s = logits[:, :, s * 2 * WINDOW:(s + 1) * 2 * WINDOW]
                sink = sink_ref[s, g]
                m = jnp.maximum(jnp.max(l_s, -1, keepdims=True), sink)
                probs.append(jnp.exp(l_s - m))
                sink_terms.append(jnp.broadcast_to(jnp.exp(sink - m), lane_low.shape))
            p = jnp.concatenate(probs, axis=-1).reshape(4 * WINDOW, 4 * WINDOW).astype(BF16)
            out = _dot(p, vd).reshape(Q_PER_KV // 2, WINDOW, 4 * HEAD_DIM)
            den = out[:, :, 2 * HEAD_DIM:] + jnp.where(lane_low, sink_terms[0], sink_terms[1])
            o = out[:, :, :2 * HEAD_DIM] / den
            for pr in range(Q_PER_KV // 2):
                a_scr[r0:r1, q0 + pr * 2 * HEAD_DIM:q0 + (pr + 1) * 2 * HEAD_DIM] = (
                    o[pr].astype(BF16))

    for g in range(N_KV_HEADS):
        for s in range(2):
            kprev_scr[2 * g + s] = k_half[g][s][PROMPT_STEP - WINDOW:]
            vprev_scr[2 * g + s] = v_half[g][s][PROMPT_STEP - WINDOW:]

    u = _gelu(_project(xb, w_in_ref, b_in_ref, V_END, U_END))
    vs = _gelu(_project(xb, w_in_ref, b_in_ref, U_END, VS_END))
    vs_bf = _layer_norm(vs, sgg_ref[...], sgb_ref[...]).astype(BF16)
    for j in range(n_blocks):
        r0, r1 = j * WINDOW, (j + 1) * WINDOW
        for g in range(N_SG_GROUPS):
            c0, c1 = g * SG_GROUP_DIM, (g + 1) * SG_GROUP_DIM
            sg = _dot(tril_scr[g], vs_bf[r0:r1, c0:c1]) + sgbias_ref[g]
            s_scr[r0:r1, c0:c1] = (u[r0:r1, c0:c1] * sg).astype(BF16)

    ga = _project(xb, w_in_ref, b_in_ref, VS_END, GA_END)
    gb = _project(xb, w_in_ref, b_in_ref, GA_END, IN_W)
    x1_ref[...] = _merge_and_norm(x, a_scr[...], s_scr[...], ga, gb, wpa_ref, wpb_ref,
                                  wout_ref, ln1g_ref, ln1b_ref)


def _const_spec(shape):
    zeros = (0,) * len(shape)
    return pl.BlockSpec(shape, lambda *_: zeros, pipeline_mode=pl.Buffered(1))


def _mixer_prompt(x, extra_rows, table, sinks, w_in_bf, b_in, sgg, sgb, sgw, sgbias, wpa, wpb,
                  wout, ln1g, ln1b):
    batch, seq, _ = x.shape
    n_steps = seq // PROMPT_STEP
    dist = np.arange(WINDOW)[:, None] + WINDOW - np.arange(2 * WINDOW)[None, :]
    bucket = jnp.asarray(_t5_bucket_np(dist))
    sink_pairs = jnp.transpose(sinks.reshape(N_KV_HEADS, Q_PER_KV // 2, 2), (2, 0, 1)).reshape(
        2, N_KV_HEADS, Q_PER_KV // 2, 1, 1)
    smem = pl.BlockSpec(memory_space=pltpu.SMEM)
    in_specs = [
        smem,
        _const_spec((WINDOW, 2 * WINDOW)),
        _const_spec((2, N_KV_HEADS, Q_PER_KV // 2, 1, 1)),
        pl.BlockSpec((None, PROMPT_STEP, D_MODEL), lambda b, n: (b, n, 0)),
        _const_spec((D_MODEL, IN_W)),
        _const_spec((1, IN_W)),
        _const_spec((1, SG_WIDTH)),
        _const_spec((1, SG_WIDTH)),
        _const_spec((N_SG_GROUPS, CHUNK, CHUNK)),
        _const_spec((N_SG_GROUPS, CHUNK, 1)),
        _const_spec((Q_W, D_MODEL)),
        _const_spec((SG_WIDTH, D_MODEL)),
        _const_spec((D_MODEL, D_MODEL)),
        _const_spec((1, D_MODEL)),
        _const_spec((1, D_MODEL)),
    ]
    out_specs = [
        pl.BlockSpec((PROMPT_STEP, D_MODEL), lambda b, n: (b * n_steps + n, 0)),
        pl.BlockSpec((None, WINDOW, KV_W), lambda b, n: (b, 0, 0)),
        pl.BlockSpec((None, WINDOW, KV_W), lambda b, n: (b, 0, 0)),
    ]
    out_shape = [
        jax.ShapeDtypeStruct((batch * seq + extra_rows, D_MODEL), F32),
        jax.ShapeDtypeStruct((batch, WINDOW, KV_W), F32),
        jax.ShapeDtypeStruct((batch, WINDOW, KV_W), F32),
    ]
    scratch = [
        pltpu.VMEM((N_KV_HEADS, Q_PER_KV // 2, WINDOW, 4 * WINDOW), F32),
        pltpu.VMEM((N_SG_GROUPS, CHUNK, CHUNK), BF16),
        pltpu.VMEM((2 * N_KV_HEADS, WINDOW, KV_W), BF16),
        pltpu.VMEM((2 * N_KV_HEADS, WINDOW, KV_W), BF16),
        pltpu.VMEM((PROMPT_STEP, Q_W), BF16),
        pltpu.VMEM((PROMPT_STEP, SG_WIDTH), BF16),
    ]
    return pl.pallas_call(
        _mixer_prompt_kernel,
        grid=(batch, n_steps),
        in_specs=in_specs,
        out_specs=out_specs,
        out_shape=out_shape,
        scratch_shapes=scratch,
        compiler_params=pltpu.CompilerParams(
            dimension_semantics=("arbitrary", "arbitrary"),
            vmem_limit_bytes=VMEM_LIMIT_BYTES),
        name="mixer_prompt",
    )(table, bucket, sink_pairs, x, w_in_bf, b_in, sgg, sgb, sgw, sgbias, wpa, wpb, wout,
      ln1g, ln1b)


def _mixer_sample_kernel(table_ref, bucket_c_ref, bucket_n_ref, sink_ref, x_ref, ck_ref, cv_ref,
                         w_in_ref, b_in_ref, sgg_ref, sgb_ref, sgw8_ref, sgbias8_ref,
                         wpa_ref, wpb_ref, wout_ref, ln1g_ref, ln1b_ref, x1_in_ref,
                         x1_ref, wk_ref, wv_ref, cvs_ref,
                         bias_c_scr, bias_n_scr, a_scr, s_scr):
    del x1_in_ref
    nseq, ds = SAMPLE_SEQS_PER_STEP, x_ref.shape[1]
    rows = nseq * ds
    win = ck_ref.shape[1]
    qrows = Q_PER_KV * ds

    @pl.when(pl.program_id(0) == 0)
    def _init_tables():
        bc = bucket_c_ref[...]
        bn = bucket_n_ref[...]
        for h in range(N_Q_HEADS):
            g, r = divmod(h, Q_PER_KV)
            bias_c_scr[g, r * ds:(r + 1) * ds, :] = _expand_bias(bc, table_ref, h)
            bias_n_scr[g, r * ds:(r + 1) * ds, :] = _expand_bias(bn, table_ref, h)

    x = x_ref[...].reshape(rows, D_MODEL)
    xb = x.astype(BF16)
    q = _project(xb, w_in_ref, b_in_ref, 0, Q_END) * ATTN_SCALE
    k = _project(xb, w_in_ref, b_in_ref, Q_END, K_END)
    v = _project(xb, w_in_ref, b_in_ref, K_END, V_END)
    ck = ck_ref[...]
    cv = cv_ref[...]
    wk_ref[:, :win - ds, :] = ck[:, ds:, :]
    wk_ref[:, win - ds:, :] = k.reshape(nseq, ds, KV_W)
    wv_ref[:, :win - ds, :] = cv[:, ds:, :]
    wv_ref[:, win - ds:, :] = v.reshape(nseq, ds, KV_W)
    nk = wk_ref[...]
    nv = wv_ref[...]

    q3 = q.reshape(nseq, ds, Q_W)
    t_q = lax.broadcasted_iota(jnp.int32, (qrows, win), 0) % ds
    col = lax.broadcasted_iota(jnp.int32, (qrows, win), 1)
    dist_c = t_q + win - col
    valid_c = (dist_c >= 0) & (dist_c <= WINDOW)
    dist_n = t_q - (col - (win - ds))
    valid_n = (col >= win - ds) & (dist_n >= 0) & (dist_n <= WINDOW)

    for g in range(N_KV_HEADS):
        c0, c1 = g * HEAD_DIM, (g + 1) * HEAD_DIM
        h0 = g * Q_PER_KV
        qs = jnp.concatenate(
            [q3[:, :, (h0 + r) * HEAD_DIM:(h0 + r + 1) * HEAD_DIM] for r in range(Q_PER_KV)],
            axis=1).astype(BF16)
        kc = ck[:, :, c0:c1].astype(BF16)
        vc = cv[:, :, c0:c1].astype(BF16)
        kn = nk[:, :, c0:c1].astype(BF16)
        vn = nv[:, :, c0:c1].astype(BF16)
        lc = jnp.einsum('bqd,bkd->bqk', qs, kc, preferred_element_type=F32)
        ln = jnp.einsum('bqd,bkd->bqk', qs, kn, preferred_element_type=F32)
        lc = jnp.where(valid_c[None], lc + bias_c_scr[g][None], NEG_INF)
        ln = jnp.where(valid_n[None], ln + bias_n_scr[g][None], NEG_INF)
        sink = sink_ref[g]
        m = jnp.maximum(jnp.maximum(jnp.max(lc, -1, keepdims=True),
                                    jnp.max(ln, -1, keepdims=True)), sink[None])
        pc = jnp.exp(lc - m)
        pn = jnp.exp(ln - m)
        den = (jnp.sum(pc, -1, keepdims=True) + jnp.sum(pn, -1, keepdims=True)
               + jnp.exp(sink[None] - m))
        o = (jnp.einsum('bqk,bkd->bqd', pc.astype(BF16), vc, preferred_element_type=F32)
             + jnp.einsum('bqk,bkd->bqd', pn.astype(BF16), vn, preferred_element_type=F32))
        o = o / den
        for r in range(Q_PER_KV):
            a_scr[:, :, (h0 + r) * HEAD_DIM:(h0 + r + 1) * HEAD_DIM] = (
                o[:, r * ds:(r + 1) * ds, :])

    u = _gelu(_project(xb, w_in_ref, b_in_ref, V_END, U_END))
    vs = _gelu(_project(xb, w_in_ref, b_in_ref, U_END, VS_END))
    vs_ln = _layer_norm(vs, sgg_ref[...], sgb_ref[...])
    cvs_ref[...] = vs_ln.reshape(nseq, ds, SG_WIDTH)
    vq = vs_ln.astype(BF16).astype(F32).reshape(nseq, ds, SG_WIDTH)
    u3 = u.reshape(nseq, ds, SG_WIDTH)
    i_row = lax.broadcasted_iota(jnp.int32, (ds, 1), 0)
    for g in range(N_SG_GROUPS):
        c0, c1 = g * SG_GROUP_DIM, (g + 1) * SG_GROUP_DIM
        acc = jnp.broadcast_to(sgbias8_ref[g][None], (nseq, ds, SG_GROUP_DIM))
        for j in range(ds):
            w_col = jnp.where(i_row >= j, sgw8_ref[g, j], 0.0)
            w_col = w_col.astype(BF16).astype(F32)
            acc = acc + w_col[None] * vq[:, j:j + 1, c0:c1]
        s_scr[:, :, c0:c1] = u3[:, :, c0:c1] * acc

    ga = _project(xb, w_in_ref, b_in_ref, VS_END, GA_END)
    gb = _project(xb, w_in_ref, b_in_ref, GA_END, IN_W)
    x1 = _merge_and_norm(x, a_scr[...].reshape(rows, Q_W).astype(BF16),
                         s_scr[...].reshape(rows, SG_WIDTH).astype(BF16),
                         ga, gb, wpa_ref, wpb_ref, wout_ref, ln1g_ref, ln1b_ref)
    x1_ref[...] = x1


def _mixer_sample(x, cache_k, cache_v, x1_all, table, sinks, w_in_bf, b_in, sgg, sgb, sgw,
                  sgbias, wpa, wpb, wout, ln1g, ln1b):
    nb, ds, _ = x.shape
    first_block = (x1_all.shape[0] - nb * ds) // (SAMPLE_SEQS_PER_STEP * ds)
    win = cache_k.shape[1]
    nseq = SAMPLE_SEQS_PER_STEP
    qrows = Q_PER_KV * ds
    t = np.arange(ds)[:, None]
    bucket_c = jnp.asarray(_t5_bucket_np(t + win - np.arange(win)[None, :]))
    bucket_n = jnp.asarray(_t5_bucket_np(t - (np.arange(win)[None, :] - (win - ds))))
    sink_rows = jnp.repeat(sinks.reshape(N_KV_HEADS, Q_PER_KV), ds, axis=1).reshape(
        N_KV_HEADS, qrows, 1)
    sgw8 = jnp.transpose(sgw[:, :ds, :ds], (0, 2, 1))[..., None]
    sgbias8 = sgbias[:, :ds, :]
    smem = pl.BlockSpec(memory_space=pltpu.SMEM)
    in_specs = [
        smem,
        _const_spec((ds, win)),
        _const_spec((ds, win)),
        _const_spec((N_KV_HEADS, qrows, 1)),
        pl.BlockSpec((nseq, ds, D_MODEL), lambda i: (i, 0, 0)),
        pl.BlockSpec((nseq, win, KV_W), lambda i: (i, 0, 0)),
        pl.BlockSpec((nseq, win, KV_W), lambda i: (i, 0, 0)),
        _const_spec((D_MODEL, IN_W)),
        _const_spec((1, IN_W)),
        _const_spec((1, SG_WIDTH)),
        _const_spec((1, SG_WIDTH)),
        _const_spec((N_SG_GROUPS, ds, ds, 1)),
        _const_spec((N_SG_GROUPS, ds, 1)),
        _const_spec((Q_W, D_MODEL)),
        _const_spec((SG_WIDTH, D_MODEL)),
        _const_spec((D_MODEL, D_MODEL)),
        _const_spec((1, D_MODEL)),
        _const_spec((1, D_MODEL)),
        pl.BlockSpec(memory_space=pl.ANY),
    ]
    out_specs = [
        pl.BlockSpec((nseq * ds, D_MODEL), lambda i: (first_block + i, 0)),
        pl.BlockSpec((nseq, win, KV_W), lambda i: (i, 0, 0)),
        pl.BlockSpec((nseq, win, KV_W), lambda i: (i, 0, 0)),
        pl.BlockSpec((nseq, ds, SG_WIDTH), lambda i: (i, 0, 0)),
    ]
    out_shape = [
        jax.ShapeDtypeStruct(x1_all.shape, F32),
        jax.ShapeDtypeStruct((nb, win, KV_W), F32),
        jax.ShapeDtypeStruct((nb, win, KV_W), F32),
        jax.ShapeDtypeStruct((nb, ds, SG_WIDTH), F32),
    ]
    scratch = [
        pltpu.VMEM((N_KV_HEADS, qrows, win), F32),
        pltpu.VMEM((N_KV_HEADS, qrows, win), F32),
        pltpu.VMEM((nseq, ds, Q_W), F32),
        pltpu.VMEM((nseq, ds, SG_WIDTH), F32),
    ]
    return pl.pallas_call(
        _mixer_sample_kernel,
        grid=(nb // nseq,),
        in_specs=in_specs,
        out_specs=out_specs,
        out_shape=out_shape,
        scratch_shapes=scratch,
        input_output_aliases={len(in_specs) - 1: 0},
        compiler_params=pltpu.CompilerParams(
            dimension_semantics=("arbitrary",),
            vmem_limit_bytes=VMEM_LIMIT_BYTES),
        name="mixer_sample",
    )(table, bucket_c, bucket_n, sink_rows, x, cache_k, cache_v, w_in_bf, b_in, sgg, sgb,
      sgw8, sgbias8, wpa, wpb, wout, ln1g, ln1b, x1_all)


def _route(xb, wrt_ref, rbias_ref):
    n = xb.shape[0]
    logits = _dot_nt(wrt_ref[...].astype(BF16), xb)
    scores = jax.nn.sigmoid(logits)
    sel = scores + rbias_ref[...]
    shape3 = (N_EXPERT_GROUPS, EXPERTS_PER_GROUP, n)
    scores3 = scores.reshape(shape3)
    sel3 = sel.reshape(shape3)
    i_in = lax.broadcasted_iota(jnp.int32, shape3, 1)
    g_id = lax.broadcasted_iota(jnp.int32, shape3, 0)
    e_id = g_id * EXPERTS_PER_GROUP + i_in
    neg = -jnp.inf

    m1 = jnp.max(sel3, axis=1, keepdims=True)
    first = jnp.min(jnp.where(sel3 == m1, i_in, EXPERTS_PER_GROUP), axis=1, keepdims=True)
    m2 = jnp.max(jnp.where(i_in == first, neg, sel3), axis=1, keepdims=True)
    gscore = m1 + m2

    gsel = jnp.zeros(gscore.shape, jnp.bool_)
    gid1 = lax.broadcasted_iota(jnp.int32, gscore.shape, 0)
    for _ in range(TOPK_GROUPS):
        m = jnp.max(gscore, axis=0, keepdims=True)
        pick = jnp.min(jnp.where(gscore == m, gid1, N_EXPERT_GROUPS), axis=0, keepdims=True)
        chosen = gid1 == pick
        gsel = gsel | chosen
        gscore = jnp.where(chosen, neg, gscore)
    val = jnp.where(gsel, sel3, NEG_INF)

    esel = jnp.zeros(shape3, jnp.bool_)
    for _ in range(TOP_K):
        m = jnp.max(jnp.max(val, axis=0, keepdims=True), axis=1, keepdims=True)
        cand = jnp.where(val == m, e_id, N_EXPERTS)
        pick = jnp.min(jnp.min(cand, axis=0, keepdims=True), axis=1, keepdims=True)
        chosen = e_id == pick
        esel = esel | chosen
        val = jnp.where(chosen, neg, val)
    w_sel = jnp.where(esel, scores3, 0.0)
    total = jnp.sum(jnp.sum(w_sel, axis=0, keepdims=True), axis=1, keepdims=True)
    gates = w_sel / total * ROUTED_SCALE
    return gates.reshape(N_EXPERTS, n)


def _swiglu(xb, wg, wu):
    return jax.nn.silu(_dot(xb, wg)) * _dot(xb, wu)


def _bf(mask):
    return jnp.where(mask, 1.0, 0.0).astype(BF16)


def _round_up_run(count):
    return jnp.ceil(count * (1.0 / RUN_ALIGN)) * RUN_ALIGN


def _run_layout_rows(sel_bf):
    n = sel_bf.shape[1]
    ones = jnp.ones((16, n), BF16)
    pc = _round_up_run(_dot_nt(ones, sel_bf))
    lower = lax.broadcasted_iota(jnp.int32, (N_EXPERTS, N_EXPERTS), 0)
    upper = lax.broadcasted_iota(jnp.int32, (N_EXPERTS, N_EXPERTS), 1)
    off = _dot(pc.astype(BF16), _bf(lower < upper))
    return pc, off


def _run_layout_cols(sel_f32):
    cnt = jnp.sum(sel_f32, axis=1, keepdims=True)
    pc = jnp.broadcast_to(_round_up_run(cnt), (N_EXPERTS, 128))
    row = lax.broadcasted_iota(jnp.int32, (N_EXPERTS, N_EXPERTS), 0)
    col = lax.broadcasted_iota(jnp.int32, (N_EXPERTS, N_EXPERTS), 1)
    off = _dot(_bf(col < row), pc.astype(BF16))
    return pc, off


def _rank_in_run(sel_bf):
    n = sel_bf.shape[1]
    m_id = lax.broadcasted_iota(jnp.int32, (n, n), 0)
    n_id = lax.broadcasted_iota(jnp.int32, (n, n), 1)
    return _dot(sel_bf, _bf(m_id < n_id))


def _route_kernel(x_ref, wrt_ref, rbias_ref, gates_ref, pc_ref):
    xb = x_ref[...].astype(BF16)
    gates = _route(xb, wrt_ref, rbias_ref)
    gates_ref[...] = gates
    sel_bf = _bf(gates > 0.0)
    pad = jnp.zeros((128 - N_EXPERTS, sel_bf.shape[1]), BF16)
    ones = jnp.ones((16, sel_bf.shape[1]), BF16)
    cnt = _dot_nt(ones, jnp.concatenate([sel_bf, pad], axis=0))
    pc_ref[...] = _round_up_run(cnt)[:1].astype(jnp.int32)


def _moe_route(x1, wrt, rbias):
    n_tok = x1.shape[0]
    n_tiles = n_tok // MOE_TILE
    return pl.pallas_call(
        _route_kernel,
        grid=(n_tiles,),
        in_specs=[
            pl.BlockSpec((MOE_TILE, D_MODEL), lambda t: (t, 0)),
            _const_spec((N_EXPERTS, D_MODEL)),
            _const_spec((N_EXPERTS, 1)),
        ],
        out_specs=[
            pl.BlockSpec((N_EXPERTS, MOE_TILE), lambda t: (0, t)),
            pl.BlockSpec((None, 1, 128), lambda t: (t, 0, 0)),
        ],
        out_shape=[
            jax.ShapeDtypeStruct((N_EXPERTS, n_tok), F32),
            jax.ShapeDtypeStruct((n_tiles, 1, 128), jnp.int32),
        ],
        compiler_params=pltpu.CompilerParams(dimension_semantics=("arbitrary",)),
        name="moe_route",
    )(x1, wrt, rbias)


def _plan_kernel(pc_ref, gbase_ref, tchunks_ref, estart_ref, enum_ref, gapstart_ref, gapn_ref,
                 meta_ref):
    n_tiles = pc_ref.shape[0]

    def zero_tile(t, c):
        tchunks_ref[t] = 0
        return c

    lax.fori_loop(0, n_tiles, zero_tile, 0)

    def expert_body(e, carry):
        g0, b0 = carry

        def tile_body(t, run):
            n = pc_ref[t, e]
            gbase_ref[t, e] = g0 + run
            tchunks_ref[t] = tchunks_ref[t] + n // RUN_ALIGN
            return run + n

        rows = lax.fori_loop(0, n_tiles, tile_body, 0)
        nb = (rows + EXPERT_BLOCK - 1) // EXPERT_BLOCK
        estart_ref[e] = b0
        enum_ref[e] = nb
        gapstart_ref[e] = g0 + rows
        gapn_ref[e] = (nb * EXPERT_BLOCK - rows) // RUN_ALIGN
        return g0 + nb * EXPERT_BLOCK, b0 + nb

    _, n_blocks = lax.fori_loop(0, N_EXPERTS, expert_body, (0, 0))
    meta_ref[0] = n_blocks


def _max_blocks(n_tiles):
    per_tile = MOE_TILE * TOP_K + N_EXPERTS * (RUN_ALIGN - 1)
    rows = n_tiles * per_tile + N_EXPERTS * (EXPERT_BLOCK - RUN_ALIGN)
    return -(-rows // EXPERT_BLOCK)


def _moe_plan(pc):
    n_tiles = pc.shape[0]
    smem = pl.BlockSpec(memory_space=pltpu.SMEM)
    i32 = jnp.int32
    return pl.pallas_call(
        _plan_kernel,
        in_specs=[smem],
        out_specs=[smem] * 7,
        out_shape=[
            jax.ShapeDtypeStruct((n_tiles, N_EXPERTS), i32),
            jax.ShapeDtypeStruct((n_tiles,), i32),
            jax.ShapeDtypeStruct((N_EXPERTS,), i32),
            jax.ShapeDtypeStruct((N_EXPERTS,), i32),
            jax.ShapeDtypeStruct((N_EXPERTS,), i32),
            jax.ShapeDtypeStruct((N_EXPERTS,), i32),
            jax.ShapeDtypeStruct((1,), i32),
        ],
        name="moe_plan",
    )(pc)


def _aligned(row):
    return row if isinstance(row, int) else pl.multiple_of(row, RUN_ALIGN)


def _rows_copy(src_ref, src_row, dst_ref, dst_row, rows, sem):
    return pltpu.make_async_copy(
        src_ref.at[pl.ds(_aligned(src_row), rows), :],
        dst_ref.at[pl.ds(_aligned(dst_row), rows), :],
        sem)


def _start_run(src_ref, src_row, dst_ref, dst_row, n_rows, sem, src_advances=True):
    def src_at(done):
        return src_row + done if src_advances else src_row

    def big_body(k, c):
        _rows_copy(src_ref, src_at(k * COPY_ROWS), dst_ref, dst_row + k * COPY_ROWS,
                   COPY_ROWS, sem).start()
        return c

    n_big = n_rows // COPY_ROWS
    lax.fori_loop(0, n_big, big_body, 0)
    done = n_big * COPY_ROWS
    mid = n_rows & (2 * RUN_ALIGN)

    @pl.when(mid != 0)
    def _mid():
        _rows_copy(src_ref, src_at(done), dst_ref, dst_row + done, 2 * RUN_ALIGN, sem).start()

    @pl.when((n_rows & RUN_ALIGN) != 0)
    def _small():
        _rows_copy(src_ref, src_at(done + mid), dst_ref, dst_row + done + mid, RUN_ALIGN,
                   sem).start()


def _wait_rows(src_ref, dst_ref, sem, n_rows):
    def big_body(i, c):
        _rows_copy(src_ref, 0, dst_ref, 0, COPY_ROWS, sem).wait()
        return c

    def small_body(i, c):
        _rows_copy(src_ref, 0, dst_ref, 0, RUN_ALIGN, sem).wait()
        return c

    lax.fori_loop(0, n_rows // COPY_ROWS, big_body, 0)
    lax.fori_loop(0, (n_rows % COPY_ROWS) // RUN_ALIGN, small_body, 0)


def _dispatch_kernel(pc_ref, gbase_ref, tchunks_ref, gapstart_ref, gapn_ref,
                     x_ref, gates_ref, xg_ref, xs_scr, zero_scr, sems):
    t = pl.program_id(0)
    n_tiles = pl.num_programs(0)
    slot = t % 2

    @pl.when(t >= 2)
    def _drain_slot():
        _wait_rows(xs_scr.at[slot], xg_ref, sems.at[slot], tchunks_ref[t - 2] * RUN_ALIGN)

    gates = gates_ref[...]
    sel = gates > 0.0
    sel_bf = _bf(sel)
    rankp = jnp.where(sel, _rank_in_run(sel_bf), -1.0).astype(BF16)
    pc_row, off_row = _run_layout_rows(sel_bf)
    pc_col, off_col = _run_layout_cols(jnp.where(sel, 1.0, 0.0))
    s_id = lax.broadcasted_iota(jnp.int32, (TILE_SLOTS, N_EXPERTS), 0).astype(F32)
    owner = _bf((s_id >= off_row[:1]) & (s_id < off_row[:1] + pc_row[:1]))
    rank_of_slot = _dot(owner, rankp)
    off_of_slot = _dot(owner, (off_col * (1.0 / RUN_ALIGN)).astype(BF16)) * RUN_ALIGN
    s_lane = lax.broadcasted_iota(jnp.int32, (TILE_SLOTS, 128), 0).astype(F32)
    onehot = jnp.concatenate(
        [_bf(rank_of_slot[:, c * 128:(c + 1) * 128] + off_of_slot == s_lane)
         for c in range(MOE_TILE // 128)], axis=1)
    xb = x_ref[...].astype(BF16)
    for c in range(TILE_SLOTS // 512):
        xs_scr[slot, c * 512:(c + 1) * 512, :] = _dot(
            onehot[c * 512:(c + 1) * 512, :], xb).astype(BF16)

    def expert_body(e, src):
        n = pc_ref[t, e]
        _start_run(xs_scr.at[slot], src, xg_ref, gbase_ref[t, e], n, sems.at[slot])
        return src + n

    lax.fori_loop(0, N_EXPERTS, expert_body, 0)

    @pl.when(t == n_tiles - 1)
    def _finish():
        zero_scr[...] = jnp.zeros_like(zero_scr)

        def gap_body(e, total):
            n = gapn_ref[e] * RUN_ALIGN
            _start_run(zero_scr, 0, xg_ref, gapstart_ref[e], n, sems.at[2], src_advances=False)
            return total + n

        gap_rows = lax.fori_loop(0, N_EXPERTS, gap_body, 0)
        _wait_rows(zero_scr, xg_ref, sems.at[2], gap_rows)
        _wait_rows(xs_scr.at[slot], xg_ref, sems.at[slot], tchunks_ref[t] * RUN_ALIGN)

        @pl.when(t >= 1)
        def _drain_other():
            _wait_rows(xs_scr.at[1 - slot], xg_ref, sems.at[1 - slot],
                       tchunks_ref[t - 1] * RUN_ALIGN)


def _moe_dispatch(x1, gates, pc, gbase, tchunks, gapstart, gapn, n_rows):
    n_tiles = x1.shape[0] // MOE_TILE
    smem = pl.BlockSpec(memory_space=pltpu.SMEM)
    return pl.pallas_call(
        _dispatch_kernel,
        grid=(n_tiles,),
        in_specs=[
            smem, smem, smem, smem, smem,
            pl.BlockSpec((MOE_TILE, D_MODEL), lambda t: (t, 0)),
            pl.BlockSpec((N_EXPERTS, MOE_TILE), lambda t: (0, t)),
        ],
        out_specs=pl.BlockSpec(memory_space=pl.ANY),
        out_shape=jax.ShapeDtypeStruct((n_rows, D_MODEL), BF16),
        scratch_shapes=[
            pltpu.VMEM((2, TILE_SLOTS, D_MODEL), BF16),
            pltpu.VMEM((COPY_ROWS, D_MODEL), BF16),
            pltpu.SemaphoreType.DMA((3,)),
        ],
        compiler_params=pltpu.CompilerParams(
            dimension_semantics=("arbitrary",), vmem_limit_bytes=VMEM_LIMIT_BYTES),
        name="moe_dispatch",
    )(pc, gbase, tchunks, gapstart, gapn, x1, gates)


def _experts_kernel(estart_ref, enum_ref, meta_ref, wg_ref, wu_ref, wd_ref, xg_ref, yg_ref,
                    x_scr, y_scr, wg_scr, wu_scr, wd_scr, sem_in, sem_out):
    e = pl.program_id(0)
    n_blocks = meta_ref[0]

    def x_copy(g, slot):
        rows = pl.ds(pl.multiple_of(g * EXPERT_BLOCK, EXPERT_BLOCK), EXPERT_BLOCK)
        return pltpu.make_async_copy(xg_ref.at[rows, :], x_scr.at[slot], sem_in.at[slot])

    def y_copy(g, slot):
        rows = pl.ds(pl.multiple_of(g * EXPERT_BLOCK, EXPERT_BLOCK), EXPERT_BLOCK)
        return pltpu.make_async_copy(y_scr.at[slot], yg_ref.at[rows, :], sem_out.at[slot])

    @pl.when((e == 0) & (n_blocks > 0))
    def _first_fetch():
        x_copy(0, 0).start()

    @pl.when(enum_ref[e] > 0)
    def _load_weights():
        wg_scr[...] = wg_ref[...].astype(BF16)
        wu_scr[...] = wu_ref[...].astype(BF16)
        wd_scr[...] = wd_ref[...].astype(BF16)

    def block_body(i, c):
        g = estart_ref[e] + i
        slot = g % 2
        x_copy(g, slot).wait()

        @pl.when(g + 1 < n_blocks)
        def _prefetch():
            x_copy(g + 1, 1 - slot).start()

        @pl.when(g >= 2)
        def _free_out_buffer():
            y_copy(g - 2, slot).wait()

        h = _swiglu(x_scr[slot], wg_scr[...], wu_scr[...])
        y_scr[slot] = _dot(h.astype(BF16), wd_scr[...]).astype(BF16)
        y_copy(g, slot).start()
        return c

    lax.fori_loop(0, enum_ref[e], block_body, 0)

    @pl.when(e == pl.num_programs(0) - 1)
    def _drain():
        for back in (1, 2):
            @pl.when(n_blocks >= back)
            def _wait_out():
                g = n_blocks - back
                y_copy(g, g % 2).wait()


def _moe_experts(estart, enum, meta, xg, wg, wu, wd):
    def w_map(e, estart_ref, enum_ref, meta_ref):
        return (e, 0, 0)

    grid_spec = pltpu.PrefetchScalarGridSpec(
        num_scalar_prefetch=3,
        grid=(N_EXPERTS,),
        in_specs=[
            pl.BlockSpec((None, D_MODEL, EXPERT_DIM), w_map),
            pl.BlockSpec((None, D_MODEL, EXPERT_DIM), w_map),
            pl.BlockSpec((None, EXPERT_DIM, D_MODEL), w_map),
            pl.BlockSpec(memory_space=pl.ANY),
        ],
        out_specs=pl.BlockSpec(memory_space=pl.ANY),
        scratch_shapes=[
            pltpu.VMEM((2, EXPERT_BLOCK, D_MODEL), BF16),
            pltpu.VMEM((2, EXPERT_BLOCK, D_MODEL), BF16),
            pltpu.VMEM((D_MODEL, EXPERT_DIM), BF16),
            pltpu.VMEM((D_MODEL, EXPERT_DIM), BF16),
            pltpu.VMEM((EXPERT_DIM, D_MODEL), BF16),
            pltpu.SemaphoreType.DMA((2,)),
            pltpu.SemaphoreType.DMA((2,)),
        ],
    )
    return pl.pallas_call(
        _experts_kernel,
        grid_spec=grid_spec,
        out_shape=jax.ShapeDtypeStruct(xg.shape, BF16),
        compiler_params=pltpu.CompilerParams(
            dimension_semantics=("arbitrary",), vmem_limit_bytes=VMEM_LIMIT_BYTES),
        name="moe_experts",
    )(estart, enum, meta, wg, wu, wd, xg)


def _combine_kernel(pc_ref, gbase_ref, tchunks_ref,
                    x_ref, gates_ref, yg_ref, wgs_ref, wus_ref, wds_ref, ln2g_ref, ln2b_ref,
                    out_a_ref, out_b_ref, ys_scr, sems, *, tiles_a):
    t = pl.program_id(0)
    n_tiles = pl.num_programs(0)
    slot = t % 2

    def start_gather(tile, dst_slot):
        def expert_body(e, dst):
            n = pc_ref[tile, e]
            _start_run(yg_ref, gbase_ref[tile, e], ys_scr.at[dst_slot], dst, n,
                       sems.at[dst_slot])
            return dst + n

        lax.fori_loop(0, N_EXPERTS, expert_body, 0)

    @pl.when(t == 0)
    def _prime():
        ys_scr[...] = jnp.zeros_like(ys_scr)
        start_gather(0, 0)

    @pl.when(t + 1 < n_tiles)
    def _prefetch():
        start_gather(t + 1, 1 - slot)

    gates = gates_ref[...]
    sel = gates > 0.0
    sel_bf = _bf(sel)
    rankp = jnp.where(sel, _rank_in_run(sel_bf), -1.0)
    pc_row, off_row = _run_layout_rows(sel_bf)
    pc_col, off_col = _run_layout_cols(jnp.where(sel, 1.0, 0.0))
    pad = jnp.zeros((128 - N_EXPERTS, MOE_TILE), F32)
    gates_t = jnp.concatenate([gates, pad], axis=0).T.astype(BF16)
    rankp_t = jnp.concatenate([rankp, pad], axis=0).T.astype(BF16)
    s_id = lax.broadcasted_iota(jnp.int32, (N_EXPERTS, TILE_SLOTS), 1).astype(F32)
    owner = _bf((s_id >= off_col[:, :1]) & (s_id < off_col[:, :1] + pc_col[:, :1]))
    owner = jnp.concatenate([owner, jnp.zeros((128 - N_EXPERTS, TILE_SLOTS), BF16)], axis=0)
    rank_of_slot = _dot(rankp_t, owner)
    gate_of_slot = _dot(gates_t, owner)
    o16 = jnp.concatenate([off_row * (1.0 / RUN_ALIGN), jnp.zeros((16, 128 - N_EXPERTS), F32)],
                          axis=1).astype(BF16)
    off_of_slot = _dot(o16, owner)[:1] * RUN_ALIGN
    s_row = lax.broadcasted_iota(jnp.int32, (1, TILE_SLOTS), 1).astype(F32)
    weights = jnp.where(rank_of_slot + off_of_slot == s_row, gate_of_slot, 0.0).astype(BF16)

    x = x_ref[...]
    xb = x.astype(BF16)
    hs = _swiglu(xb, wgs_ref[...].astype(BF16), wus_ref[...].astype(BF16))
    shared = _dot(hs.astype(BF16), wds_ref[...].astype(BF16))

    _wait_rows(yg_ref, ys_scr.at[slot], sems.at[slot], tchunks_ref[t] * RUN_ALIGN)
    routed = _dot(weights, ys_scr[slot])
    y = _layer_norm(ALPHA * x + (routed + shared), ln2g_ref[...], ln2b_ref[...])

    @pl.when(t < tiles_a)
    def _store_a():
        out_a_ref[...] = y

    @pl.when(t >= tiles_a)
    def _store_b():
        out_b_ref[...] = y


def _moe_combine(x1, gates, yg, pc, gbase, tchunks, wgs, wus, wds, ln2g, ln2b, rows_a):
    n_tok = x1.shape[0]
    tiles_a = rows_a // MOE_TILE
    smem = pl.BlockSpec(memory_space=pltpu.SMEM)
    return pl.pallas_call(
        functools.partial(_combine_kernel, tiles_a=tiles_a),
        grid=(n_tok // MOE_TILE,),
        in_specs=[
            smem, smem, smem,
            pl.BlockSpec((MOE_TILE, D_MODEL), lambda t: (t, 0)),
            pl.BlockSpec((N_EXPERTS, MOE_TILE), lambda t: (0, t)),
            pl.BlockSpec(memory_space=pl.ANY),
            _const_spec((D_MODEL, SHARED_DIM)),
            _const_spec((D_MODEL, SHARED_DIM)),
            _const_spec((SHARED_DIM, D_MODEL)),
            _const_spec((1, D_MODEL)),
            _const_spec((1, D_MODEL)),
        ],
        out_specs=[
            pl.BlockSpec((MOE_TILE, D_MODEL), lambda t: (jnp.minimum(t, tiles_a - 1), 0)),
            pl.BlockSpec((MOE_TILE, D_MODEL), lambda t: (jnp.maximum(t - tiles_a, 0), 0)),
        ],
        out_shape=[
            jax.ShapeDtypeStruct((rows_a, D_MODEL), F32),
            jax.ShapeDtypeStruct((n_tok - rows_a, D_MODEL), F32),
        ],
        scratch_shapes=[
            pltpu.VMEM((2, TILE_SLOTS, D_MODEL), BF16),
            pltpu.SemaphoreType.DMA((2,)),
        ],
        compiler_params=pltpu.CompilerParams(
            dimension_semantics=("arbitrary",), vmem_limit_bytes=VMEM_LIMIT_BYTES),
        name="moe_combine",
    )(pc, gbase, tchunks, x1, gates, yg, wgs, wus, wds, ln2g, ln2b)


def _moe(x1, rows_a, wrt, rbias, wg, wu, wd, wgs, wus, wds, ln2g, ln2b):
    n_tiles = x1.shape[0] // MOE_TILE
    gates, pc3 = _moe_route(x1, wrt, rbias)
    pc = pc3.reshape(n_tiles, 128)
    gbase, tchunks, estart, enum, gapstart, gapn, meta = _moe_plan(pc)
    n_rows = _max_blocks(n_tiles) * EXPERT_BLOCK
    xg = _moe_dispatch(x1, gates, pc, gbase, tchunks, gapstart, gapn, n_rows)
    yg = _moe_experts(estart, enum, meta, xg, wg, wu, wd)
    return _moe_combine(x1, gates, yg, pc, gbase, tchunks, wgs, wus, wds, ln2g, ln2b, rows_a)


def kernel(x_prompt, x_sample, cache_win_k, cache_win_v, rel_bias_table, w_in, b_in, attn_sinks,
           sg_ln_g, sg_ln_b, sg_w, sg_b, w_proj_a, w_proj_b, w_out, ln1_g, ln1_b, w_router,
           router_bias, w_gate_e, w_up_e, w_down_e, w_gate_s, w_up_s, w_down_s, ln2_g, ln2_b):
    assert DEPTH == 1 and w_in.shape[0] == 1
    batch, seq, _ = x_prompt.shape
    nb, ds, _ = x_sample.shape
    win = cache_win_k.shape[2]

    w_in_bf = w_in[0].astype(BF16)
    wpa = w_proj_a[0].astype(BF16)
    wpb = w_proj_b[0].astype(BF16)
    wout = w_out[0].astype(BF16)
    b_in2 = b_in[0].reshape(1, IN_W)
    sgg = sg_ln_g[0].reshape(1, SG_WIDTH)
    sgb = sg_ln_b[0].reshape(1, SG_WIDTH)
    sgw = sg_w[0]
    sgbias = sg_b[0].reshape(N_SG_GROUPS, CHUNK, 1)
    ln1g = ln1_g[0].reshape(1, D_MODEL)
    ln1b = ln1_b[0].reshape(1, D_MODEL)
    sinks = attn_sinks[0]

    x1_p, wk_p, wv_p = _mixer_prompt(
        x_prompt, nb * ds, rel_bias_table, sinks, w_in_bf, b_in2, sgg, sgb, sgw, sgbias, wpa, wpb,
        wout, ln1g, ln1b)
    x1, wk_s, wv_s, cvs = _mixer_sample(
        x_sample, cache_win_k[0].reshape(nb, win, KV_W), cache_win_v[0].reshape(nb, win, KV_W),
        x1_p, rel_bias_table, sinks, w_in_bf, b_in2, sgg, sgb, sgw, sgbias, wpa, wpb, wout,
        ln1g, ln1b)

    y_p, y_s = _moe(x1, batch * seq, wrt=w_router[0].T,
                    rbias=router_bias[0].reshape(N_EXPERTS, 1),
                    wg=w_gate_e[0], wu=w_up_e[0], wd=w_down_e[0],
                    wgs=w_gate_s[0], wus=w_up_s[0], wds=w_down_s[0],
                    ln2g=ln2_g[0].reshape(1, D_MODEL), ln2b=ln2_b[0].reshape(1, D_MODEL))
    y_p = y_p.reshape(batch, seq, D_MODEL)
    y_s = y_s.reshape(nb, ds, D_MODEL)

    kv_shape = (1, -1, WINDOW, N_KV_HEADS, HEAD_DIM)
    return (y_p, y_s,
            wk_p.reshape(kv_shape), wv_p.reshape(kv_shape),
            wk_s.reshape(1, nb, win, N_KV_HEADS, HEAD_DIM),
            wv_s.reshape(1, nb, win, N_KV_HEADS, HEAD_DIM),
            cvs.reshape(1, nb, ds, N_SG_GROUPS, SG_GROUP_DIM))
```

```python
import functools
import math

import jax
import jax.numpy as jnp
import numpy as np
from jax import lax
from jax.experimental import pallas as pl
from jax.experimental.pallas import tpu as pltpu

F32 = jnp.float32
BF16 = jnp.bfloat16

D_MODEL = 1024
DEPTH = 1
HEAD_DIM = 64
N_Q_HEADS = 16
N_KV_HEADS = 2
Q_PER_KV = N_Q_HEADS // N_KV_HEADS
WINDOW = 128
ATTN_SCALE = HEAD_DIM ** -0.5
NEG_INF = -1e30
N_BUCKETS = 32
BUCKET_MAX_EXACT = 16
BUCKET_MAX_DIST = 128
CHUNK = 128
N_SG_GROUPS = 4
SG_GROUP_DIM = 128
SG_WIDTH = N_SG_GROUPS * SG_GROUP_DIM
Q_W = N_Q_HEADS * HEAD_DIM
KV_W = N_KV_HEADS * HEAD_DIM
Q_END = Q_W
K_END = Q_END + KV_W
V_END = K_END + KV_W
U_END = V_END + SG_WIDTH
VS_END = U_END + SG_WIDTH
GA_END = VS_END + D_MODEL
IN_W = GA_END + D_MODEL
N_EXPERTS = 64
TOP_K = 8
N_EXPERT_GROUPS = 8
EXPERTS_PER_GROUP = N_EXPERTS // N_EXPERT_GROUPS
TOPK_GROUPS = 4
EXPERT_DIM = 256
SHARED_DIM = 256
ROUTED_SCALE = 2.5
ALPHA = (2 * DEPTH) ** 0.25
LN_EPS = 1e-5

VMEM_LIMIT_BYTES = 56 * 1024 * 1024

PROMPT_STEP = 512
SAMPLE_SEQS_PER_STEP = 32
MOE_TILE = 256
RUN_ALIGN = 16
COPY_ROWS = 4 * RUN_ALIGN
EXPERT_BLOCK = 1024
TILE_SLOTS = -(-(MOE_TILE * TOP_K + N_EXPERTS * (RUN_ALIGN - 1)) // 512) * 512


def _t5_bucket_np(dist):
    d = np.maximum(dist, 0)
    ratio = np.maximum(d, 1).astype(np.float32) / np.float32(BUCKET_MAX_EXACT)
    large = BUCKET_MAX_EXACT + (
        np.log(ratio) / np.float32(math.log(BUCKET_MAX_DIST / BUCKET_MAX_EXACT))
        * np.float32(N_BUCKETS - BUCKET_MAX_EXACT)).astype(np.int32)
    large = np.minimum(large, N_BUCKETS - 1)
    return np.where(d < BUCKET_MAX_EXACT, d, large).astype(np.int32)


def _layer_norm(x, g, b):
    mu = jnp.mean(x, -1, keepdims=True)
    xc = x - mu
    var = jnp.mean(xc * xc, -1, keepdims=True)
    return xc * lax.rsqrt(var + LN_EPS) * g + b


def _gelu(x):
    return jax.nn.gelu(x)


def _dot(a, b):
    return jnp.dot(a, b, preferred_element_type=F32)


def _dot_nt(a, b):
    return lax.dot_general(a, b, (((1,), (1,)), ((), ())), preferred_element_type=F32)


def _project(xb, w_in_ref, b_in_ref, lo, hi):
    return _dot(xb, w_in_ref[:, lo:hi]) + b_in_ref[:, lo:hi]


def _expand_bias(bucket, table_ref, head):
    acc = jnp.zeros(bucket.shape, F32)
    for b in range(N_BUCKETS):
        acc = jnp.where(bucket == b, table_ref[b, head], acc)
    return acc


def _merge_and_norm(x, a_bf, s_bf, ga, gb, wpa_ref, wpb_ref, wout_ref, g_ref, b_ref):
    pa = _dot(a_bf, wpa_ref[...])
    pb = _dot(s_bf, wpb_ref[...])
    hpre = jax.nn.sigmoid(ga) * pa + jax.nn.sigmoid(gb) * pb
    h = _dot(hpre.astype(BF16), wout_ref[...])
    return _layer_norm(ALPHA * x + h, g_ref[...], b_ref[...])


def _mixer_prompt_kernel(table_ref, bucket_ref, sink_ref, x_ref, w_in_ref, b_in_ref,
                         sgg_ref, sgb_ref, sgw_ref, sgbias_ref, wpa_ref, wpb_ref, wout_ref,
                         ln1g_ref, ln1b_ref,
                         x1_ref, wk_ref, wv_ref,
                         bias_scr, tril_scr, kprev_scr, vprev_scr, a_scr, s_scr):
    b_idx = pl.program_id(0)
    n_idx = pl.program_id(1)
    n_blocks = PROMPT_STEP // WINDOW

    @pl.when((b_idx == 0) & (n_idx == 0))
    def _init_tables():
        bucket = bucket_ref[...]
        for h in range(N_Q_HEADS):
            g, r = divmod(h, Q_PER_KV)
            pair, parity = divmod(r, 2)
            bias_scr[g, pair, :, parity * 2 * WINDOW:(parity + 1) * 2 * WINDOW] = (
                _expand_bias(bucket, table_ref, h))
        row = lax.broadcasted_iota(jnp.int32, (CHUNK, CHUNK), 0)
        col = lax.broadcasted_iota(jnp.int32, (CHUNK, CHUNK), 1)
        for g in range(N_SG_GROUPS):
            tril_scr[g] = jnp.where(row >= col, sgw_ref[g], 0.0).astype(BF16)

    @pl.when(n_idx == 0)
    def _reset_carry():
        kprev_scr[...] = jnp.zeros_like(kprev_scr)
        vprev_scr[...] = jnp.zeros_like(vprev_scr)

    x = x_ref[...]
    xb = x.astype(BF16)
    q_bf = (_project(xb, w_in_ref, b_in_ref, 0, Q_END) * ATTN_SCALE).astype(BF16)
    k = _project(xb, w_in_ref, b_in_ref, Q_END, K_END)
    v = _project(xb, w_in_ref, b_in_ref, K_END, V_END)

    @pl.when(n_idx == pl.num_programs(1) - 1)
    def _emit_window():
        wk_ref[...] = k[PROMPT_STEP - WINDOW:, :]
        wv_ref[...] = v[PROMPT_STEP - WINDOW:, :]

    low = lax.broadcasted_iota(jnp.int32, (PROMPT_STEP, KV_W), 1) < HEAD_DIM

    def lane_halves(t):
        t_sw = pltpu.roll(t, HEAD_DIM, axis=1)
        zero = jnp.zeros_like(t)
        return [[jnp.where(low, t, zero).astype(BF16), jnp.where(low, zero, t_sw).astype(BF16)],
                [jnp.where(low, t_sw, zero).astype(BF16), jnp.where(low, zero, t).astype(BF16)]]

    k_half = lane_halves(k)
    v_half = lane_halves(v)

    row = lax.broadcasted_iota(jnp.int32, (WINDOW, 4 * WINDOW), 0)
    col = lax.broadcasted_iota(jnp.int32, (WINDOW, 4 * WINDOW), 1) % (2 * WINDOW)
    dist = row + WINDOW - col
    band_valid = (dist >= 0) & (dist <= WINDOW)
    first_valid = band_valid & ((col >= WINDOW) | (n_idx > 0))
    lane_low = lax.broadcasted_iota(jnp.int32, (Q_PER_KV // 2, WINDOW, 2 * HEAD_DIM), 2) < HEAD_DIM
    ones_rows = lax.broadcasted_iota(jnp.int32, (4 * WINDOW, 2 * HEAD_DIM), 0) < 2 * WINDOW
    ones_cols = lax.broadcasted_iota(jnp.int32, (4 * WINDOW, 2 * HEAD_DIM), 1) < HEAD_DIM
    sum_block = _bf(ones_rows == ones_cols)

    for j in range(n_blocks):
        r0, r1 = j * WINDOW, (j + 1) * WINDOW
        valid = first_valid if j == 0 else band_valid
        for g in range(N_KV_HEADS):
            def band(cur, prev_scr):
                parts = []
                for s in range(2):
                    prev = prev_scr[2 * g + s] if j == 0 else cur[g][s][r0 - WINDOW:r0]
                    parts += [prev, cur[g][s][r0:r1]]
                return jnp.concatenate(parts, axis=0)

            kd = band(k_half, kprev_scr)
            vd = jnp.concatenate([band(v_half, vprev_scr), sum_block], axis=1)
            q0 = g * Q_PER_KV * HEAD_DIM
            qp = jnp.concatenate(
                [q_bf[r0:r1, q0 + pr * 2 * HEAD_DIM:q0 + (pr + 1) * 2 * HEAD_DIM]
                 for pr in range(Q_PER_KV // 2)], axis=0)
            logits = _dot_nt(qp, kd).reshape(Q_PER_KV // 2, WINDOW, 4 * WINDOW)
            logits = jnp.where(valid[None], logits + bias_scr[g], NEG_INF)
            probs, sink_terms = [], []
            for s in range(2):
                l_s = logits[:, :, s * 2 * WINDOW:(s + 1) * 2 * WINDOW]
                sink = sink_ref[s, g]
                m = jnp.maximum(jnp.max(l_s, -1, keepdims=True), sink)
                probs.append(jnp.exp(l_s - m))
                sink_terms.append(jnp.broadcast_to(jnp.exp(sink - m), lane_low.shape))
            p = jnp.concatenate(probs, axis=-1).reshape(4 * WINDOW, 4 * WINDOW).astype(BF16)
            out = _dot(p, vd).reshape(Q_PER_KV // 2, WINDOW, 4 * HEAD_DIM)
            den = out[:, :, 2 * HEAD_DIM:] + jnp.where(lane_low, sink_terms[0], sink_terms[1])
            o = out[:, :, :2 * HEAD_DIM] / den
            for pr in range(Q_PER_KV // 2):
                a_scr[r0:r1, q0 + pr * 2 * HEAD_DIM:q0 + (pr + 1) * 2 * HEAD_DIM] = (
                    o[pr].astype(BF16))

    for g in range(N_KV_HEADS):
        for s in range(2):
            kprev_scr[2 * g + s] = k_half[g][s][PROMPT_STEP - WINDOW:]
            vprev_scr[2 * g + s] = v_half[g][s][PROMPT_STEP - WINDOW:]

    u = _gelu(_project(xb, w_in_ref, b_in_ref, V_END, U_END))
    vs = _gelu(_project(xb, w_in_ref, b_in_ref, U_END, VS_END))
    vs_bf = _layer_norm(vs, sgg_ref[...], sgb_ref[...]).astype(BF16)
    for j in range(n_blocks):
        r0, r1 = j * WINDOW, (j + 1) * WINDOW
        for g in range(N_SG_GROUPS):
            c0, c1 = g * SG_GROUP_DIM, (g + 1) * SG_GROUP_DIM
            sg = _dot(tril_scr[g], vs_bf[r0:r1, c0:c1]) + sgbias_ref[g]
            s_scr[r0:r1, c0:c1] = (u[r0:r1, c0:c1] * sg).astype(BF16)

    ga = _project(xb, w_in_ref, b_in_ref, VS_END, GA_END)
    gb = _project(xb, w_in_ref, b_in_ref, GA_END, IN_W)
    x1_ref[...] = _merge_and_norm(x, a_scr[...], s_scr[...], ga, gb, wpa_ref, wpb_ref,
                                  wout_ref, ln1g_ref, ln1b_ref)


def _const_spec(shape):
    zeros = (0,) * len(shape)
    return pl.BlockSpec(shape, lambda *_: zeros, pipeline_mode=pl.Buffered(1))


def _mixer_prompt(x, extra_rows, table, sinks, w_in_bf, b_in, sgg, sgb, sgw, sgbias, wpa, wpb,
                  wout, ln1g, ln1b):
    batch, seq, _ = x.shape
    n_steps = seq // PROMPT_STEP
    dist = np.arange(WINDOW)[:, None] + WINDOW - np.arange(2 * WINDOW)[None, :]
    bucket = jnp.asarray(_t5_bucket_np(dist))
    sink_pairs = jnp.transpose(sinks.reshape(N_KV_HEADS, Q_PER_KV // 2, 2), (2, 0, 1)).reshape(
        2, N_KV_HEADS, Q_PER_KV // 2, 1, 1)
    smem = pl.BlockSpec(memory_space=pltpu.SMEM)
    in_specs = [
        smem,
        _const_spec((WINDOW, 2 * WINDOW)),
        _const_spec((2, N_KV_HEADS, Q_PER_KV // 2, 1, 1)),
        pl.BlockSpec((None, PROMPT_STEP, D_MODEL), lambda b, n: (b, n, 0)),
        _const_spec((D_MODEL, IN_W)),
        _const_spec((1, IN_W)),
        _const_spec((1, SG_WIDTH)),
        _const_spec((1, SG_WIDTH)),
        _const_spec((N_SG_GROUPS, CHUNK, CHUNK)),
        _const_spec((N_SG_GROUPS, CHUNK, 1)),
        _const_spec((Q_W, D_MODEL)),
        _const_spec((SG_WIDTH, D_MODEL)),
        _const_spec((D_MODEL, D_MODEL)),
        _const_spec((1, D_MODEL)),
        _const_spec((1, D_MODEL)),
    ]
    out_specs = [
        pl.BlockSpec((PROMPT_STEP, D_MODEL), lambda b, n: (b * n_steps + n, 0)),
        pl.BlockSpec((None, WINDOW, KV_W), lambda b, n: (b, 0, 0)),
        pl.BlockSpec((None, WINDOW, KV_W), lambda b, n: (b, 0, 0)),
    ]
    out_shape = [
        jax.ShapeDtypeStruct((batch * seq + extra_rows, D_MODEL), F32),
        jax.ShapeDtypeStruct((batch, WINDOW, KV_W), F32),
        jax.ShapeDtypeStruct((batch, WINDOW, KV_W), F32),
    ]
    scratch = [
        pltpu.VMEM((N_KV_HEADS, Q_PER_KV // 2, WINDOW, 4 * WINDOW), F32),
        pltpu.VMEM((N_SG_GROUPS, CHUNK, CHUNK), BF16),
        pltpu.VMEM((2 * N_KV_HEADS, WINDOW, KV_W), BF16),
        pltpu.VMEM((2 * N_KV_HEADS, WINDOW, KV_W), BF16),
        pltpu.VMEM((PROMPT_STEP, Q_W), BF16),
        pltpu.VMEM((PROMPT_STEP, SG_WIDTH), BF16),
    ]
    return pl.pallas_call(
        _mixer_prompt_kernel,
        grid=(batch, n_steps),
        in_specs=in_specs,
        out_specs=out_specs,
        out_shape=out_shape,
        scratch_shapes=scratch,
        compiler_params=pltpu.CompilerParams(
            dimension_semantics=("arbitrary", "arbitrary"),
            vmem_limit_bytes=VMEM_LIMIT_BYTES),
        name="mixer_prompt",
    )(table, bucket, sink_pairs, x, w_in_bf, b_in, sgg, sgb, sgw, sgbias, wpa, wpb, wout,
      ln1g, ln1b)


def _mixer_sample_kernel(table_ref, bucket_c_ref, bucket_n_ref, sink_ref, x_ref, ck_ref, cv_ref,
                         w_in_ref, b_in_ref, sgg_ref, sgb_ref, sgw8_ref, sgbias8_ref,
                         wpa_ref, wpb_ref, wout_ref, ln1g_ref, ln1b_ref, x1_in_ref,
                         x1_ref, wk_ref, wv_ref, cvs_ref,
                         bias_c_scr, bias_n_scr, a_scr, s_scr):
    del x1_in_ref
    nseq, ds = SAMPLE_SEQS_PER_STEP, x_ref.shape[1]
    rows = nseq * ds
    win = ck_ref.shape[1]
    qrows = Q_PER_KV * ds

    @pl.when(pl.program_id(0) == 0)
    def _init_tables():
        bc = bucket_c_ref[...]
        bn = bucket_n_ref[...]
        for h in range(N_Q_HEADS):
            g, r = divmod(h, Q_PER_KV)
            bias_c_scr[g, r * ds:(r + 1) * ds, :] = _expand_bias(bc, table_ref, h)
            bias_n_scr[g, r * ds:(r + 1) * ds, :] = _expand_bias(bn, table_ref, h)

    x = x_ref[...].reshape(rows, D_MODEL)
    xb = x.astype(BF16)
    q = _project(xb, w_in_ref, b_in_ref, 0, Q_END) * ATTN_SCALE
    k = _project(xb, w_in_ref, b_in_ref, Q_END, K_END)
    v = _project(xb, w_in_ref, b_in_ref, K_END, V_END)
    ck = ck_ref[...]
    cv = cv_ref[...]
    wk_ref[:, :win - ds, :] = ck[:, ds:, :]
    wk_ref[:, win - ds:, :] = k.reshape(nseq, ds, KV_W)
    wv_ref[:, :win - ds, :] = cv[:, ds:, :]
    wv_ref[:, win - ds:, :] = v.reshape(nseq, ds, KV_W)
    nk = wk_ref[...]
    nv = wv_ref[...]

    q3 = q.reshape(nseq, ds, Q_W)
    t_q = lax.broadcasted_iota(jnp.int32, (qrows, win), 0) % ds
    col = lax.broadcasted_iota(jnp.int32, (qrows, win), 1)
    dist_c = t_q + win - col
    valid_c = (dist_c >= 0) & (dist_c <= WINDOW)
    dist_n = t_q - (col - (win - ds))
    valid_n = (col >= win - ds) & (dist_n >= 0) & (dist_n <= WINDOW)

    for g in range(N_KV_HEADS):
        c0, c1 = g * HEAD_DIM, (g + 1) * HEAD_DIM
        h0 = g * Q_PER_KV
        qs = jnp.concatenate(
            [q3[:, :, (h0 + r) * HEAD_DIM:(h0 + r + 1) * HEAD_DIM] for r in range(Q_PER_KV)],
            axis=1).astype(BF16)
        kc = ck[:, :, c0:c1].astype(BF16)
        vc = cv[:, :, c0:c1].astype(BF16)
        kn = nk[:, :, c0:c1].astype(BF16)
        vn = nv[:, :, c0:c1].astype(BF16)
        lc = jnp.einsum('bqd,bkd->bqk', qs, kc, preferred_element_type=F32)
        ln = jnp.einsum('bqd,bkd->bqk', qs, kn, preferred_element_type=F32)
        lc = jnp.where(valid_c[None], lc + bias_c_scr[g][None], NEG_INF)
        ln = jnp.where(valid_n[None], ln + bias_n_scr[g][None], NEG_INF)
        sink = sink_ref[g]
        m = jnp.maximum(jnp.maximum(jnp.max(lc, -1, keepdims=True),
                                    jnp.max(ln, -1, keepdims=True)), sink[None])
        pc = jnp.exp(lc - m)
        pn = jnp.exp(ln - m)
        den = (jnp.sum(pc, -1, keepdims=True) + jnp.sum(pn, -1, keepdims=True)
               + jnp.exp(sink[None] - m))
        o = (jnp.einsum('bqk,bkd->bqd', pc.astype(BF16), vc, preferred_element_type=F32)
             + jnp.einsum('bqk,bkd->bqd', pn.astype(BF16), vn, preferred_element_type=F32))
        o = o / den
        for r in range(Q_PER_KV):
            a_scr[:, :, (h0 + r) * HEAD_DIM:(h0 + r + 1) * HEAD_DIM] = (
                o[:, r * ds:(r + 1) * ds, :])

    u = _gelu(_project(xb, w_in_ref, b_in_ref, V_END, U_END))
    vs = _gelu(_project(xb, w_in_ref, b_in_ref, U_END, VS_END))
    vs_ln = _layer_norm(vs, sgg_ref[...], sgb_ref[...])
    cvs_ref[...] = vs_ln.reshape(nseq, ds, SG_WIDTH)
    vq = vs_ln.astype(BF16).astype(F32).reshape(nseq, ds, SG_WIDTH)
    u3 = u.reshape(nseq, ds, SG_WIDTH)
    i_row = lax.broadcasted_iota(jnp.int32, (ds, 1), 0)
    for g in range(N_SG_GROUPS):
        c0, c1 = g * SG_GROUP_DIM, (g + 1) * SG_GROUP_DIM
        acc = jnp.broadcast_to(sgbias8_ref[g][None], (nseq, ds, SG_GROUP_DIM))
        for j in range(ds):
            w_col = jnp.where(i_row >= j, sgw8_ref[g, j], 0.0)
            w_col = w_col.astype(BF16).astype(F32)
            acc = acc + w_col[None] * vq[:, j:j + 1, c0:c1]
        s_scr[:, :, c0:c1] = u3[:, :, c0:c1] * acc

    ga = _project(xb, w_in_ref, b_in_ref, VS_END, GA_END)
    gb = _project(xb, w_in_ref, b_in_ref, GA_END, IN_W)
    x1 = _merge_and_norm(x, a_scr[...].reshape(rows, Q_W).astype(BF16),
                         s_scr[...].reshape(rows, SG_WIDTH).astype(BF16),
                         ga, gb, wpa_ref, wpb_ref, wout_ref, ln1g_ref, ln1b_ref)
    x1_ref[...] = x1


def _mixer_sample(x, cache_k, cache_v, x1_all, table, sinks, w_in_bf, b_in, sgg, sgb, sgw,
                  sgbias, wpa, wpb, wout, ln1g, ln1b):
    nb, ds, _ = x.shape
    first_block = (x1_all.shape[0] - nb * ds) // (SAMPLE_SEQS_PER_STEP * ds)
    win = cache_k.shape[1]
    nseq = SAMPLE_SEQS_PER_STEP
    qrows = Q_PER_KV * ds
    t = np.arange(ds)[:, None]
    bucket_c = jnp.asarray(_t5_bucket_np(t + win - np.arange(win)[None, :]))
    bucket_n = jnp.asarray(_t5_bucket_np(t - (np.arange(win)[None, :] - (win - ds))))
    sink_rows = jnp.repeat(sinks.reshape(N_KV_HEADS, Q_PER_KV), ds, axis=1).reshape(
        N_KV_HEADS, qrows, 1)
    sgw8 = jnp.transpose(sgw[:, :ds, :ds], (0, 2, 1))[..., None]
    sgbias8 = sgbias[:, :ds, :]
    smem = pl.BlockSpec(memory_space=pltpu.SMEM)
    in_specs = [
        smem,
        _const_spec((ds, win)),
        _const_spec((ds, win)),
        _const_spec((N_KV_HEADS, qrows, 1)),
        pl.BlockSpec((nseq, ds, D_MODEL), lambda i: (i, 0, 0)),
        pl.BlockSpec((nseq, win, KV_W), lambda i: (i, 0, 0)),
        pl.BlockSpec((nseq, win, KV_W), lambda i: (i, 0, 0)),
        _const_spec((D_MODEL, IN_W)),
        _const_spec((1, IN_W)),
        _const_spec((1, SG_WIDTH)),
        _const_spec((1, SG_WIDTH)),
        _const_spec((N_SG_GROUPS, ds, ds, 1)),
        _const_spec((N_SG_GROUPS, ds, 1)),
        _const_spec((Q_W, D_MODEL)),
        _const_spec((SG_WIDTH, D_MODEL)),
        _const_spec((D_MODEL, D_MODEL)),
        _const_spec((1, D_MODEL)),
        _const_spec((1, D_MODEL)),
        pl.BlockSpec(memory_space=pl.ANY),
    ]
    out_specs = [
        pl.BlockSpec((nseq * ds, D_MODEL), lambda i: (first_block + i, 0)),
        pl.BlockSpec((nseq, win, KV_W), lambda i: (i, 0, 0)),
        pl.BlockSpec((nseq, win, KV_W), lambda i: (i, 0, 0)),
        pl.BlockSpec((nseq, ds, SG_WIDTH), lambda i: (i, 0, 0)),
    ]
    out_shape = [
        jax.ShapeDtypeStruct(x1_all.shape, F32),
        jax.ShapeDtypeStruct((nb, win, KV_W), F32),
        jax.ShapeDtypeStruct((nb, win, KV_W), F32),
        jax.ShapeDtypeStruct((nb, ds, SG_WIDTH), F32),
    ]
    scratch = [
        pltpu.VMEM((N_KV_HEADS, qrows, win), F32),
        pltpu.VMEM((N_KV_HEADS, qrows, win), F32),
        pltpu.VMEM((nseq, ds, Q_W), F32),
        pltpu.VMEM((nseq, ds, SG_WIDTH), F32),
    ]
    return pl.pallas_call(
        _mixer_sample_kernel,
        grid=(nb // nseq,),
        in_specs=in_specs,
        out_specs=out_specs,
        out_shape=out_shape,
        scratch_shapes=scratch,
        input_output_aliases={len(in_specs) - 1: 0},
        compiler_params=pltpu.CompilerParams(
            dimension_semantics=("arbitrary",),
            vmem_limit_bytes=VMEM_LIMIT_BYTES),
        name="mixer_sample",
    )(table, bucket_c, bucket_n, sink_rows, x, cache_k, cache_v, w_in_bf, b_in, sgg, sgb,
      sgw8, sgbias8, wpa, wpb, wout, ln1g, ln1b, x1_all)


def _route(xb, wrt_ref, rbias_ref):
    n = xb.shape[0]
    logits = _dot_nt(wrt_ref[...].astype(BF16), xb)
    scores = jax.nn.sigmoid(logits)
    sel = scores + rbias_ref[...]
    shape3 = (N_EXPERT_GROUPS, EXPERTS_PER_GROUP, n)
    scores3 = scores.reshape(shape3)
    sel3 = sel.reshape(shape3)
    i_in = lax.broadcasted_iota(jnp.int32, shape3, 1)
    g_id = lax.broadcasted_iota(jnp.int32, shape3, 0)
    e_id = g_id * EXPERTS_PER_GROUP + i_in
    neg = -jnp.inf

    m1 = jnp.max(sel3, axis=1, keepdims=True)
    first = jnp.min(jnp.where(sel3 == m1, i_in, EXPERTS_PER_GROUP), axis=1, keepdims=True)
    m2 = jnp.max(jnp.where(i_in == first, neg, sel3), axis=1, keepdims=True)
    gscore = m1 + m2

    gsel = jnp.zeros(gscore.shape, jnp.bool_)
    gid1 = lax.broadcasted_iota(jnp.int32, gscore.shape, 0)
    for _ in range(TOPK_GROUPS):
        m = jnp.max(gscore, axis=0, keepdims=True)
        pick = jnp.min(jnp.where(gscore == m, gid1, N_EXPERT_GROUPS), axis=0, keepdims=True)
        chosen = gid1 == pick
        gsel = gsel | chosen
        gscore = jnp.where(chosen, neg, gscore)
    val = jnp.where(gsel, sel3, NEG_INF)

    esel = jnp.zeros(shape3, jnp.bool_)
    for _ in range(TOP_K):
        m = jnp.max(jnp.max(val, axis=0, keepdims=True), axis=1, keepdims=True)
        cand = jnp.where(val == m, e_id, N_EXPERTS)
        pick = jnp.min(jnp.min(cand, axis=0, keepdims=True), axis=1, keepdims=True)
        chosen = e_id == pick
        esel = esel | chosen
        val = jnp.where(chosen, neg, val)
    w_sel = jnp.where(esel, scores3, 0.0)
    total = jnp.sum(jnp.sum(w_sel, axis=0, keepdims=True), axis=1, keepdims=True)
    gates = w_sel / total * ROUTED_SCALE
    return gates.reshape(N_EXPERTS, n)


def _swiglu(xb, wg, wu):
    return jax.nn.silu(_dot(xb, wg)) * _dot(xb, wu)


def _bf(mask):
    return jnp.where(mask, 1.0, 0.0).astype(BF16)


def _round_up_run(count):
    return jnp.ceil(count * (1.0 / RUN_ALIGN)) * RUN_ALIGN


def _run_layout_rows(sel_bf):
    n = sel_bf.shape[1]
    ones = jnp.ones((16, n), BF16)
    pc = _round_up_run(_dot_nt(ones, sel_bf))
    lower = lax.broadcasted_iota(jnp.int32, (N_EXPERTS, N_EXPERTS), 0)
    upper = lax.broadcasted_iota(jnp.int32, (N_EXPERTS, N_EXPERTS), 1)
    off = _dot(pc.astype(BF16), _bf(lower < upper))
    return pc, off


def _run_layout_cols(sel_f32):
    cnt = jnp.sum(sel_f32, axis=1, keepdims=True)
    pc = jnp.broadcast_to(_round_up_run(cnt), (N_EXPERTS, 128))
    row = lax.broadcasted_iota(jnp.int32, (N_EXPERTS, N_EXPERTS), 0)
    col = lax.broadcasted_iota(jnp.int32, (N_EXPERTS, N_EXPERTS), 1)
    off = _dot(_bf(col < row), pc.astype(BF16))
    return pc, off


def _rank_in_run(sel_bf):
    n = sel_bf.shape[1]
    m_id = lax.broadcasted_iota(jnp.int32, (n, n), 0)
    n_id = lax.broadcasted_iota(jnp.int32, (n, n), 1)
    return _dot(sel_bf, _bf(m_id < n_id))


def _route_kernel(x_ref, wrt_ref, rbias_ref, gates_ref, pc_ref):
    xb = x_ref[...].astype(BF16)
    gates = _route(xb, wrt_ref, rbias_ref)
    gates_ref[...] = gates
    sel_bf = _bf(gates > 0.0)
    pad = jnp.zeros((128 - N_EXPERTS, sel_bf.shape[1]), BF16)
    ones = jnp.ones((16, sel_bf.shape[1]), BF16)
    cnt = _dot_nt(ones, jnp.concatenate([sel_bf, pad], axis=0))
    pc_ref[...] = _round_up_run(cnt)[:1].astype(jnp.int32)


def _moe_route(x1, wrt, rbias):
    n_tok = x1.shape[0]
    n_tiles = n_tok // MOE_TILE
    return pl.pallas_call(
        _route_kernel,
        grid=(n_tiles,),
        in_specs=[
            pl.BlockSpec((MOE_TILE, D_MODEL), lambda t: (t, 0)),
            _const_spec((N_EXPERTS, D_MODEL)),
            _const_spec((N_EXPERTS, 1)),
        ],
        out_specs=[
            pl.BlockSpec((N_EXPERTS, MOE_TILE), lambda t: (0, t)),
            pl.BlockSpec((None, 1, 128), lambda t: (t, 0, 0)),
        ],
        out_shape=[
            jax.ShapeDtypeStruct((N_EXPERTS, n_tok), F32),
            jax.ShapeDtypeStruct((n_tiles, 1, 128), jnp.int32),
        ],
        compiler_params=pltpu.CompilerParams(dimension_semantics=("arbitrary",)),
        name="moe_route",
    )(x1, wrt, rbias)


PLAN_ESTART, PLAN_ENUM, PLAN_GAPSTART, PLAN_GAPN, PLAN_NBLOCKS = range(5)


def _plan_kernel(pc_ref, gbase_ref, tchunks_ref, ev_ref):
    n_tiles = pc_ref.shape[0]
    pad = jnp.zeros((128 - n_tiles, 128), F32)
    q = jnp.concatenate([pc_ref[...].astype(F32), pad], axis=0) * (1.0 / RUN_ALIGN)
    qb = q.astype(BF16)
    r = lax.broadcasted_iota(jnp.int32, (128, 128), 0)
    c = lax.broadcasted_iota(jnp.int32, (128, 128), 1)
    ones = jnp.ones((128, 128), BF16)
    before = _dot(_bf(c < r), qb) * RUN_ALIGN
    rows_e = _dot(ones[:16], qb) * RUN_ALIGN
    nb = jnp.ceil(rows_e * (1.0 / EXPERT_BLOCK))
    estart = _dot(nb.astype(BF16), _bf(r < c))
    g0 = estart * EXPERT_BLOCK
    gbase_ref[...] = (g0[:1] + before).astype(jnp.int32)
    tchunks_ref[...] = _dot(qb, ones).astype(jnp.int32)
    n_blocks = _dot(nb.astype(BF16), ones)
    row = lax.broadcasted_iota(jnp.int32, (8, 128), 0)
    ev = jnp.zeros((8, 128), F32)
    for i, val in ((PLAN_ESTART, estart), (PLAN_ENUM, nb), (PLAN_GAPSTART, g0 + rows_e),
                   (PLAN_GAPN, (nb * EXPERT_BLOCK - rows_e) * (1.0 / RUN_ALIGN)),
                   (PLAN_NBLOCKS, n_blocks)):
        ev = jnp.where(row == i, val[:8], ev)
    ev_ref[...] = ev.astype(jnp.int32)


def _max_blocks(n_tiles):
    per_tile = MOE_TILE * TOP_K + N_EXPERTS * (RUN_ALIGN - 1)
    rows = n_tiles * per_tile + N_EXPERTS * (EXPERT_BLOCK - RUN_ALIGN)
    return -(-rows // EXPERT_BLOCK)


def _moe_plan(pc):
    assert pc.shape[0] <= 128 and MOE_TILE // RUN_ALIGN < 256
    assert pc.shape[0] * MOE_TILE // EXPERT_BLOCK + 1 < 256
    i32 = jnp.int32
    return pl.pallas_call(
        _plan_kernel,
        out_shape=[
            jax.ShapeDtypeStruct((128, 128), i32),
            jax.ShapeDtypeStruct((128, 128), i32),
            jax.ShapeDtypeStruct((8, 128), i32),
        ],
        name="moe_plan",
    )(pc)


def _aligned(row):
    return row if isinstance(row, int) else pl.multiple_of(row, RUN_ALIGN)


def _rows_copy(src_ref, src_row, dst_ref, dst_row, rows, sem):
    return pltpu.make_async_copy(
        src_ref.at[pl.ds(_aligned(src_row), rows), :],
        dst_ref.at[pl.ds(_aligned(dst_row), rows), :],
        sem)


def _start_run(src_ref, src_row, dst_ref, dst_row, n_rows, sem, src_advances=True):
    def src_at(done):
        return src_row + done if src_advances else src_row

    def big_body(k, c):
        _rows_copy(src_ref, src_at(k * COPY_ROWS), dst_ref, dst_row + k * COPY_ROWS,
                   COPY_ROWS, sem).start()
        return c

    n_big = n_rows // COPY_ROWS
    lax.fori_loop(0, n_big, big_body, 0)
    done = n_big * COPY_ROWS
    mid = n_rows & (2 * RUN_ALIGN)

    @pl.when(mid != 0)
    def _mid():
        _rows_copy(src_ref, src_at(done), dst_ref, dst_row + done, 2 * RUN_ALIGN, sem).start()

    @pl.when((n_rows & RUN_ALIGN) != 0)
    def _small():
        _rows_copy(src_ref, src_at(done + mid), dst_ref, dst_row + done + mid, RUN_ALIGN,
                   sem).start()


def _wait_rows(src_ref, dst_ref, sem, n_rows):
    def big_body(i, c):
        _rows_copy(src_ref, 0, dst_ref, 0, COPY_ROWS, sem).wait()
        return c

    def small_body(i, c):
        _rows_copy(src_ref, 0, dst_ref, 0, RUN_ALIGN, sem).wait()
        return c

    lax.fori_loop(0, n_rows // COPY_ROWS, big_body, 0)
    lax.fori_loop(0, (n_rows % COPY_ROWS) // RUN_ALIGN, small_body, 0)


def _dispatch_kernel(pc_ref, gbase_ref, tchunks_ref, ev_ref,
                     x_ref, gates_ref, xg_ref, xs_scr, zero_scr, sems):
    t = pl.program_id(0)
    n_tiles = pl.num_programs(0)
    slot = t % 2

    @pl.when(t >= 2)
    def _drain_slot():
        _wait_rows(xs_scr.at[slot], xg_ref, sems.at[slot], tchunks_ref[t - 2, 0] * RUN_ALIGN)

    gates = gates_ref[...]
    sel = gates > 0.0
    sel_bf = _bf(sel)
    rankp = jnp.where(sel, _rank_in_run(sel_bf), -1.0).astype(BF16)
    pc_row, off_row = _run_layout_rows(sel_bf)
    pc_col, off_col = _run_layout_cols(jnp.where(sel, 1.0, 0.0))
    s_id = lax.broadcasted_iota(jnp.int32, (TILE_SLOTS, N_EXPERTS), 0).astype(F32)
    owner = _bf((s_id >= off_row[:1]) & (s_id < off_row[:1] + pc_row[:1]))
    rank_of_slot = _dot(owner, rankp)
    off_of_slot = _dot(owner, (off_col * (1.0 / RUN_ALIGN)).astype(BF16)) * RUN_ALIGN
    s_lane = lax.broadcasted_iota(jnp.int32, (TILE_SLOTS, 128), 0).astype(F32)
    onehot = jnp.concatenate(
        [_bf(rank_of_slot[:, c * 128:(c + 1) * 128] + off_of_slot == s_lane)
         for c in range(MOE_TILE // 128)], axis=1)
    xb = x_ref[...].astype(BF16)
    for c in range(TILE_SLOTS // 512):
        xs_scr[slot, c * 512:(c + 1) * 512, :] = _dot(
            onehot[c * 512:(c + 1) * 512, :], xb).astype(BF16)

    def expert_body(e, src):
        n = pc_ref[t, e]
        _start_run(xs_scr.at[slot], src, xg_ref, gbase_ref[t, e], n, sems.at[slot])
        return src + n

    lax.fori_loop(0, N_EXPERTS, expert_body, 0)

    @pl.when(t == n_tiles - 1)
    def _finish():
        zero_scr[...] = jnp.zeros_like(zero_scr)

        def gap_body(e, total):
            n = ev_ref[PLAN_GAPN, e] * RUN_ALIGN
            _start_run(zero_scr, 0, xg_ref, ev_ref[PLAN_GAPSTART, e], n, sems.at[2],
                       src_advances=False)
            return total + n

        gap_rows = lax.fori_loop(0, N_EXPERTS, gap_body, 0)
        _wait_rows(zero_scr, xg_ref, sems.at[2], gap_rows)
        _wait_rows(xs_scr.at[slot], xg_ref, sems.at[slot], tchunks_ref[t, 0] * RUN_ALIGN)

        @pl.when(t >= 1)
        def _drain_other():
            _wait_rows(xs_scr.at[1 - slot], xg_ref, sems.at[1 - slot],
                       tchunks_ref[t - 1, 0] * RUN_ALIGN)


def _moe_dispatch(x1, gates, pc, gbase, tchunks, ev, n_rows):
    n_tiles = x1.shape[0] // MOE_TILE
    smem = pl.BlockSpec(memory_space=pltpu.SMEM)
    return pl.pallas_call(
        _dispatch_kernel,
        grid=(n_tiles,),
        in_specs=[
            smem, smem, smem, smem,
            pl.BlockSpec((MOE_TILE, D_MODEL), lambda t: (t, 0)),
            pl.BlockSpec((N_EXPERTS, MOE_TILE), lambda t: (0, t)),
        ],
        out_specs=pl.BlockSpec(memory_space=pl.ANY),
        out_shape=jax.ShapeDtypeStruct((n_rows, D_MODEL), BF16),
        scratch_shapes=[
            pltpu.VMEM((2, TILE_SLOTS, D_MODEL), BF16),
            pltpu.VMEM((COPY_ROWS, D_MODEL), BF16),
            pltpu.SemaphoreType.DMA((3,)),
        ],
        compiler_params=pltpu.CompilerParams(
            dimension_semantics=("arbitrary",), vmem_limit_bytes=VMEM_LIMIT_BYTES),
        name="moe_dispatch",
    )(pc, gbase, tchunks, ev, x1, gates)


def _experts_kernel(ev_ref, wg_ref, wu_ref, wd_ref, xg_ref, yg_ref,
                    x_scr, y_scr, wg_scr, wu_scr, wd_scr, sem_in, sem_out):
    e = pl.program_id(0)
    n_blocks = ev_ref[PLAN_NBLOCKS, 0]
    first_block = ev_ref[PLAN_ESTART, e]
    blocks_here = ev_ref[PLAN_ENUM, e]

    def x_copy(g, slot):
        rows = pl.ds(pl.multiple_of(g * EXPERT_BLOCK, EXPERT_BLOCK), EXPERT_BLOCK)
        return pltpu.make_async_copy(xg_ref.at[rows, :], x_scr.at[slot], sem_in.at[slot])

    def y_copy(g, slot):
        rows = pl.ds(pl.multiple_of(g * EXPERT_BLOCK, EXPERT_BLOCK), EXPERT_BLOCK)
        return pltpu.make_async_copy(y_scr.at[slot], yg_ref.at[rows, :], sem_out.at[slot])

    @pl.when((e == 0) & (n_blocks > 0))
    def _first_fetch():
        x_copy(0, 0).start()

    @pl.when(blocks_here > 0)
    def _load_weights():
        wg_scr[...] = wg_ref[...].astype(BF16)
        wu_scr[...] = wu_ref[...].astype(BF16)
        wd_scr[...] = wd_ref[...].astype(BF16)

    def block_body(i, c):
        g = first_block + i
        slot = g % 2
        x_copy(g, slot).wait()

        @pl.when(g + 1 < n_blocks)
        def _prefetch():
            x_copy(g + 1, 1 - slot).start()

        @pl.when(g >= 2)
        def _free_out_buffer():
            y_copy(g - 2, slot).wait()

        h = _swiglu(x_scr[slot], wg_scr[...], wu_scr[...])
        y_scr[slot] = _dot(h.astype(BF16), wd_scr[...]).astype(BF16)
        y_copy(g, slot).start()
        return c

    lax.fori_loop(0, blocks_here, block_body, 0)

    @pl.when(e == pl.num_programs(0) - 1)
    def _drain():
        for back in (1, 2):
            @pl.when(n_blocks >= back)
            def _wait_out():
                g = n_blocks - back
                y_copy(g, g % 2).wait()


def _moe_experts(ev, xg, wg, wu, wd):
    def w_map(e, ev_ref):
        return (e, 0, 0)

    grid_spec = pltpu.PrefetchScalarGridSpec(
        num_scalar_prefetch=1,
        grid=(N_EXPERTS,),
        in_specs=[
            pl.BlockSpec((None, D_MODEL, EXPERT_DIM), w_map),
            pl.BlockSpec((None, D_MODEL, EXPERT_DIM), w_map),
            pl.BlockSpec((None, EXPERT_DIM, D_MODEL), w_map),
            pl.BlockSpec(memory_space=pl.ANY),
        ],
        out_specs=pl.BlockSpec(memory_space=pl.ANY),
        scratch_shapes=[
            pltpu.VMEM((2, EXPERT_BLOCK, D_MODEL), BF16),
            pltpu.VMEM((2, EXPERT_BLOCK, D_MODEL), BF16),
            pltpu.VMEM((D_MODEL, EXPERT_DIM), BF16),
            pltpu.VMEM((D_MODEL, EXPERT_DIM), BF16),
            pltpu.VMEM((EXPERT_DIM, D_MODEL), BF16),
            pltpu.SemaphoreType.DMA((2,)),
            pltpu.SemaphoreType.DMA((2,)),
        ],
    )
    return pl.pallas_call(
        _experts_kernel,
        grid_spec=grid_spec,
        out_shape=jax.ShapeDtypeStruct(xg.shape, BF16),
        compiler_params=pltpu.CompilerParams(
            dimension_semantics=("arbitrary",), vmem_limit_bytes=VMEM_LIMIT_BYTES),
        name="moe_experts",
    )(ev, wg, wu, wd, xg)


def _combine_kernel(pc_ref, gbase_ref, tchunks_ref,
                    x_ref, gates_ref, yg_ref, wgs_ref, wus_ref, wds_ref, ln2g_ref, ln2b_ref,
                    out_a_ref, out_b_ref, ys_scr, sems, *, tiles_a):
    t = pl.program_id(0)
    n_tiles = pl.num_programs(0)
    slot = t % 2

    def start_gather(tile, dst_slot):
        def expert_body(e, dst):
            n = pc_ref[tile, e]
            _start_run(yg_ref, gbase_ref[tile, e], ys_scr.at[dst_slot], dst, n,
                       sems.at[dst_slot])
            return dst + n

        lax.fori_loop(0, N_EXPERTS, expert_body, 0)

    @pl.when(t == 0)
    def _prime():
        ys_scr[...] = jnp.zeros_like(ys_scr)
        start_gather(0, 0)

    @pl.when(t + 1 < n_tiles)
    def _prefetch():
        start_gather(t + 1, 1 - slot)

    gates = gates_ref[...]
    sel = gates > 0.0
    sel_bf = _bf(sel)
    rankp = jnp.where(sel, _rank_in_run(sel_bf), -1.0)
    pc_row, off_row = _run_layout_rows(sel_bf)
    pc_col, off_col = _run_layout_cols(jnp.where(sel, 1.0, 0.0))
    pad = jnp.zeros((128 - N_EXPERTS, MOE_TILE), F32)
    gates_t = jnp.concatenate([gates, pad], axis=0).T.astype(BF16)
    rankp_t = jnp.concatenate([rankp, pad], axis=0).T.astype(BF16)
    s_id = lax.broadcasted_iota(jnp.int32, (N_EXPERTS, TILE_SLOTS), 1).astype(F32)
    owner = _bf((s_id >= off_col[:, :1]) & (s_id < off_col[:, :1] + pc_col[:, :1]))
    owner = jnp.concatenate([owner, jnp.zeros((128 - N_EXPERTS, TILE_SLOTS), BF16)], axis=0)
    rank_of_slot = _dot(rankp_t, owner)
    gate_of_slot = _dot(gates_t, owner)
    o16 = jnp.concatenate([off_row * (1.0 / RUN_ALIGN), jnp.zeros((16, 128 - N_EXPERTS), F32)],
                          axis=1).astype(BF16)
    off_of_slot = _dot(o16, owner)[:1] * RUN_ALIGN
    s_row = lax.broadcasted_iota(jnp.int32, (1, TILE_SLOTS), 1).astype(F32)
    weights = jnp.where(rank_of_slot + off_of_slot == s_row, gate_of_slot, 0.0).astype(BF16)

    x = x_ref[...]
    xb = x.astype(BF16)
    hs = _swiglu(xb, wgs_ref[...].astype(BF16), wus_ref[...].astype(BF16))
    shared = _dot(hs.astype(BF16), wds_ref[...].astype(BF16))

    _wait_rows(yg_ref, ys_scr.at[slot], sems.at[slot], tchunks_ref[t, 0] * RUN_ALIGN)
    routed = _dot(weights, ys_scr[slot])
    y = _layer_norm(ALPHA * x + (routed + shared), ln2g_ref[...], ln2b_ref[...])

    @pl.when(t < tiles_a)
    def _store_a():
        out_a_ref[...] = y

    @pl.when(t >= tiles_a)
    def _store_b():
        out_b_ref[...] = y


def _moe_combine(x1, gates, yg, pc, gbase, tchunks, wgs, wus, wds, ln2g, ln2b, rows_a):
    n_tok = x1.shape[0]
    tiles_a = rows_a // MOE_TILE
    smem = pl.BlockSpec(memory_space=pltpu.SMEM)
    return pl.pallas_call(
        functools.partial(_combine_kernel, tiles_a=tiles_a),
        grid=(n_tok // MOE_TILE,),
        in_specs=[
            smem, smem, smem,
            pl.BlockSpec((MOE_TILE, D_MODEL), lambda t: (t, 0)),
            pl.BlockSpec((N_EXPERTS, MOE_TILE), lambda t: (0, t)),
            pl.BlockSpec(memory_space=pl.ANY),
            _const_spec((D_MODEL, SHARED_DIM)),
            _const_spec((D_MODEL, SHARED_DIM)),
            _const_spec((SHARED_DIM, D_MODEL)),
            _const_spec((1, D_MODEL)),
            _const_spec((1, D_MODEL)),
        ],
        out_specs=[
            pl.BlockSpec((MOE_TILE, D_MODEL), lambda t: (jnp.minimum(t, tiles_a - 1), 0)),
            pl.BlockSpec((MOE_TILE, D_MODEL), lambda t: (jnp.maximum(t - tiles_a, 0), 0)),
        ],
        out_shape=[
            jax.ShapeDtypeStruct((rows_a, D_MODEL), F32),
            jax.ShapeDtypeStruct((n_tok - rows_a, D_MODEL), F32),
        ],
        scratch_shapes=[
            pltpu.VMEM((2, TILE_SLOTS, D_MODEL), BF16),
            pltpu.SemaphoreType.DMA((2,)),
        ],
        compiler_params=pltpu.CompilerParams(
            dimension_semantics=("arbitrary",), vmem_limit_bytes=VMEM_LIMIT_BYTES),
        name="moe_combine",
    )(pc, gbase, tchunks, x1, gates, yg, wgs, wus, wds, ln2g, ln2b)


def _moe(x1, rows_a, wrt, rbias, wg, wu, wd, wgs, wus, wds, ln2g, ln2b):
    n_tiles = x1.shape[0] // MOE_TILE
    gates, pc3 = _moe_route(x1, wrt, rbias)
    pc = pc3.reshape(n_tiles, 128)
    gbase, tchunks, ev = _moe_plan(pc)
    n_rows = _max_blocks(n_tiles) * EXPERT_BLOCK
    xg = _moe_dispatch(x1, gates, pc, gbase, tchunks, ev, n_rows)
    yg = _moe_experts(ev, xg, wg, wu, wd)
    return _moe_combine(x1, gates, yg, pc, gbase, tchunks, wgs, wus, wds, ln2g, ln2b, rows_a)


def kernel(x_prompt, x_sample, cache_win_k, cache_win_v, rel_bias_table, w_in, b_in, attn_sinks,
           sg_ln_g, sg_ln_b, sg_w, sg_b, w_proj_a, w_proj_b, w_out, ln1_g, ln1_b, w_router,
           router_bias, w_gate_e, w_up_e, w_down_e, w_gate_s, w_up_s, w_down_s, ln2_g, ln2_b):
    assert DEPTH == 1 and w_in.shape[0] == 1
    batch, seq, _ = x_prompt.shape
    nb, ds, _ = x_sample.shape
    win = cache_win_k.shape[2]

    w_in_bf = w_in[0].astype(BF16)
    wpa = w_proj_a[0].astype(BF16)
    wpb = w_proj_b[0].astype(BF16)
    wout = w_out[0].astype(BF16)
    b_in2 = b_in[0].reshape(1, IN_W)
    sgg = sg_ln_g[0].reshape(1, SG_WIDTH)
    sgb = sg_ln_b[0].reshape(1, SG_WIDTH)
    sgw = sg_w[0]
    sgbias = sg_b[0].reshape(N_SG_GROUPS, CHUNK, 1)
    ln1g = ln1_g[0].reshape(1, D_MODEL)
    ln1b = ln1_b[0].reshape(1, D_MODEL)
    sinks = attn_sinks[0]

    x1_p, wk_p, wv_p = _mixer_prompt(
        x_prompt, nb * ds, rel_bias_table, sinks, w_in_bf, b_in2, sgg, sgb, sgw, sgbias, wpa, wpb,
        wout, ln1g, ln1b)
    x1, wk_s, wv_s, cvs = _mixer_sample(
        x_sample, cache_win_k[0].reshape(nb, win, KV_W), cache_win_v[0].reshape(nb, win, KV_W),
        x1_p, rel_bias_table, sinks, w_in_bf, b_in2, sgg, sgb, sgw, sgbias, wpa, wpb, wout,
        ln1g, ln1b)

    y_p, y_s = _moe(x1, batch * seq, wrt=w_router[0].T,
                    rbias=router_bias[0].reshape(N_EXPERTS, 1),
                    wg=w_gate_e[0], wu=w_up_e[0], wd=w_down_e[0],
                    wgs=w_gate_s[0], wus=w_up_s[0], wds=w_down_s[0],
                    ln2g=ln2_g[0].reshape(1, D_MODEL), ln2b=ln2_b[0].reshape(1, D_MODEL))
    y_p = y_p.reshape(batch, seq, D_MODEL)
    y_s = y_s.reshape(nb, ds, D_MODEL)

    kv_shape = (1, -1, WINDOW, N_KV_HEADS, HEAD_DIM)
    return (y_p, y_s,
            wk_p.reshape(kv_shape), wv_p.reshape(kv_shape),
            wk_s.reshape(1, nb, win, N_KV_HEADS, HEAD_DIM),
            wv_s.reshape(1, nb, win, N_KV_HEADS, HEAD_DIM),
            cvs.reshape(1, nb, ds, N_SG_GROUPS, SG_GROUP_DIM))
```

```python
import functools
import math

import jax
import jax.numpy as jnp
import numpy as np
from jax import lax
from jax.experimental import pallas as pl
from jax.experimental.pallas import tpu as pltpu

F32 = jnp.float32
BF16 = jnp.bfloat16

D_MODEL = 1024
DEPTH = 1
HEAD_DIM = 64
N_Q_HEADS = 16
N_KV_HEADS = 2
Q_PER_KV = N_Q_HEADS // N_KV_HEADS
WINDOW = 128
ATTN_SCALE = HEAD_DIM ** -0.5
NEG_INF = -1e30
N_BUCKETS = 32
BUCKET_MAX_EXACT = 16
BUCKET_MAX_DIST = 128
CHUNK = 128
N_SG_GROUPS = 4
SG_GROUP_DIM = 128
SG_WIDTH = N_SG_GROUPS * SG_GROUP_DIM
Q_W = N_Q_HEADS * HEAD_DIM
KV_W = N_KV_HEADS * HEAD_DIM
Q_END = Q_W
K_END = Q_END + KV_W
V_END = K_END + KV_W
U_END = V_END + SG_WIDTH
VS_END = U_END + SG_WIDTH
GA_END = VS_END + D_MODEL
IN_W = GA_END + D_MODEL
N_EXPERTS = 64
TOP_K = 8
N_EXPERT_GROUPS = 8
EXPERTS_PER_GROUP = N_EXPERTS // N_EXPERT_GROUPS
TOPK_GROUPS = 4
EXPERT_DIM = 256
SHARED_DIM = 256
ROUTED_SCALE = 2.5
ALPHA = (2 * DEPTH) ** 0.25
LN_EPS = 1e-5

VMEM_LIMIT_BYTES = 56 * 1024 * 1024

PROMPT_STEP = 512
SAMPLE_SEQS_PER_STEP = 32
MOE_TILE = 256
RUN_ALIGN = 16
COPY_ROWS = 4 * RUN_ALIGN
EXPERT_BLOCK = 1024
TILE_SLOTS = -(-(MOE_TILE * TOP_K + N_EXPERTS * (RUN_ALIGN - 1)) // 512) * 512


def _t5_bucket_np(dist):
    d = np.maximum(dist, 0)
    ratio = np.maximum(d, 1).astype(np.float32) / np.float32(BUCKET_MAX_EXACT)
    large = BUCKET_MAX_EXACT + (
        np.log(ratio) / np.float32(math.log(BUCKET_MAX_DIST / BUCKET_MAX_EXACT))
        * np.float32(N_BUCKETS - BUCKET_MAX_EXACT)).astype(np.int32)
    large = np.minimum(large, N_BUCKETS - 1)
    return np.where(d < BUCKET_MAX_EXACT, d, large).astype(np.int32)


def _layer_norm(x, g, b):
    mu = jnp.mean(x, -1, keepdims=True)
    xc = x - mu
    var = jnp.mean(xc * xc, -1, keepdims=True)
    return xc * lax.rsqrt(var + LN_EPS) * g + b


def _gelu(x):
    return jax.nn.gelu(x)


def _dot(a, b):
    return jnp.dot(a, b, preferred_element_type=F32)


def _dot_nt(a, b):
    return lax.dot_general(a, b, (((1,), (1,)), ((), ())), preferred_element_type=F32)


def _project(xb, w_in_ref, b_in_ref, lo, hi):
    return _dot(xb, w_in_ref[:, lo:hi]) + b_in_ref[:, lo:hi]


def _expand_bias(bucket, table_ref, head):
    acc = jnp.zeros(bucket.shape, F32)
    for b in range(N_BUCKETS):
        acc = jnp.where(bucket == b, table_ref[b, head], acc)
    return acc


def _merge_and_norm(x, a_bf, s_bf, ga, gb, wpa_ref, wpb_ref, wout_ref, g_ref, b_ref):
    pa = _dot(a_bf, wpa_ref[...])
    pb = _dot(s_bf, wpb_ref[...])
    hpre = jax.nn.sigmoid(ga) * pa + jax.nn.sigmoid(gb) * pb
    h = _dot(hpre.astype(BF16), wout_ref[...])
    return _layer_norm(ALPHA * x + h, g_ref[...], b_ref[...])


def _mixer_prompt_kernel(table_ref, bucket_ref, sink_ref, x_ref, w_in_ref, b_in_ref,
                         sgg_ref, sgb_ref, sgw_ref, sgbias_ref, wpa_ref, wpb_ref, wout_ref,
                         ln1g_ref, ln1b_ref,
                         x1_ref, wk_ref, wv_ref,
                         bias_scr, tril_scr, kprev_scr, vprev_scr, a_scr, s_scr):
    b_idx = pl.program_id(0)
    n_idx = pl.program_id(1)
    n_blocks = PROMPT_STEP // WINDOW

    @pl.when((b_idx == 0) & (n_idx == 0))
    def _init_tables():
        bucket = bucket_ref[...]
        for h in range(N_Q_HEADS):
            g, r = divmod(h, Q_PER_KV)
            pair, parity = divmod(r, 2)
            bias_scr[g, pair, :, parity * 2 * WINDOW:(parity + 1) * 2 * WINDOW] = (
                _expand_bias(bucket, table_ref, h))
        row = lax.broadcasted_iota(jnp.int32, (CHUNK, CHUNK), 0)
        col = lax.broadcasted_iota(jnp.int32, (CHUNK, CHUNK), 1)
        for g in range(N_SG_GROUPS):
            tril_scr[g] = jnp.where(row >= col, sgw_ref[g], 0.0).astype(BF16)

    @pl.when(n_idx == 0)
    def _reset_carry():
        kprev_scr[...] = jnp.zeros_like(kprev_scr)
        vprev_scr[...] = jnp.zeros_like(vprev_scr)

    x = x_ref[...]
    xb = x.astype(BF16)
    q_bf = (_project(xb, w_in_ref, b_in_ref, 0, Q_END) * ATTN_SCALE).astype(BF16)
    k = _project(xb, w_in_ref, b_in_ref, Q_END, K_END)
    v = _project(xb, w_in_ref, b_in_ref, K_END, V_END)

    @pl.when(n_idx == pl.num_programs(1) - 1)
    def _emit_window():
        wk_ref[...] = k[PROMPT_STEP - WINDOW:, :]
        wv_ref[...] = v[PROMPT_STEP - WINDOW:, :]

    low = lax.broadcasted_iota(jnp.int32, (PROMPT_STEP, KV_W), 1) < HEAD_DIM

    def lane_halves(t):
        t_sw = pltpu.roll(t, HEAD_DIM, axis=1)
        zero = jnp.zeros_like(t)
        return [[jnp.where(low, t, zero).astype(BF16), jnp.where(low, zero, t_sw).astype(BF16)],
                [jnp.where(low, t_sw, zero).astype(BF16), jnp.where(low, zero, t).astype(BF16)]]

    k_half = lane_halves(k)
    v_half = lane_halves(v)

    row = lax.broadcasted_iota(jnp.int32, (WINDOW, 4 * WINDOW), 0)
    col = lax.broadcasted_iota(jnp.int32, (WINDOW, 4 * WINDOW), 1) % (2 * WINDOW)
    dist = row + WINDOW - col
    band_valid = (dist >= 0) & (dist <= WINDOW)
    first_valid = band_valid & ((col >= WINDOW) | (n_idx > 0))
    lane_low = lax.broadcasted_iota(jnp.int32, (Q_PER_KV // 2, WINDOW, 2 * HEAD_DIM), 2) < HEAD_DIM
    ones_rows = lax.broadcasted_iota(jnp.int32, (4 * WINDOW, 2 * HEAD_DIM), 0) < 2 * WINDOW
    ones_cols = lax.broadcasted_iota(jnp.int32, (4 * WINDOW, 2 * HEAD_DIM), 1) < HEAD_DIM
    sum_block = _bf(ones_rows == ones_cols)

    for j in range(n_blocks):
        r0, r1 = j * WINDOW, (j + 1) * WINDOW
        valid = first_valid if j == 0 else band_valid
        for g in range(N_KV_HEADS):
            def band(cur, prev_scr):
                parts = []
                for s in range(2):
                    prev = prev_scr[2 * g + s] if j == 0 else cur[g][s][r0 - WINDOW:r0]
                    parts += [prev, cur[g][s][r0:r1]]
                return jnp.concatenate(parts, axis=0)

            kd = band(k_half, kprev_scr)
            vd = jnp.concatenate([band(v_half, vprev_scr), sum_block], axis=1)
            q0 = g * Q_PER_KV * HEAD_DIM
            qp = jnp.concatenate(
                [q_bf[r0:r1, q0 + pr * 2 * HEAD_DIM:q0 + (pr + 1) * 2 * HEAD_DIM]
                 for pr in range(Q_PER_KV // 2)], axis=0)
            logits = _dot_nt(qp, kd).reshape(Q_PER_KV // 2, WINDOW, 4 * WINDOW)
            logits = jnp.where(valid[None], logits + bias_scr[g], NEG_INF)
            probs, sink_terms = [], []
            for s in range(2):
                l_s = logits[:, :, s * 2 * WINDOW:(s + 1) * 2 * WINDOW]
                sink = sink_ref[s, g]
                m = jnp.maximum(jnp.max(l_s, -1, keepdims=True), sink)
                probs.append(jnp.exp(l_s - m))
                sink_terms.append(jnp.broadcast_to(jnp.exp(sink - m), lane_low.shape))
            p = jnp.concatenate(probs, axis=-1).reshape(4 * WINDOW, 4 * WINDOW).astype(BF16)
            out = _dot(p, vd).reshape(Q_PER_KV // 2, WINDOW, 4 * HEAD_DIM)
            den = out[:, :, 2 * HEAD_DIM:] + jnp.where(lane_low, sink_terms[0], sink_terms[1])
            o = out[:, :, :2 * HEAD_DIM] / den
            for pr in range(Q_PER_KV // 2):
                a_scr[r0:r1, q0 + pr * 2 * HEAD_DIM:q0 + (pr + 1) * 2 * HEAD_DIM] = (
                    o[pr].astype(BF16))

    for g in range(N_KV_HEADS):
        for s in range(2):
            kprev_scr[2 * g + s] = k_half[g][s][PROMPT_STEP - WINDOW:]
            vprev_scr[2 * g + s] = v_half[g][s][PROMPT_STEP - WINDOW:]

    u = _gelu(_project(xb, w_in_ref, b_in_ref, V_END, U_END))
    vs = _gelu(_project(xb, w_in_ref, b_in_ref, U_END, VS_END))
    vs_bf = _layer_norm(vs, sgg_ref[...], sgb_ref[...]).astype(BF16)
    for j in range(n_blocks):
        r0, r1 = j * WINDOW, (j + 1) * WINDOW
        for g in range(N_SG_GROUPS):
            c0, c1 = g * SG_GROUP_DIM, (g + 1) * SG_GROUP_DIM
            sg = _dot(tril_scr[g], vs_bf[r0:r1, c0:c1]) + sgbias_ref[g]
            s_scr[r0:r1, c0:c1] = (u[r0:r1, c0:c1] * sg).astype(BF16)

    ga = _project(xb, w_in_ref, b_in_ref, VS_END, GA_END)
    gb = _project(xb, w_in_ref, b_in_ref, GA_END, IN_W)
    x1_ref[...] = _merge_and_norm(x, a_scr[...], s_scr[...], ga, gb, wpa_ref, wpb_ref,
                                  wout_ref, ln1g_ref, ln1b_ref)


def _const_spec(shape):
    zeros = (0,) * len(shape)
    return pl.BlockSpec(shape, lambda *_: zeros, pipeline_mode=pl.Buffered(1))


def _mixer_prompt(x, extra_rows, table, sinks, w_in_bf, b_in, sgg, sgb, sgw, sgbias, wpa, wpb,
                  wout, ln1g, ln1b):
    batch, seq, _ = x.shape
    n_steps = seq // PROMPT_STEP
    dist = np.arange(WINDOW)[:, None] + WINDOW - np.arange(2 * WINDOW)[None, :]
    bucket = jnp.asarray(_t5_bucket_np(dist))
    sink_pairs = jnp.transpose(sinks.reshape(N_KV_HEADS, Q_PER_KV // 2, 2), (2, 0, 1)).reshape(
        2, N_KV_HEADS, Q_PER_KV // 2, 1, 1)
    smem = pl.BlockSpec(memory_space=pltpu.SMEM)
    in_specs = [
        smem,
        _const_spec((WINDOW, 2 * WINDOW)),
        _const_spec((2, N_KV_HEADS, Q_PER_KV // 2, 1, 1)),
        pl.BlockSpec((None, PROMPT_STEP, D_MODEL), lambda b, n: (b, n, 0)),
        _const_spec((D_MODEL, IN_W)),
        _const_spec((1, IN_W)),
        _const_spec((1, SG_WIDTH)),
        _const_spec((1, SG_WIDTH)),
        _const_spec((N_SG_GROUPS, CHUNK, CHUNK)),
        _const_spec((N_SG_GROUPS, CHUNK, 1)),
        _const_spec((Q_W, D_MODEL)),
        _const_spec((SG_WIDTH, D_MODEL)),
        _const_spec((D_MODEL, D_MODEL)),
        _const_spec((1, D_MODEL)),
        _const_spec((1, D_MODEL)),
    ]
    out_specs = [
        pl.BlockSpec((PROMPT_STEP, D_MODEL), lambda b, n: (b * n_steps + n, 0)),
        pl.BlockSpec((None, WINDOW, KV_W), lambda b, n: (b, 0, 0)),
        pl.BlockSpec((None, WINDOW, KV_W), lambda b, n: (b, 0, 0)),
    ]
    out_shape = [
        jax.ShapeDtypeStruct((batch * seq + extra_rows, D_MODEL), F32),
        jax.ShapeDtypeStruct((batch, WINDOW, KV_W), F32),
        jax.ShapeDtypeStruct((batch, WINDOW, KV_W), F32),
    ]
    scratch = [
        pltpu.VMEM((N_KV_HEADS, Q_PER_KV // 2, WINDOW, 4 * WINDOW), F32),
        pltpu.VMEM((N_SG_GROUPS, CHUNK, CHUNK), BF16),
        pltpu.VMEM((2 * N_KV_HEADS, WINDOW, KV_W), BF16),
        pltpu.VMEM((2 * N_KV_HEADS, WINDOW, KV_W), BF16),
        pltpu.VMEM((PROMPT_STEP, Q_W), BF16),
        pltpu.VMEM((PROMPT_STEP, SG_WIDTH), BF16),
    ]
    return pl.pallas_call(
        _mixer_prompt_kernel,
        grid=(batch, n_steps),
        in_specs=in_specs,
        out_specs=out_specs,
        out_shape=out_shape,
        scratch_shapes=scratch,
        compiler_params=pltpu.CompilerParams(
            dimension_semantics=("arbitrary", "arbitrary"),
            vmem_limit_bytes=VMEM_LIMIT_BYTES),
        name="mixer_prompt",
    )(table, bucket, sink_pairs, x, w_in_bf, b_in, sgg, sgb, sgw, sgbias, wpa, wpb, wout,
      ln1g, ln1b)


def _mixer_sample_kernel(table_ref, bucket_c_ref, bucket_n_ref, sink_ref, x_ref, ck_ref, cv_ref,
                         w_in_ref, b_in_ref, sgg_ref, sgb_ref, sgw8_ref, sgbias8_ref,
                         wpa_ref, wpb_ref, wout_ref, ln1g_ref, ln1b_ref, x1_in_ref,
                         x1_ref, wk_ref, wv_ref, cvs_ref,
                         bias_c_scr, bias_n_scr, a_scr, s_scr):
    del x1_in_ref
    nseq, ds = SAMPLE_SEQS_PER_STEP, x_ref.shape[1]
    rows = nseq * ds
    win = ck_ref.shape[1]
    qrows = Q_PER_KV * ds

    @pl.when(pl.program_id(0) == 0)
    def _init_tables():
        bc = bucket_c_ref[...]
        bn = bucket_n_ref[...]
        for h in range(N_Q_HEADS):
            g, r = divmod(h, Q_PER_KV)
            bias_c_scr[g, r * ds:(r + 1) * ds, :] = _expand_bias(bc, table_ref, h)
            bias_n_scr[g, r * ds:(r + 1) * ds, :] = _expand_bias(bn, table_ref, h)

    x = x_ref[...].reshape(rows, D_MODEL)
    xb = x.astype(BF16)
    q = _project(xb, w_in_ref, b_in_ref, 0, Q_END) * ATTN_SCALE
    k = _project(xb, w_in_ref, b_in_ref, Q_END, K_END)
    v = _project(xb, w_in_ref, b_in_ref, K_END, V_END)
    ck = ck_ref[...]
    cv = cv_ref[...]
    wk_ref[:, :win - ds, :] = ck[:, ds:, :]
    wk_ref[:, win - ds:, :] = k.reshape(nseq, ds, KV_W)
    wv_ref[:, :win - ds, :] = cv[:, ds:, :]
    wv_ref[:, win - ds:, :] = v.reshape(nseq, ds, KV_W)
    nk = wk_ref[...]
    nv = wv_ref[...]

    q3 = q.reshape(nseq, ds, Q_W)
    t_q = lax.broadcasted_iota(jnp.int32, (qrows, win), 0) % ds
    col = lax.broadcasted_iota(jnp.int32, (qrows, win), 1)
    dist_c = t_q + win - col
    valid_c = (dist_c >= 0) & (dist_c <= WINDOW)
    dist_n = t_q - (col - (win - ds))
    valid_n = (col >= win - ds) & (dist_n >= 0) & (dist_n <= WINDOW)

    for g in range(N_KV_HEADS):
        c0, c1 = g * HEAD_DIM, (g + 1) * HEAD_DIM
        h0 = g * Q_PER_KV
        qs = jnp.concatenate(
            [q3[:, :, (h0 + r) * HEAD_DIM:(h0 + r + 1) * HEAD_DIM] for r in range(Q_PER_KV)],
            axis=1).astype(BF16)
        kc = ck[:, :, c0:c1].astype(BF16)
        vc = cv[:, :, c0:c1].astype(BF16)
        kn = nk[:, :, c0:c1].astype(BF16)
        vn = nv[:, :, c0:c1].astype(BF16)
        lc = jnp.einsum('bqd,bkd->bqk', qs, kc, preferred_element_type=F32)
        ln = jnp.einsum('bqd,bkd->bqk', qs, kn, preferred_element_type=F32)
        lc = jnp.where(valid_c[None], lc + bias_c_scr[g][None], NEG_INF)
        ln = jnp.where(valid_n[None], ln + bias_n_scr[g][None], NEG_INF)
        sink = sink_ref[g]
        m = jnp.maximum(jnp.maximum(jnp.max(lc, -1, keepdims=True),
                                    jnp.max(ln, -1, keepdims=True)), sink[None])
        pc = jnp.exp(lc - m)
        pn = jnp.exp(ln - m)
        den = (jnp.sum(pc, -1, keepdims=True) + jnp.sum(pn, -1, keepdims=True)
               + jnp.exp(sink[None] - m))
        o = (jnp.einsum('bqk,bkd->bqd', pc.astype(BF16), vc, preferred_element_type=F32)
             + jnp.einsum('bqk,bkd->bqd', pn.astype(BF16), vn, preferred_element_type=F32))
        o = o / den
        for r in range(Q_PER_KV):
            a_scr[:, :, (h0 + r) * HEAD_DIM:(h0 + r + 1) * HEAD_DIM] = (
                o[:, r * ds:(r + 1) * ds, :])

    u = _gelu(_project(xb, w_in_ref, b_in_ref, V_END, U_END))
    vs = _gelu(_project(xb, w_in_ref, b_in_ref, U_END, VS_END))
    vs_ln = _layer_norm(vs, sgg_ref[...], sgb_ref[...])
    cvs_ref[...] = vs_ln.reshape(nseq, ds, SG_WIDTH)
    vq = vs_ln.astype(BF16).astype(F32).reshape(nseq, ds, SG_WIDTH)
    u3 = u.reshape(nseq, ds, SG_WIDTH)
    i_row = lax.broadcasted_iota(jnp.int32, (ds, 1), 0)
    for g in range(N_SG_GROUPS):
        c0, c1 = g * SG_GROUP_DIM, (g + 1) * SG_GROUP_DIM
        acc = jnp.broadcast_to(sgbias8_ref[g][None], (nseq, ds, SG_GROUP_DIM))
        for j in range(ds):
            w_col = jnp.where(i_row >= j, sgw8_ref[g, j], 0.0)
            w_col = w_col.astype(BF16).astype(F32)
            acc = acc + w_col[None] * vq[:, j:j + 1, c0:c1]
        s_scr[:, :, c0:c1] = u3[:, :, c0:c1] * acc

    ga = _project(xb, w_in_ref, b_in_ref, VS_END, GA_END)
    gb = _project(xb, w_in_ref, b_in_ref, GA_END, IN_W)
    x1 = _merge_and_norm(x, a_scr[...].reshape(rows, Q_W).astype(BF16),
                         s_scr[...].reshape(rows, SG_WIDTH).astype(BF16),
                         ga, gb, wpa_ref, wpb_ref, wout_ref, ln1g_ref, ln1b_ref)
    x1_ref[...] = x1


def _mixer_sample(x, cache_k, cache_v, x1_all, table, sinks, w_in_bf, b_in, sgg, sgb, sgw,
                  sgbias, wpa, wpb, wout, ln1g, ln1b):
    nb, ds, _ = x.shape
    first_block = (x1_all.shape[0] - nb * ds) // (SAMPLE_SEQS_PER_STEP * ds)
    win = cache_k.shape[1]
    nseq = SAMPLE_SEQS_PER_STEP
    qrows = Q_PER_KV * ds
    t = np.arange(ds)[:, None]
    bucket_c = jnp.asarray(_t5_bucket_np(t + win - np.arange(win)[None, :]))
    bucket_n = jnp.asarray(_t5_bucket_np(t - (np.arange(win)[None, :] - (win - ds))))
    sink_rows = jnp.repeat(sinks.reshape(N_KV_HEADS, Q_PER_KV), ds, axis=1).reshape(
        N_KV_HEADS, qrows, 1)
    sgw8 = jnp.transpose(sgw[:, :ds, :ds], (0, 2, 1))[..., None]
    sgbias8 = sgbias[:, :ds, :]
    smem = pl.BlockSpec(memory_space=pltpu.SMEM)
    in_specs = [
        smem,
        _const_spec((ds, win)),
        _const_spec((ds, win)),
        _const_spec((N_KV_HEADS, qrows, 1)),
        pl.BlockSpec((nseq, ds, D_MODEL), lambda i: (i, 0, 0)),
        pl.BlockSpec((nseq, win, KV_W), lambda i: (i, 0, 0)),
        pl.BlockSpec((nseq, win, KV_W), lambda i: (i, 0, 0)),
        _const_spec((D_MODEL, IN_W)),
        _const_spec((1, IN_W)),
        _const_spec((1, SG_WIDTH)),
        _const_spec((1, SG_WIDTH)),
        _const_spec((N_SG_GROUPS, ds, ds, 1)),
        _const_spec((N_SG_GROUPS, ds, 1)),
        _const_spec((Q_W, D_MODEL)),
        _const_spec((SG_WIDTH, D_MODEL)),
        _const_spec((D_MODEL, D_MODEL)),
        _const_spec((1, D_MODEL)),
        _const_spec((1, D_MODEL)),
        pl.BlockSpec(memory_space=pl.ANY),
    ]
    out_specs = [
        pl.BlockSpec((nseq * ds, D_MODEL), lambda i: (first_block + i, 0)),
        pl.BlockSpec((nseq, win, KV_W), lambda i: (i, 0, 0)),
        pl.BlockSpec((nseq, win, KV_W), lambda i: (i, 0, 0)),
        pl.BlockSpec((nseq, ds, SG_WIDTH), lambda i: (i, 0, 0)),
    ]
    out_shape = [
        jax.ShapeDtypeStruct(x1_all.shape, F32),
        jax.ShapeDtypeStruct((nb, win, KV_W), F32),
        jax.ShapeDtypeStruct((nb, win, KV_W), F32),
        jax.ShapeDtypeStruct((nb, ds, SG_WIDTH), F32),
    ]
    scratch = [
        pltpu.VMEM((N_KV_HEADS, qrows, win), F32),
        pltpu.VMEM((N_KV_HEADS, qrows, win), F32),
        pltpu.VMEM((nseq, ds, Q_W), F32),
        pltpu.VMEM((nseq, ds, SG_WIDTH), F32),
    ]
    return pl.pallas_call(
        _mixer_sample_kernel,
        grid=(nb // nseq,),
        in_specs=in_specs,
        out_specs=out_specs,
        out_shape=out_shape,
        scratch_shapes=scratch,
        input_output_aliases={len(in_specs) - 1: 0},
        compiler_params=pltpu.CompilerParams(
            dimension_semantics=("arbitrary",),
            vmem_limit_bytes=VMEM_LIMIT_BYTES),
        name="mixer_sample",
    )(table, bucket_c, bucket_n, sink_rows, x, cache_k, cache_v, w_in_bf, b_in, sgg, sgb,
      sgw8, sgbias8, wpa, wpb, wout, ln1g, ln1b, x1_all)


def _route(xb, wrt_ref, rbias_ref):
    n = xb.shape[0]
    logits = _dot_nt(wrt_ref[...].astype(BF16), xb)
    scores = jax.nn.sigmoid(logits)
    sel = scores + rbias_ref[...]
    shape3 = (N_EXPERT_GROUPS, EXPERTS_PER_GROUP, n)
    scores3 = scores.reshape(shape3)
    sel3 = sel.reshape(shape3)
    i_in = lax.broadcasted_iota(jnp.int32, shape3, 1)
    g_id = lax.broadcasted_iota(jnp.int32, shape3, 0)
    e_id = g_id * EXPERTS_PER_GROUP + i_in
    neg = -jnp.inf

    m1 = jnp.max(sel3, axis=1, keepdims=True)
    first = jnp.min(jnp.where(sel3 == m1, i_in, EXPERTS_PER_GROUP), axis=1, keepdims=True)
    m2 = jnp.max(jnp.where(i_in == first, neg, sel3), axis=1, keepdims=True)
    gscore = m1 + m2

    gsel = jnp.zeros(gscore.shape, jnp.bool_)
    gid1 = lax.broadcasted_iota(jnp.int32, gscore.shape, 0)
    for _ in range(TOPK_GROUPS):
        m = jnp.max(gscore, axis=0, keepdims=True)
        pick = jnp.min(jnp.where(gscore == m, gid1, N_EXPERT_GROUPS), axis=0, keepdims=True)
        chosen = gid1 == pick
        gsel = gsel | chosen
        gscore = jnp.where(chosen, neg, gscore)
    val = jnp.where(gsel, sel3, NEG_INF)

    esel = jnp.zeros(shape3, jnp.bool_)
    for _ in range(TOP_K):
        m = jnp.max(jnp.max(val, axis=0, keepdims=True), axis=1, keepdims=True)
        cand = jnp.where(val == m, e_id, N_EXPERTS)
        pick = jnp.min(jnp.min(cand, axis=0, keepdims=True), axis=1, keepdims=True)
        chosen = e_id == pick
        esel = esel | chosen
        val = jnp.where(chosen, neg, val)
    w_sel = jnp.where(esel, scores3, 0.0)
    total = jnp.sum(jnp.sum(w_sel, axis=0, keepdims=True), axis=1, keepdims=True)
    gates = w_sel / total * ROUTED_SCALE
    return gates.reshape(N_EXPERTS, n)


def _swiglu(xb, wg, wu):
    return jax.nn.silu(_dot(xb, wg)) * _dot(xb, wu)


def _bf(mask):
    return jnp.where(mask, 1.0, 0.0).astype(BF16)


def _round_up_run(count):
    return jnp.ceil(count * (1.0 / RUN_ALIGN)) * RUN_ALIGN


def _run_layout_rows(sel_bf):
    n = sel_bf.shape[1]
    ones = jnp.ones((16, n), BF16)
    pc = _round_up_run(_dot_nt(ones, sel_bf))
    lower = lax.broadcasted_iota(jnp.int32, (N_EXPERTS, N_EXPERTS), 0)
    upper = lax.broadcasted_iota(jnp.int32, (N_EXPERTS, N_EXPERTS), 1)
    off = _dot(pc.astype(BF16), _bf(lower < upper))
    return pc, off


def _run_layout_cols(sel_f32):
    cnt = jnp.sum(sel_f32, axis=1, keepdims=True)
    pc = jnp.broadcast_to(_round_up_run(cnt), (N_EXPERTS, 128))
    row = lax.broadcasted_iota(jnp.int32, (N_EXPERTS, N_EXPERTS), 0)
    col = lax.broadcasted_iota(jnp.int32, (N_EXPERTS, N_EXPERTS), 1)
    off = _dot(_bf(col < row), pc.astype(BF16))
    return pc, off


def _rank_in_run(sel_bf):
    n = sel_bf.shape[1]
    m_id = lax.broadcasted_iota(jnp.int32, (n, n), 0)
    n_id = lax.broadcasted_iota(jnp.int32, (n, n), 1)
    return _dot(sel_bf, _bf(m_id < n_id))


def _route_kernel(x_ref, wrt_ref, rbias_ref, gates_ref, pc_ref):
    xb = x_ref[...].astype(BF16)
    gates = _route(xb, wrt_ref, rbias_ref)
    gates_ref[...] = gates
    sel_bf = _bf(gates > 0.0)
    pad = jnp.zeros((128 - N_EXPERTS, sel_bf.shape[1]), BF16)
    ones = jnp.ones((16, sel_bf.shape[1]), BF16)
    cnt = _dot_nt(ones, jnp.concatenate([sel_bf, pad], axis=0))
    pc_ref[...] = _round_up_run(cnt)[:1].astype(jnp.int32)


def _moe_route(x1, wrt, rbias):
    n_tok = x1.shape[0]
    n_tiles = n_tok // MOE_TILE
    return pl.pallas_call(
        _route_kernel,
        grid=(n_tiles,),
        in_specs=[
            pl.BlockSpec((MOE_TILE, D_MODEL), lambda t: (t, 0)),
            _const_spec((N_EXPERTS, D_MODEL)),
            _const_spec((N_EXPERTS, 1)),
        ],
        out_specs=[
            pl.BlockSpec((N_EXPERTS, MOE_TILE), lambda t: (0, t)),
            pl.BlockSpec((None, 1, 128), lambda t: (t, 0, 0)),
        ],
        out_shape=[
            jax.ShapeDtypeStruct((N_EXPERTS, n_tok), F32),
            jax.ShapeDtypeStruct((n_tiles, 1, 128), jnp.int32),
        ],
        compiler_params=pltpu.CompilerParams(dimension_semantics=("arbitrary",)),
        name="moe_route",
    )(x1, wrt, rbias)


PLAN_ESTART, PLAN_ENUM, PLAN_GAPSTART, PLAN_GAPN, PLAN_NBLOCKS = range(5)


def _plan_kernel(pc_ref, gbase_ref, tchunks_ref, ev_ref):
    n_tiles = pc_ref.shape[0]
    pad = jnp.zeros((128 - n_tiles, 128), F32)
    q = jnp.concatenate([pc_ref[...].astype(F32), pad], axis=0) * (1.0 / RUN_ALIGN)
    qb = q.astype(BF16)
    r = lax.broadcasted_iota(jnp.int32, (128, 128), 0)
    c = lax.broadcasted_iota(jnp.int32, (128, 128), 1)
    ones = jnp.ones((128, 128), BF16)
    before = _dot(_bf(c < r), qb) * RUN_ALIGN
    rows_e = _dot(ones[:16], qb) * RUN_ALIGN
    nb = jnp.ceil(rows_e * (1.0 / EXPERT_BLOCK))
    estart = _dot(nb.astype(BF16), _bf(r < c))
    g0 = estart * EXPERT_BLOCK
    gbase_ref[...] = (g0[:1] + before).astype(jnp.int32)
    tchunks_ref[...] = _dot(qb, ones).astype(jnp.int32)
    n_blocks = _dot(nb.astype(BF16), ones)
    row = lax.broadcasted_iota(jnp.int32, (8, 128), 0)
    ev = jnp.zeros((8, 128), F32)
    for i, val in ((PLAN_ESTART, estart), (PLAN_ENUM, nb), (PLAN_GAPSTART, g0 + rows_e),
                   (PLAN_GAPN, (nb * EXPERT_BLOCK - rows_e) * (1.0 / RUN_ALIGN)),
                   (PLAN_NBLOCKS, n_blocks)):
        ev = jnp.where(row == i, val[:8], ev)
    ev_ref[...] = ev.astype(jnp.int32)


def _max_blocks(n_tiles):
    per_tile = MOE_TILE * TOP_K + N_EXPERTS * (RUN_ALIGN - 1)
    rows = n_tiles * per_tile + N_EXPERTS * (EXPERT_BLOCK - RUN_ALIGN)
    return -(-rows // EXPERT_BLOCK)


def _moe_plan(pc):
    assert pc.shape[0] <= 128 and MOE_TILE // RUN_ALIGN < 256
    assert pc.shape[0] * MOE_TILE // EXPERT_BLOCK + 1 < 256
    i32 = jnp.int32
    return pl.pallas_call(
        _plan_kernel,
        out_shape=[
            jax.ShapeDtypeStruct((128, 128), i32),
            jax.ShapeDtypeStruct((128, 128), i32),
            jax.ShapeDtypeStruct((8, 128), i32),
        ],
        name="moe_plan",
    )(pc)


def _aligned(row):
    return row if isinstance(row, int) else pl.multiple_of(row, RUN_ALIGN)


def _rows_copy(src_ref, src_row, dst_ref, dst_row, rows, sem):
    return pltpu.make_async_copy(
        src_ref.at[pl.ds(_aligned(src_row), rows), :],
        dst_ref.at[pl.ds(_aligned(dst_row), rows), :],
        sem)


def _start_run(src_ref, src_row, dst_ref, dst_row, n_rows, sem, src_advances=True):
    def src_at(done):
        return src_row + done if src_advances else src_row

    def big_body(k, c):
        _rows_copy(src_ref, src_at(k * COPY_ROWS), dst_ref, dst_row + k * COPY_ROWS,
                   COPY_ROWS, sem).start()
        return c

    n_big = n_rows // COPY_ROWS
    lax.fori_loop(0, n_big, big_body, 0)
    done = n_big * COPY_ROWS
    mid = n_rows & (2 * RUN_ALIGN)

    @pl.when(mid != 0)
    def _mid():
        _rows_copy(src_ref, src_at(done), dst_ref, dst_row + done, 2 * RUN_ALIGN, sem).start()

    @pl.when((n_rows & RUN_ALIGN) != 0)
    def _small():
        _rows_copy(src_ref, src_at(done + mid), dst_ref, dst_row + done + mid, RUN_ALIGN,
                   sem).start()


def _wait_rows(src_ref, dst_ref, sem, n_rows):
    def big_body(i, c):
        _rows_copy(src_ref, 0, dst_ref, 0, COPY_ROWS, sem).wait()
        return c

    def small_body(i, c):
        _rows_copy(src_ref, 0, dst_ref, 0, RUN_ALIGN, sem).wait()
        return c

    lax.fori_loop(0, n_rows // COPY_ROWS, big_body, 0)
    lax.fori_loop(0, (n_rows % COPY_ROWS) // RUN_ALIGN, small_body, 0)


def _dispatch_kernel(pc_ref, gbase_ref, tchunks_ref, ev_ref,
                     x_ref, gates_ref, xg_ref, xs_scr, zero_scr, sems):
    t = pl.program_id(0)
    n_tiles = pl.num_programs(0)
    slot = t % 2

    @pl.when(t >= 2)
    def _drain_slot():
        _wait_rows(xs_scr.at[slot], xg_ref, sems.at[slot], tchunks_ref[t - 2, 0] * RUN_ALIGN)

    gates = gates_ref[...]
    sel = gates > 0.0
    sel_bf = _bf(sel)
    rankp = jnp.where(sel, _rank_in_run(sel_bf), -1.0).astype(BF16)
    pc_row, off_row = _run_layout_rows(sel_bf)
    pc_col, off_col = _run_layout_cols(jnp.where(sel, 1.0, 0.0))
    s_id = lax.broadcasted_iota(jnp.int32, (TILE_SLOTS, N_EXPERTS), 0).astype(F32)
    owner = _bf((s_id >= off_row[:1]) & (s_id < off_row[:1] + pc_row[:1]))
    rank_of_slot = _dot(owner, rankp)
    off_of_slot = _dot(owner, (off_col * (1.0 / RUN_ALIGN)).astype(BF16)) * RUN_ALIGN
    s_lane = lax.broadcasted_iota(jnp.int32, (TILE_SLOTS, 128), 0).astype(F32)
    onehot = jnp.concatenate(
        [_bf(rank_of_slot[:, c * 128:(c + 1) * 128] + off_of_slot == s_lane)
         for c in range(MOE_TILE // 128)], axis=1)
    xb = x_ref[...].astype(BF16)
    used_slots = tchunks_ref[t, 0] * RUN_ALIGN
    for c in range(TILE_SLOTS // 512):
        @pl.when(c * 512 < used_slots)
        def _sort_chunk(c=c):
            xs_scr[slot, c * 512:(c + 1) * 512, :] = _dot(
                onehot[c * 512:(c + 1) * 512, :], xb).astype(BF16)

    def expert_body(e, src):
        n = pc_ref[t, e]
        _start_run(xs_scr.at[slot], src, xg_ref, gbase_ref[t, e], n, sems.at[slot])
        return src + n

    lax.fori_loop(0, N_EXPERTS, expert_body, 0)

    @pl.when(t == n_tiles - 1)
    def _finish():
        zero_scr[...] = jnp.zeros_like(zero_scr)

        def gap_body(e, total):
            n = ev_ref[PLAN_GAPN, e] * RUN_ALIGN
            _start_run(zero_scr, 0, xg_ref, ev_ref[PLAN_GAPSTART, e], n, sems.at[2],
                       src_advances=False)
            return total + n

        gap_rows = lax.fori_loop(0, N_EXPERTS, gap_body, 0)
        _wait_rows(zero_scr, xg_ref, sems.at[2], gap_rows)
        _wait_rows(xs_scr.at[slot], xg_ref, sems.at[slot], tchunks_ref[t, 0] * RUN_ALIGN)

        @pl.when(t >= 1)
        def _drain_other():
            _wait_rows(xs_scr.at[1 - slot], xg_ref, sems.at[1 - slot],
                       tchunks_ref[t - 1, 0] * RUN_ALIGN)


def _moe_dispatch(x1, gates, pc, gbase, tchunks, ev, n_rows):
    n_tiles = x1.shape[0] // MOE_TILE
    smem = pl.BlockSpec(memory_space=pltpu.SMEM)
    return pl.pallas_call(
        _dispatch_kernel,
        grid=(n_tiles,),
        in_specs=[
            smem, smem, smem, smem,
            pl.BlockSpec((MOE_TILE, D_MODEL), lambda t: (t, 0)),
            pl.BlockSpec((N_EXPERTS, MOE_TILE), lambda t: (0, t)),
        ],
        out_specs=pl.BlockSpec(memory_space=pl.ANY),
        out_shape=jax.ShapeDtypeStruct((n_rows, D_MODEL), BF16),
        scratch_shapes=[
            pltpu.VMEM((2, TILE_SLOTS, D_MODEL), BF16),
            pltpu.VMEM((COPY_ROWS, D_MODEL), BF16),
            pltpu.SemaphoreType.DMA((3,)),
        ],
        compiler_params=pltpu.CompilerParams(
            dimension_semantics=("arbitrary",), vmem_limit_bytes=VMEM_LIMIT_BYTES),
        name="moe_dispatch",
    )(pc, gbase, tchunks, ev, x1, gates)


def _experts_kernel(ev_ref, wg_ref, wu_ref, wd_ref, xg_ref, yg_ref,
                    x_scr, y_scr, wg_scr, wu_scr, wd_scr, sem_in, sem_out):
    e = pl.program_id(0)
    n_blocks = ev_ref[PLAN_NBLOCKS, 0]
    first_block = ev_ref[PLAN_ESTART, e]
    blocks_here = ev_ref[PLAN_ENUM, e]

    def x_copy(g, slot):
        rows = pl.ds(pl.multiple_of(g * EXPERT_BLOCK, EXPERT_BLOCK), EXPERT_BLOCK)
        return pltpu.make_async_copy(xg_ref.at[rows, :], x_scr.at[slot], sem_in.at[slot])

    def y_copy(g, slot):
        rows = pl.ds(pl.multiple_of(g * EXPERT_BLOCK, EXPERT_BLOCK), EXPERT_BLOCK)
        return pltpu.make_async_copy(y_scr.at[slot], yg_ref.at[rows, :], sem_out.at[slot])

    @pl.when((e == 0) & (n_blocks > 0))
    def _first_fetch():
        x_copy(0, 0).start()

    @pl.when(blocks_here > 0)
    def _load_weights():
        wg_scr[...] = wg_ref[...].astype(BF16)
        wu_scr[...] = wu_ref[...].astype(BF16)
        wd_scr[...] = wd_ref[...].astype(BF16)

    def block_body(i, c):
        g = first_block + i
        slot = g % 2
        x_copy(g, slot).wait()

        @pl.when(g + 1 < n_blocks)
        def _prefetch():
            x_copy(g + 1, 1 - slot).start()

        @pl.when(g >= 2)
        def _free_out_buffer():
            y_copy(g - 2, slot).wait()

        h = _swiglu(x_scr[slot], wg_scr[...], wu_scr[...])
        y_scr[slot] = _dot(h.astype(BF16), wd_scr[...]).astype(BF16)
        y_copy(g, slot).start()
        return c

    lax.fori_loop(0, blocks_here, block_body, 0)

    @pl.when(e == pl.num_programs(0) - 1)
    def _drain():
        for back in (1, 2):
            @pl.when(n_blocks >= back)
            def _wait_out():
                g = n_blocks - back
                y_copy(g, g % 2).wait()


def _moe_experts(ev, xg, wg, wu, wd):
    def w_map(e, ev_ref):
        return (e, 0, 0)

    grid_spec = pltpu.PrefetchScalarGridSpec(
        num_scalar_prefetch=1,
        grid=(N_EXPERTS,),
        in_specs=[
            pl.BlockSpec((None, D_MODEL, EXPERT_DIM), w_map),
            pl.BlockSpec((None, D_MODEL, EXPERT_DIM), w_map),
            pl.BlockSpec((None, EXPERT_DIM, D_MODEL), w_map),
            pl.BlockSpec(memory_space=pl.ANY),
        ],
        out_specs=pl.BlockSpec(memory_space=pl.ANY),
        scratch_shapes=[
            pltpu.VMEM((2, EXPERT_BLOCK, D_MODEL), BF16),
            pltpu.VMEM((2, EXPERT_BLOCK, D_MODEL), BF16),
            pltpu.VMEM((D_MODEL, EXPERT_DIM), BF16),
            pltpu.VMEM((D_MODEL, EXPERT_DIM), BF16),
            pltpu.VMEM((EXPERT_DIM, D_MODEL), BF16),
            pltpu.SemaphoreType.DMA((2,)),
            pltpu.SemaphoreType.DMA((2,)),
        ],
    )
    return pl.pallas_call(
        _experts_kernel,
        grid_spec=grid_spec,
        out_shape=jax.ShapeDtypeStruct(xg.shape, BF16),
        compiler_params=pltpu.CompilerParams(
            dimension_semantics=("arbitrary",), vmem_limit_bytes=VMEM_LIMIT_BYTES),
        name="moe_experts",
    )(ev, wg, wu, wd, xg)


def _combine_kernel(pc_ref, gbase_ref, tchunks_ref,
                    x_ref, gates_ref, yg_ref, wgs_ref, wus_ref, wds_ref, ln2g_ref, ln2b_ref,
                    out_a_ref, out_b_ref, ys_scr, acc_scr, sems, *, tiles_a):
    t = pl.program_id(0)
    n_tiles = pl.num_programs(0)
    slot = t % 2

    def start_gather(tile, dst_slot):
        def expert_body(e, dst):
            n = pc_ref[tile, e]
            _start_run(yg_ref, gbase_ref[tile, e], ys_scr.at[dst_slot], dst, n,
                       sems.at[dst_slot])
            return dst + n

        lax.fori_loop(0, N_EXPERTS, expert_body, 0)

    @pl.when(t == 0)
    def _prime():
        ys_scr[...] = jnp.zeros_like(ys_scr)
        start_gather(0, 0)

    @pl.when(t + 1 < n_tiles)
    def _prefetch():
        start_gather(t + 1, 1 - slot)

    gates = gates_ref[...]
    sel = gates > 0.0
    sel_bf = _bf(sel)
    rankp = jnp.where(sel, _rank_in_run(sel_bf), -1.0)
    pc_row, off_row = _run_layout_rows(sel_bf)
    pc_col, off_col = _run_layout_cols(jnp.where(sel, 1.0, 0.0))
    pad = jnp.zeros((128 - N_EXPERTS, MOE_TILE), F32)
    gates_t = jnp.concatenate([gates, pad], axis=0).T.astype(BF16)
    rankp_t = jnp.concatenate([rankp, pad], axis=0).T.astype(BF16)
    s_id = lax.broadcasted_iota(jnp.int32, (N_EXPERTS, TILE_SLOTS), 1).astype(F32)
    owner = _bf((s_id >= off_col[:, :1]) & (s_id < off_col[:, :1] + pc_col[:, :1]))
    owner = jnp.concatenate([owner, jnp.zeros((128 - N_EXPERTS, TILE_SLOTS), BF16)], axis=0)
    rank_of_slot = _dot(rankp_t, owner)
    gate_of_slot = _dot(gates_t, owner)
    o16 = jnp.concatenate([off_row * (1.0 / RUN_ALIGN), jnp.zeros((16, 128 - N_EXPERTS), F32)],
                          axis=1).astype(BF16)
    off_of_slot = _dot(o16, owner)[:1] * RUN_ALIGN
    s_row = lax.broadcasted_iota(jnp.int32, (1, TILE_SLOTS), 1).astype(F32)
    weights = jnp.where(rank_of_slot + off_of_slot == s_row, gate_of_slot, 0.0).astype(BF16)

    x = x_ref[...]
    xb = x.astype(BF16)
    hs = _swiglu(xb, wgs_ref[...].astype(BF16), wus_ref[...].astype(BF16))
    shared = _dot(hs.astype(BF16), wds_ref[...].astype(BF16))

    _wait_rows(yg_ref, ys_scr.at[slot], sems.at[slot], tchunks_ref[t, 0] * RUN_ALIGN)
    acc_scr[...] = shared
    used_slots = tchunks_ref[t, 0] * RUN_ALIGN
    for c in range(TILE_SLOTS // 512):
        @pl.when(c * 512 < used_slots)
        def _sum_chunk(c=c):
            acc_scr[...] += _dot(weights[:, c * 512:(c + 1) * 512],
                                 ys_scr[slot, c * 512:(c + 1) * 512, :])
    y = _layer_norm(ALPHA * x + acc_scr[...], ln2g_ref[...], ln2b_ref[...])

    @pl.when(t < tiles_a)
    def _store_a():
        out_a_ref[...] = y

    @pl.when(t >= tiles_a)
    def _store_b():
        out_b_ref[...] = y


def _moe_combine(x1, gates, yg, pc, gbase, tchunks, wgs, wus, wds, ln2g, ln2b, rows_a):
    n_tok = x1.shape[0]
    tiles_a = rows_a // MOE_TILE
    smem = pl.BlockSpec(memory_space=pltpu.SMEM)
    return pl.pallas_call(
        functools.partial(_combine_kernel, tiles_a=tiles_a),
        grid=(n_tok // MOE_TILE,),
        in_specs=[
            smem, smem, smem,
            pl.BlockSpec((MOE_TILE, D_MODEL), lambda t: (t, 0)),
            pl.BlockSpec((N_EXPERTS, MOE_TILE), lambda t: (0, t)),
            pl.BlockSpec(memory_space=pl.ANY),
            _const_spec((D_MODEL, SHARED_DIM)),
            _const_spec((D_MODEL, SHARED_DIM)),
            _const_spec((SHARED_DIM, D_MODEL)),
            _const_spec((1, D_MODEL)),
            _const_spec((1, D_MODEL)),
        ],
        out_specs=[
            pl.BlockSpec((MOE_TILE, D_MODEL), lambda t: (jnp.minimum(t, tiles_a - 1), 0)),
            pl.BlockSpec((MOE_TILE, D_MODEL), lambda t: (jnp.maximum(t - tiles_a, 0), 0)),
        ],
        out_shape=[
            jax.ShapeDtypeStruct((rows_a, D_MODEL), F32),
            jax.ShapeDtypeStruct((n_tok - rows_a, D_MODEL), F32),
        ],
        scratch_shapes=[
            pltpu.VMEM((2, TILE_SLOTS, D_MODEL), BF16),
            pltpu.VMEM((MOE_TILE, D_MODEL), F32),
            pltpu.SemaphoreType.DMA((2,)),
        ],
        compiler_params=pltpu.CompilerParams(
            dimension_semantics=("arbitrary",), vmem_limit_bytes=VMEM_LIMIT_BYTES),
        name="moe_combine",
    )(pc, gbase, tchunks, x1, gates, yg, wgs, wus, wds, ln2g, ln2b)


def _moe(x1, rows_a, wrt, rbias, wg, wu, wd, wgs, wus, wds, ln2g, ln2b):
    n_tiles = x1.shape[0] // MOE_TILE
    gates, pc3 = _moe_route(x1, wrt, rbias)
    pc = pc3.reshape(n_tiles, 128)
    gbase, tchunks, ev = _moe_plan(pc)
    n_rows = _max_blocks(n_tiles) * EXPERT_BLOCK
    xg = _moe_dispatch(x1, gates, pc, gbase, tchunks, ev, n_rows)
    yg = _moe_experts(ev, xg, wg, wu, wd)
    return _moe_combine(x1, gates, yg, pc, gbase, tchunks, wgs, wus, wds, ln2g, ln2b, rows_a)


def kernel(x_prompt, x_sample, cache_win_k, cache_win_v, rel_bias_table, w_in, b_in, attn_sinks,
           sg_ln_g, sg_ln_b, sg_w, sg_b, w_proj_a, w_proj_b, w_out, ln1_g, ln1_b, w_router,
           router_bias, w_gate_e, w_up_e, w_down_e, w_gate_s, w_up_s, w_down_s, ln2_g, ln2_b):
    assert DEPTH == 1 and w_in.shape[0] == 1
    batch, seq, _ = x_prompt.shape
    nb, ds, _ = x_sample.shape
    win = cache_win_k.shape[2]

    w_in_bf = w_in[0].astype(BF16)
    wpa = w_proj_a[0].astype(BF16)
    wpb = w_proj_b[0].astype(BF16)
    wout = w_out[0].astype(BF16)
    b_in2 = b_in[0].reshape(1, IN_W)
    sgg = sg_ln_g[0].reshape(1, SG_WIDTH)
    sgb = sg_ln_b[0].reshape(1, SG_WIDTH)
    sgw = sg_w[0]
    sgbias = sg_b[0].reshape(N_SG_GROUPS, CHUNK, 1)
    ln1g = ln1_g[0].reshape(1, D_MODEL)
    ln1b = ln1_b[0].reshape(1, D_MODEL)
    sinks = attn_sinks[0]

    x1_p, wk_p, wv_p = _mixer_prompt(
        x_prompt, nb * ds, rel_bias_table, sinks, w_in_bf, b_in2, sgg, sgb, sgw, sgbias, wpa, wpb,
        wout, ln1g, ln1b)
    x1, wk_s, wv_s, cvs = _mixer_sample(
        x_sample, cache_win_k[0].reshape(nb, win, KV_W), cache_win_v[0].reshape(nb, win, KV_W),
        x1_p, rel_bias_table, sinks, w_in_bf, b_in2, sgg, sgb, sgw, sgbias, wpa, wpb, wout,
        ln1g, ln1b)

    y_p, y_s = _moe(x1, batch * seq, wrt=w_router[0].T,
                    rbias=router_bias[0].reshape(N_EXPERTS, 1),
                    wg=w_gate_e[0], wu=w_up_e[0], wd=w_down_e[0],
                    wgs=w_gate_s[0], wus=w_up_s[0], wds=w_down_s[0],
                    ln2g=ln2_g[0].reshape(1, D_MODEL), ln2b=ln2_b[0].reshape(1, D_MODEL))
    y_p = y_p.reshape(batch, seq, D_MODEL)
    y_s = y_s.reshape(nb, ds, D_MODEL)

    kv_shape = (1, -1, WINDOW, N_KV_HEADS, HEAD_DIM)
    return (y_p, y_s,
            wk_p.reshape(kv_shape), wv_p.reshape(kv_shape),
            wk_s.reshape(1, nb, win, N_KV_HEADS, HEAD_DIM),
            wv_s.reshape(1, nb, win, N_KV_HEADS, HEAD_DIM),
            cvs.reshape(1, nb, ds, N_SG_GROUPS, SG_GROUP_DIM))
```

```python
import functools
import math

import jax
import jax.numpy as jnp
import numpy as np
from jax import lax
from jax.experimental import pallas as pl
from jax.experimental.pallas import tpu as pltpu

F32 = jnp.float32
BF16 = jnp.bfloat16

D_MODEL = 1024
DEPTH = 1
HEAD_DIM = 64
N_Q_HEADS = 16
N_KV_HEADS = 2
Q_PER_KV = N_Q_HEADS // N_KV_HEADS
WINDOW = 128
ATTN_SCALE = HEAD_DIM ** -0.5
NEG_INF = -1e30
N_BUCKETS = 32
BUCKET_MAX_EXACT = 16
BUCKET_MAX_DIST = 128
CHUNK = 128
N_SG_GROUPS = 4
SG_GROUP_DIM = 128
SG_WIDTH = N_SG_GROUPS * SG_GROUP_DIM
Q_W = N_Q_HEADS * HEAD_DIM
KV_W = N_KV_HEADS * HEAD_DIM
Q_END = Q_W
K_END = Q_END + KV_W
V_END = K_END + KV_W
U_END = V_END + SG_WIDTH
VS_END = U_END + SG_WIDTH
GA_END = VS_END + D_MODEL
IN_W = GA_END + D_MODEL
N_EXPERTS = 64
TOP_K = 8
N_EXPERT_GROUPS = 8
EXPERTS_PER_GROUP = N_EXPERTS // N_EXPERT_GROUPS
TOPK_GROUPS = 4
EXPERT_DIM = 256
SHARED_DIM = 256
ROUTED_SCALE = 2.5
ALPHA = (2 * DEPTH) ** 0.25
LN_EPS = 1e-5

VMEM_LIMIT_BYTES = 56 * 1024 * 1024

PROMPT_STEP = 1024
SAMPLE_SEQS_PER_STEP = 32
MOE_TILE = 256
RUN_ALIGN = 16
COPY_ROWS = 4 * RUN_ALIGN
EXPERT_BLOCK = 1024
TILE_SLOTS = -(-(MOE_TILE * TOP_K + N_EXPERTS * (RUN_ALIGN - 1)) // 512) * 512


def _t5_bucket_np(dist):
    d = np.maximum(dist, 0)
    ratio = np.maximum(d, 1).astype(np.float32) / np.float32(BUCKET_MAX_EXACT)
    large = BUCKET_MAX_EXACT + (
        np.log(ratio) / np.float32(math.log(BUCKET_MAX_DIST / BUCKET_MAX_EXACT))
        * np.float32(N_BUCKETS - BUCKET_MAX_EXACT)).astype(np.int32)
    large = np.minimum(large, N_BUCKETS - 1)
    return np.where(d < BUCKET_MAX_EXACT, d, large).astype(np.int32)


def _layer_norm(x, g, b):
    mu = jnp.mean(x, -1, keepdims=True)
    xc = x - mu
    var = jnp.mean(xc * xc, -1, keepdims=True)
    return xc * lax.rsqrt(var + LN_EPS) * g + b


def _gelu(x):
    return jax.nn.gelu(x)


def _dot(a, b):
    return jnp.dot(a, b, preferred_element_type=F32)


def _dot_nt(a, b):
    return lax.dot_general(a, b, (((1,), (1,)), ((), ())), preferred_element_type=F32)


def _project(xb, w_in_ref, b_in_ref, lo, hi):
    return _dot(xb, w_in_ref[:, lo:hi]) + b_in_ref[:, lo:hi]


def _expand_bias(bucket, table_ref, head):
    acc = jnp.zeros(bucket.shape, F32)
    for b in range(N_BUCKETS):
        acc = jnp.where(bucket == b, table_ref[b, head], acc)
    return acc


def _merge_and_norm(x, a_bf, s_bf, ga, gb, wpa_ref, wpb_ref, wout_ref, g_ref, b_ref):
    pa = _dot(a_bf, wpa_ref[...])
    pb = _dot(s_bf, wpb_ref[...])
    hpre = jax.nn.sigmoid(ga) * pa + jax.nn.sigmoid(gb) * pb
    h = _dot(hpre.astype(BF16), wout_ref[...])
    return _layer_norm(ALPHA * x + h, g_ref[...], b_ref[...])


def _mixer_prompt_kernel(table_ref, bucket_ref, sink_ref, x_ref, w_in_ref, b_in_ref,
                         sgg_ref, sgb_ref, sgw_ref, sgbias_ref, wpa_ref, wpb_ref, wout_ref,
                         ln1g_ref, ln1b_ref,
                         x1_ref, wk_ref, wv_ref,
                         bias_scr, tril_scr, kprev_scr, vprev_scr, a_scr, s_scr):
    b_idx = pl.program_id(0)
    n_idx = pl.program_id(1)
    n_blocks = PROMPT_STEP // WINDOW

    @pl.when((b_idx == 0) & (n_idx == 0))
    def _init_tables():
        bucket = bucket_ref[...]
        for h in range(N_Q_HEADS):
            g, r = divmod(h, Q_PER_KV)
            pair, parity = divmod(r, 2)
            bias_scr[g, pair, :, parity * 2 * WINDOW:(parity + 1) * 2 * WINDOW] = (
                _expand_bias(bucket, table_ref, h))
        row = lax.broadcasted_iota(jnp.int32, (CHUNK, CHUNK), 0)
        col = lax.broadcasted_iota(jnp.int32, (CHUNK, CHUNK), 1)
        for g in range(N_SG_GROUPS):
            tril_scr[g] = jnp.where(row >= col, sgw_ref[g], 0.0).astype(BF16)

    @pl.when(n_idx == 0)
    def _reset_carry():
        kprev_scr[...] = jnp.zeros_like(kprev_scr)
        vprev_scr[...] = jnp.zeros_like(vprev_scr)

    x = x_ref[...]
    xb = x.astype(BF16)
    q_bf = (_project(xb, w_in_ref, b_in_ref, 0, Q_END) * ATTN_SCALE).astype(BF16)
    k = _project(xb, w_in_ref, b_in_ref, Q_END, K_END)
    v = _project(xb, w_in_ref, b_in_ref, K_END, V_END)

    @pl.when(n_idx == pl.num_programs(1) - 1)
    def _emit_window():
        wk_ref[...] = k[PROMPT_STEP - WINDOW:, :]
        wv_ref[...] = v[PROMPT_STEP - WINDOW:, :]

    low = lax.broadcasted_iota(jnp.int32, (PROMPT_STEP, KV_W), 1) < HEAD_DIM

    def lane_halves(t):
        t_sw = pltpu.roll(t, HEAD_DIM, axis=1)
        zero = jnp.zeros_like(t)
        return [[jnp.where(low, t, zero).astype(BF16), jnp.where(low, zero, t_sw).astype(BF16)],
                [jnp.where(low, t_sw, zero).astype(BF16), jnp.where(low, zero, t).astype(BF16)]]

    k_half = lane_halves(k)
    v_half = lane_halves(v)

    row = lax.broadcasted_iota(jnp.int32, (WINDOW, 4 * WINDOW), 0)
    col = lax.broadcasted_iota(jnp.int32, (WINDOW, 4 * WINDOW), 1) % (2 * WINDOW)
    dist = row + WINDOW - col
    band_valid = (dist >= 0) & (dist <= WINDOW)
    first_valid = band_valid & ((col >= WINDOW) | (n_idx > 0))
    lane_low = lax.broadcasted_iota(jnp.int32, (Q_PER_KV // 2, WINDOW, 2 * HEAD_DIM), 2) < HEAD_DIM
    ones_rows = lax.broadcasted_iota(jnp.int32, (4 * WINDOW, 2 * HEAD_DIM), 0) < 2 * WINDOW
    ones_cols = lax.broadcasted_iota(jnp.int32, (4 * WINDOW, 2 * HEAD_DIM), 1) < HEAD_DIM
    sum_block = _bf(ones_rows == ones_cols)

    for j in range(n_blocks):
        r0, r1 = j * WINDOW, (j + 1) * WINDOW
        valid = first_valid if j == 0 else band_valid
        for g in range(N_KV_HEADS):
            def band(cur, prev_scr):
                parts = []
                for s in range(2):
                    prev = prev_scr[2 * g + s] if j == 0 else cur[g][s][r0 - WINDOW:r0]
                    parts += [prev, cur[g][s][r0:r1]]
                return jnp.concatenate(parts, axis=0)

            kd = band(k_half, kprev_scr)
            vd = jnp.concatenate([band(v_half, vprev_scr), sum_block], axis=1)
            q0 = g * Q_PER_KV * HEAD_DIM
            qp = jnp.concatenate(
                [q_bf[r0:r1, q0 + pr * 2 * HEAD_DIM:q0 + (pr + 1) * 2 * HEAD_DIM]
                 for pr in range(Q_PER_KV // 2)], axis=0)
            logits = _dot_nt(qp, kd).reshape(Q_PER_KV // 2, WINDOW, 4 * WINDOW)
            logits = jnp.where(valid[None], logits + bias_scr[g], NEG_INF)
            probs, sink_terms = [], []
            for s in range(2):
                l_s = logits[:, :, s * 2 * WINDOW:(s + 1) * 2 * WINDOW]
                sink = sink_ref[s, g]
                m = jnp.maximum(jnp.max(l_s, -1, keepdims=True), sink)
                probs.append(jnp.exp(l_s - m))
                sink_terms.append(jnp.broadcast_to(jnp.exp(sink - m), lane_low.shape))
            p = jnp.concatenate(probs, axis=-1).reshape(4 * WINDOW, 4 * WINDOW).astype(BF16)
            out = _dot(p, vd).reshape(Q_PER_KV // 2, WINDOW, 4 * HEAD_DIM)
            den = out[:, :, 2 * HEAD_DIM:] + jnp.where(lane_low, sink_terms[0], sink_terms[1])
            o = out[:, :, :2 * HEAD_DIM] / den
            for pr in range(Q_PER_KV // 2):
                a_scr[r0:r1, q0 + pr * 2 * HEAD_DIM:q0 + (pr + 1) * 2 * HEAD_DIM] = (
                    o[pr].astype(BF16))

    for g in range(N_KV_HEADS):
        for s in range(2):
            kprev_scr[2 * g + s] = k_half[g][s][PROMPT_STEP - WINDOW:]
            vprev_scr[2 * g + s] = v_half[g][s][PROMPT_STEP - WINDOW:]

    u = _gelu(_project(xb, w_in_ref, b_in_ref, V_END, U_END))
    vs = _gelu(_project(xb, w_in_ref, b_in_ref, U_END, VS_END))
    vs_bf = _layer_norm(vs, sgg_ref[...], sgb_ref[...]).astype(BF16)
    for j in range(n_blocks):
        r0, r1 = j * WINDOW, (j + 1) * WINDOW
        for g in range(N_SG_GROUPS):
            c0, c1 = g * SG_GROUP_DIM, (g + 1) * SG_GROUP_DIM
            sg = _dot(tril_scr[g], vs_bf[r0:r1, c0:c1]) + sgbias_ref[g]
            s_scr[r0:r1, c0:c1] = (u[r0:r1, c0:c1] * sg).astype(BF16)

    ga = _project(xb, w_in_ref, b_in_ref, VS_END, GA_END)
    gb = _project(xb, w_in_ref, b_in_ref, GA_END, IN_W)
    x1_ref[...] = _merge_and_norm(x, a_scr[...], s_scr[...], ga, gb, wpa_ref, wpb_ref,
                                  wout_ref, ln1g_ref, ln1b_ref)


def _const_spec(shape):
    zeros = (0,) * len(shape)
    return pl.BlockSpec(shape, lambda *_: zeros, pipeline_mode=pl.Buffered(1))


def _mixer_prompt(x, extra_rows, table, sinks, w_in_bf, b_in, sgg, sgb, sgw, sgbias, wpa, wpb,
                  wout, ln1g, ln1b):
    batch, seq, _ = x.shape
    n_steps = seq // PROMPT_STEP
    dist = np.arange(WINDOW)[:, None] + WINDOW - np.arange(2 * WINDOW)[None, :]
    bucket = jnp.asarray(_t5_bucket_np(dist))
    sink_pairs = jnp.transpose(sinks.reshape(N_KV_HEADS, Q_PER_KV // 2, 2), (2, 0, 1)).reshape(
        2, N_KV_HEADS, Q_PER_KV // 2, 1, 1)
    smem = pl.BlockSpec(memory_space=pltpu.SMEM)
    in_specs = [
        smem,
        _const_spec((WINDOW, 2 * WINDOW)),
        _const_spec((2, N_KV_HEADS, Q_PER_KV // 2, 1, 1)),
        pl.BlockSpec((None, PROMPT_STEP, D_MODEL), lambda b, n: (b, n, 0),
                     pipeline_mode=pl.Buffered(1)),
        _const_spec((D_MODEL, IN_W)),
        _const_spec((1, IN_W)),
        _const_spec((1, SG_WIDTH)),
        _const_spec((1, SG_WIDTH)),
        _const_spec((N_SG_GROUPS, CHUNK, CHUNK)),
        _const_spec((N_SG_GROUPS, CHUNK, 1)),
        _const_spec((Q_W, D_MODEL)),
        _const_spec((SG_WIDTH, D_MODEL)),
        _const_spec((D_MODEL, D_MODEL)),
        _const_spec((1, D_MODEL)),
        _const_spec((1, D_MODEL)),
    ]
    out_specs = [
        pl.BlockSpec((PROMPT_STEP, D_MODEL), lambda b, n: (b * n_steps + n, 0),
                     pipeline_mode=pl.Buffered(1)),
        pl.BlockSpec((None, WINDOW, KV_W), lambda b, n: (b, 0, 0)),
        pl.BlockSpec((None, WINDOW, KV_W), lambda b, n: (b, 0, 0)),
    ]
    out_shape = [
        jax.ShapeDtypeStruct((batch * seq + extra_rows, D_MODEL), F32),
        jax.ShapeDtypeStruct((batch, WINDOW, KV_W), F32),
        jax.ShapeDtypeStruct((batch, WINDOW, KV_W), F32),
    ]
    scratch = [
        pltpu.VMEM((N_KV_HEADS, Q_PER_KV // 2, WINDOW, 4 * WINDOW), F32),
        pltpu.VMEM((N_SG_GROUPS, CHUNK, CHUNK), BF16),
        pltpu.VMEM((2 * N_KV_HEADS, WINDOW, KV_W), BF16),
        pltpu.VMEM((2 * N_KV_HEADS, WINDOW, KV_W), BF16),
        pltpu.VMEM((PROMPT_STEP, Q_W), BF16),
        pltpu.VMEM((PROMPT_STEP, SG_WIDTH), BF16),
    ]
    return pl.pallas_call(
        _mixer_prompt_kernel,
        grid=(batch, n_steps),
        in_specs=in_specs,
        out_specs=out_specs,
        out_shape=out_shape,
        scratch_shapes=scratch,
        compiler_params=pltpu.CompilerParams(
            dimension_semantics=("arbitrary", "arbitrary"),
            vmem_limit_bytes=VMEM_LIMIT_BYTES),
        name="mixer_prompt",
    )(table, bucket, sink_pairs, x, w_in_bf, b_in, sgg, sgb, sgw, sgbias, wpa, wpb, wout,
      ln1g, ln1b)


def _mixer_sample_kernel(table_ref, bucket_c_ref, bucket_n_ref, sink_ref, x_ref, ck_ref, cv_ref,
                         w_in_ref, b_in_ref, sgg_ref, sgb_ref, sgw8_ref, sgbias8_ref,
                         wpa_ref, wpb_ref, wout_ref, ln1g_ref, ln1b_ref, x1_in_ref,
                         x1_ref, wk_ref, wv_ref, cvs_ref,
                         bias_c_scr, bias_n_scr, a_scr, s_scr):
    del x1_in_ref
    nseq, ds = SAMPLE_SEQS_PER_STEP, x_ref.shape[1]
    rows = nseq * ds
    win = ck_ref.shape[1]
    qrows = Q_PER_KV * ds

    @pl.when(pl.program_id(0) == 0)
    def _init_tables():
        bc = bucket_c_ref[...]
        bn = bucket_n_ref[...]
        for h in range(N_Q_HEADS):
            g, r = divmod(h, Q_PER_KV)
            bias_c_scr[g, r * ds:(r + 1) * ds, :] = _expand_bias(bc, table_ref, h)
            bias_n_scr[g, r * ds:(r + 1) * ds, :] = _expand_bias(bn, table_ref, h)

    x = x_ref[...].reshape(rows, D_MODEL)
    xb = x.astype(BF16)
    q = _project(xb, w_in_ref, b_in_ref, 0, Q_END) * ATTN_SCALE
    k = _project(xb, w_in_ref, b_in_ref, Q_END, K_END)
    v = _project(xb, w_in_ref, b_in_ref, K_END, V_END)
    ck = ck_ref[...]
    cv = cv_ref[...]
    wk_ref[:, :win - ds, :] = ck[:, ds:, :]
    wk_ref[:, win - ds:, :] = k.reshape(nseq, ds, KV_W)
    wv_ref[:, :win - ds, :] = cv[:, ds:, :]
    wv_ref[:, win - ds:, :] = v.reshape(nseq, ds, KV_W)
    nk = wk_ref[...]
    nv = wv_ref[...]

    q3 = q.reshape(nseq, ds, Q_W)
    t_q = lax.broadcasted_iota(jnp.int32, (qrows, win), 0) % ds
    col = lax.broadcasted_iota(jnp.int32, (qrows, win), 1)
    dist_c = t_q + win - col
    valid_c = (dist_c >= 0) & (dist_c <= WINDOW)
    dist_n = t_q - (col - (win - ds))
    valid_n = (col >= win - ds) & (dist_n >= 0) & (dist_n <= WINDOW)

    for g in range(N_KV_HEADS):
        c0, c1 = g * HEAD_DIM, (g + 1) * HEAD_DIM
        h0 = g * Q_PER_KV
        qs = jnp.concatenate(
            [q3[:, :, (h0 + r) * HEAD_DIM:(h0 + r + 1) * HEAD_DIM] for r in range(Q_PER_KV)],
            axis=1).astype(BF16)
        kc = ck[:, :, c0:c1].astype(BF16)
        vc = cv[:, :, c0:c1].astype(BF16)
        kn = nk[:, :, c0:c1].astype(BF16)
        vn = nv[:, :, c0:c1].astype(BF16)
        lc = jnp.einsum('bqd,bkd->bqk', qs, kc, preferred_element_type=F32)
        ln = jnp.einsum('bqd,bkd->bqk', qs, kn, preferred_element_type=F32)
        lc = jnp.where(valid_c[None], lc + bias_c_scr[g][None], NEG_INF)
        ln = jnp.where(valid_n[None], ln + bias_n_scr[g][None], NEG_INF)
        sink = sink_ref[g]
        m = jnp.maximum(jnp.maximum(jnp.max(lc, -1, keepdims=True),
                                    jnp.max(ln, -1, keepdims=True)), sink[None])
        pc = jnp.exp(lc - m)
        pn = jnp.exp(ln - m)
        den = (jnp.sum(pc, -1, keepdims=True) + jnp.sum(pn, -1, keepdims=True)
               + jnp.exp(sink[None] - m))
        o = (jnp.einsum('bqk,bkd->bqd', pc.astype(BF16), vc, preferred_element_type=F32)
             + jnp.einsum('bqk,bkd->bqd', pn.astype(BF16), vn, preferred_element_type=F32))
        o = o / den
        for r in range(Q_PER_KV):
            a_scr[:, :, (h0 + r) * HEAD_DIM:(h0 + r + 1) * HEAD_DIM] = (
                o[:, r * ds:(r + 1) * ds, :])

    u = _gelu(_project(xb, w_in_ref, b_in_ref, V_END, U_END))
    vs = _gelu(_project(xb, w_in_ref, b_in_ref, U_END, VS_END))
    vs_ln = _layer_norm(vs, sgg_ref[...], sgb_ref[...])
    cvs_ref[...] = vs_ln.reshape(nseq, ds, SG_WIDTH)
    vq = vs_ln.astype(BF16).astype(F32).reshape(nseq, ds, SG_WIDTH)
    u3 = u.reshape(nseq, ds, SG_WIDTH)
    i_row = lax.broadcasted_iota(jnp.int32, (ds, 1), 0)
    for g in range(N_SG_GROUPS):
        c0, c1 = g * SG_GROUP_DIM, (g + 1) * SG_GROUP_DIM
        acc = jnp.broadcast_to(sgbias8_ref[g][None], (nseq, ds, SG_GROUP_DIM))
        for j in range(ds):
            w_col = jnp.where(i_row >= j, sgw8_ref[g, j], 0.0)
            w_col = w_col.astype(BF16).astype(F32)
            acc = acc + w_col[None] * vq[:, j:j + 1, c0:c1]
        s_scr[:, :, c0:c1] = u3[:, :, c0:c1] * acc

    ga = _project(xb, w_in_ref, b_in_ref, VS_END, GA_END)
    gb = _project(xb, w_in_ref, b_in_ref, GA_END, IN_W)
    x1 = _merge_and_norm(x, a_scr[...].reshape(rows, Q_W).astype(BF16),
                         s_scr[...].reshape(rows, SG_WIDTH).astype(BF16),
                         ga, gb, wpa_ref, wpb_ref, wout_ref, ln1g_ref, ln1b_ref)
    x1_ref[...] = x1


def _mixer_sample(x, cache_k, cache_v, x1_all, table, sinks, w_in_bf, b_in, sgg, sgb, sgw,
                  sgbias, wpa, wpb, wout, ln1g, ln1b):
    nb, ds, _ = x.shape
    first_block = (x1_all.shape[0] - nb * ds) // (SAMPLE_SEQS_PER_STEP * ds)
    win = cache_k.shape[1]
    nseq = SAMPLE_SEQS_PER_STEP
    qrows = Q_PER_KV * ds
    t = np.arange(ds)[:, None]
    bucket_c = jnp.asarray(_t5_bucket_np(t + win - np.arange(win)[None, :]))
    bucket_n = jnp.asarray(_t5_bucket_np(t - (np.arange(win)[None, :] - (win - ds))))
    sink_rows = jnp.repeat(sinks.reshape(N_KV_HEADS, Q_PER_KV), ds, axis=1).reshape(
        N_KV_HEADS, qrows, 1)
    sgw8 = jnp.transpose(sgw[:, :ds, :ds], (0, 2, 1))[..., None]
    sgbias8 = sgbias[:, :ds, :]
    smem = pl.BlockSpec(memory_space=pltpu.SMEM)
    in_specs = [
        smem,
        _const_spec((ds, win)),
        _const_spec((ds, win)),
        _const_spec((N_KV_HEADS, qrows, 1)),
        pl.BlockSpec((nseq, ds, D_MODEL), lambda i: (i, 0, 0)),
        pl.BlockSpec((nseq, win, KV_W), lambda i: (i, 0, 0)),
        pl.BlockSpec((nseq, win, KV_W), lambda i: (i, 0, 0)),
        _const_spec((D_MODEL, IN_W)),
        _const_spec((1, IN_W)),
        _const_spec((1, SG_WIDTH)),
        _const_spec((1, SG_WIDTH)),
        _const_spec((N_SG_GROUPS, ds, ds, 1)),
        _const_spec((N_SG_GROUPS, ds, 1)),
        _const_spec((Q_W, D_MODEL)),
        _const_spec((SG_WIDTH, D_MODEL)),
        _const_spec((D_MODEL, D_MODEL)),
        _const_spec((1, D_MODEL)),
        _const_spec((1, D_MODEL)),
        pl.BlockSpec(memory_space=pl.ANY),
    ]
    out_specs = [
        pl.BlockSpec((nseq * ds, D_MODEL), lambda i: (first_block + i, 0)),
        pl.BlockSpec((nseq, win, KV_W), lambda i: (i, 0, 0)),
        pl.BlockSpec((nseq, win, KV_W), lambda i: (i, 0, 0)),
        pl.BlockSpec((nseq, ds, SG_WIDTH), lambda i: (i, 0, 0)),
    ]
    out_shape = [
        jax.ShapeDtypeStruct(x1_all.shape, F32),
        jax.ShapeDtypeStruct((nb, win, KV_W), F32),
        jax.ShapeDtypeStruct((nb, win, KV_W), F32),
        jax.ShapeDtypeStruct((nb, ds, SG_WIDTH), F32),
    ]
    scratch = [
        pltpu.VMEM((N_KV_HEADS, qrows, win), F32),
        pltpu.VMEM((N_KV_HEADS, qrows, win), F32),
        pltpu.VMEM((nseq, ds, Q_W), F32),
        pltpu.VMEM((nseq, ds, SG_WIDTH), F32),
    ]
    return pl.pallas_call(
        _mixer_sample_kernel,
        grid=(nb // nseq,),
        in_specs=in_specs,
        out_specs=out_specs,
        out_shape=out_shape,
        scratch_shapes=scratch,
        input_output_aliases={len(in_specs) - 1: 0},
        compiler_params=pltpu.CompilerParams(
            dimension_semantics=("arbitrary",),
            vmem_limit_bytes=VMEM_LIMIT_BYTES),
        name="mixer_sample",
    )(table, bucket_c, bucket_n, sink_rows, x, cache_k, cache_v, w_in_bf, b_in, sgg, sgb,
      sgw8, sgbias8, wpa, wpb, wout, ln1g, ln1b, x1_all)


def _route(xb, wrt_ref, rbias_ref):
    n = xb.shape[0]
    logits = _dot_nt(wrt_ref[...].astype(BF16), xb)
    scores = jax.nn.sigmoid(logits)
    sel = scores + rbias_ref[...]
    shape3 = (N_EXPERT_GROUPS, EXPERTS_PER_GROUP, n)
    scores3 = scores.reshape(shape3)
    sel3 = sel.reshape(shape3)
    i_in = lax.broadcasted_iota(jnp.int32, shape3, 1)
    g_id = lax.broadcasted_iota(jnp.int32, shape3, 0)
    e_id = g_id * EXPERTS_PER_GROUP + i_in
    neg = -jnp.inf

    m1 = jnp.max(sel3, axis=1, keepdims=True)
    first = jnp.min(jnp.where(sel3 == m1, i_in, EXPERTS_PER_GROUP), axis=1, keepdims=True)
    m2 = jnp.max(jnp.where(i_in == first, neg, sel3), axis=1, keepdims=True)
    gscore = m1 + m2

    gsel = jnp.zeros(gscore.shape, jnp.bool_)
    gid1 = lax.broadcasted_iota(jnp.int32, gscore.shape, 0)
    for _ in range(TOPK_GROUPS):
        m = jnp.max(gscore, axis=0, keepdims=True)
        pick = jnp.min(jnp.where(gscore == m, gid1, N_EXPERT_GROUPS), axis=0, keepdims=True)
        chosen = gid1 == pick
        gsel = gsel | chosen
        gscore = jnp.where(chosen, neg, gscore)
    val = jnp.where(gsel, sel3, NEG_INF)

    esel = jnp.zeros(shape3, jnp.bool_)
    for _ in range(TOP_K):
        m = jnp.max(jnp.max(val, axis=0, keepdims=True), axis=1, keepdims=True)
        cand = jnp.where(val == m, e_id, N_EXPERTS)
        pick = jnp.min(jnp.min(cand, axis=0, keepdims=True), axis=1, keepdims=True)
        chosen = e_id == pick
        esel = esel | chosen
        val = jnp.where(chosen, neg, val)
    w_sel = jnp.where(esel, scores3, 0.0)
    total = jnp.sum(jnp.sum(w_sel, axis=0, keepdims=True), axis=1, keepdims=True)
    gates = w_sel / total * ROUTED_SCALE
    return gates.reshape(N_EXPERTS, n)


def _swiglu(xb, wg, wu):
    return jax.nn.silu(_dot(xb, wg)) * _dot(xb, wu)


def _bf(mask):
    return jnp.where(mask, 1.0, 0.0).astype(BF16)


def _round_up_run(count):
    return jnp.ceil(count * (1.0 / RUN_ALIGN)) * RUN_ALIGN


def _run_layout_rows(sel_bf):
    n = sel_bf.shape[1]
    ones = jnp.ones((16, n), BF16)
    pc = _round_up_run(_dot_nt(ones, sel_bf))
    lower = lax.broadcasted_iota(jnp.int32, (N_EXPERTS, N_EXPERTS), 0)
    upper = lax.broadcasted_iota(jnp.int32, (N_EXPERTS, N_EXPERTS), 1)
    off = _dot(pc.astype(BF16), _bf(lower < upper))
    return pc, off


def _run_layout_cols(sel_f32):
    cnt = jnp.sum(sel_f32, axis=1, keepdims=True)
    pc = jnp.broadcast_to(_round_up_run(cnt), (N_EXPERTS, 128))
    row = lax.broadcasted_iota(jnp.int32, (N_EXPERTS, N_EXPERTS), 0)
    col = lax.broadcasted_iota(jnp.int32, (N_EXPERTS, N_EXPERTS), 1)
    off = _dot(_bf(col < row), pc.astype(BF16))
    return pc, off


def _rank_in_run(sel_bf):
    n = sel_bf.shape[1]
    m_id = lax.broadcasted_iota(jnp.int32, (n, n), 0)
    n_id = lax.broadcasted_iota(jnp.int32, (n, n), 1)
    return _dot(sel_bf, _bf(m_id < n_id))


def _route_kernel(x_ref, wrt_ref, rbias_ref, gates_ref, pc_ref):
    xb = x_ref[...].astype(BF16)
    gates = _route(xb, wrt_ref, rbias_ref)
    gates_ref[...] = gates
    sel_bf = _bf(gates > 0.0)
    pad = jnp.zeros((128 - N_EXPERTS, sel_bf.shape[1]), BF16)
    ones = jnp.ones((16, sel_bf.shape[1]), BF16)
    cnt = _dot_nt(ones, jnp.concatenate([sel_bf, pad], axis=0))
    pc_ref[...] = _round_up_run(cnt)[:1].astype(jnp.int32)


def _moe_route(x1, wrt, rbias):
    n_tok = x1.shape[0]
    n_tiles = n_tok // MOE_TILE
    return pl.pallas_call(
        _route_kernel,
        grid=(n_tiles,),
        in_specs=[
            pl.BlockSpec((MOE_TILE, D_MODEL), lambda t: (t, 0)),
            _const_spec((N_EXPERTS, D_MODEL)),
            _const_spec((N_EXPERTS, 1)),
        ],
        out_specs=[
            pl.BlockSpec((N_EXPERTS, MOE_TILE), lambda t: (0, t)),
            pl.BlockSpec((None, 1, 128), lambda t: (t, 0, 0)),
        ],
        out_shape=[
            jax.ShapeDtypeStruct((N_EXPERTS, n_tok), F32),
            jax.ShapeDtypeStruct((n_tiles, 1, 128), jnp.int32),
        ],
        compiler_params=pltpu.CompilerParams(dimension_semantics=("arbitrary",)),
        name="moe_route",
    )(x1, wrt, rbias)


PLAN_ESTART, PLAN_ENUM, PLAN_GAPSTART, PLAN_GAPN, PLAN_NBLOCKS = range(5)


def _plan_kernel(pc_ref, gbase_ref, tchunks_ref, ev_ref):
    n_tiles = pc_ref.shape[0]
    pad = jnp.zeros((128 - n_tiles, 128), F32)
    q = jnp.concatenate([pc_ref[...].astype(F32), pad], axis=0) * (1.0 / RUN_ALIGN)
    qb = q.astype(BF16)
    r = lax.broadcasted_iota(jnp.int32, (128, 128), 0)
    c = lax.broadcasted_iota(jnp.int32, (128, 128), 1)
    ones = jnp.ones((128, 128), BF16)
    before = _dot(_bf(c < r), qb) * RUN_ALIGN
    rows_e = _dot(ones[:16], qb) * RUN_ALIGN
    nb = jnp.ceil(rows_e * (1.0 / EXPERT_BLOCK))
    estart = _dot(nb.astype(BF16), _bf(r < c))
    g0 = estart * EXPERT_BLOCK
    gbase_ref[...] = (g0[:1] + before).astype(jnp.int32)
    tchunks_ref[...] = _dot(qb, ones).astype(jnp.int32)
    n_blocks = _dot(nb.astype(BF16), ones)
    row = lax.broadcasted_iota(jnp.int32, (8, 128), 0)
    ev = jnp.zeros((8, 128), F32)
    for i, val in ((PLAN_ESTART, estart), (PLAN_ENUM, nb), (PLAN_GAPSTART, g0 + rows_e),
                   (PLAN_GAPN, (nb * EXPERT_BLOCK - rows_e) * (1.0 / RUN_ALIGN)),
                   (PLAN_NBLOCKS, n_blocks)):
        ev = jnp.where(row == i, val[:8], ev)
    ev_ref[...] = ev.astype(jnp.int32)


def _max_blocks(n_tiles):
    per_tile = MOE_TILE * TOP_K + N_EXPERTS * (RUN_ALIGN - 1)
    rows = n_tiles * per_tile + N_EXPERTS * (EXPERT_BLOCK - RUN_ALIGN)
    return -(-rows // EXPERT_BLOCK)


def _moe_plan(pc):
    assert pc.shape[0] <= 128 and MOE_TILE // RUN_ALIGN < 256
    assert pc.shape[0] * MOE_TILE // EXPERT_BLOCK + 1 < 256
    i32 = jnp.int32
    return pl.pallas_call(
        _plan_kernel,
        out_shape=[
            jax.ShapeDtypeStruct((128, 128), i32),
            jax.ShapeDtypeStruct((128, 128), i32),
            jax.ShapeDtypeStruct((8, 128), i32),
        ],
        name="moe_plan",
    )(pc)


def _aligned(row):
    return row if isinstance(row, int) else pl.multiple_of(row, RUN_ALIGN)


def _rows_copy(src_ref, src_row, dst_ref, dst_row, rows, sem):
    return pltpu.make_async_copy(
        src_ref.at[pl.ds(_aligned(src_row), rows), :],
        dst_ref.at[pl.ds(_aligned(dst_row), rows), :],
        sem)


def _start_run(src_ref, src_row, dst_ref, dst_row, n_rows, sem, src_advances=True):
    def src_at(done):
        return src_row + done if src_advances else src_row

    def big_body(k, c):
        _rows_copy(src_ref, src_at(k * COPY_ROWS), dst_ref, dst_row + k * COPY_ROWS,
                   COPY_ROWS, sem).start()
        return c

    n_big = n_rows // COPY_ROWS
    lax.fori_loop(0, n_big, big_body, 0)
    done = n_big * COPY_ROWS
    mid = n_rows & (2 * RUN_ALIGN)

    @pl.when(mid != 0)
    def _mid():
        _rows_copy(src_ref, src_at(done), dst_ref, dst_row + done, 2 * RUN_ALIGN, sem).start()

    @pl.when((n_rows & RUN_ALIGN) != 0)
    def _small():
        _rows_copy(src_ref, src_at(done + mid), dst_ref, dst_row + done + mid, RUN_ALIGN,
                   sem).start()


def _wait_rows(src_ref, dst_ref, sem, n_rows):
    def big_body(i, c):
        _rows_copy(src_ref, 0, dst_ref, 0, COPY_ROWS, sem).wait()
        return c

    def small_body(i, c):
        _rows_copy(src_ref, 0, dst_ref, 0, RUN_ALIGN, sem).wait()
        return c

    lax.fori_loop(0, n_rows // COPY_ROWS, big_body, 0)
    lax.fori_loop(0, (n_rows % COPY_ROWS) // RUN_ALIGN, small_body, 0)


def _dispatch_kernel(pc_ref, gbase_ref, tchunks_ref, ev_ref,
                     x_ref, gates_ref, xg_ref, xs_scr, zero_scr, sems):
    t = pl.program_id(0)
    n_tiles = pl.num_programs(0)
    slot = t % 2

    @pl.when(t >= 2)
    def _drain_slot():
        _wait_rows(xs_scr.at[slot], xg_ref, sems.at[slot], tchunks_ref[t - 2, 0] * RUN_ALIGN)

    gates = gates_ref[...]
    sel = gates > 0.0
    sel_bf = _bf(sel)
    rankp = jnp.where(sel, _rank_in_run(sel_bf), -1.0).astype(BF16)
    pc_row, off_row = _run_layout_rows(sel_bf)
    pc_col, off_col = _run_layout_cols(jnp.where(sel, 1.0, 0.0))
    s_id = lax.broadcasted_iota(jnp.int32, (TILE_SLOTS, N_EXPERTS), 0).astype(F32)
    owner = _bf((s_id >= off_row[:1]) & (s_id < off_row[:1] + pc_row[:1]))
    rank_of_slot = _dot(owner, rankp)
    off_of_slot = _dot(owner, (off_col * (1.0 / RUN_ALIGN)).astype(BF16)) * RUN_ALIGN
    s_lane = lax.broadcasted_iota(jnp.int32, (TILE_SLOTS, 128), 0).astype(F32)
    onehot = jnp.concatenate(
        [_bf(rank_of_slot[:, c * 128:(c + 1) * 128] + off_of_slot == s_lane)
         for c in range(MOE_TILE // 128)], axis=1)
    xb = x_ref[...].astype(BF16)
    for c in range(TILE_SLOTS // 512):
        xs_scr[slot, c * 512:(c + 1) * 512, :] = _dot(
            onehot[c * 512:(c + 1) * 512, :], xb).astype(BF16)

    def expert_body(e, src):
        n = pc_ref[t, e]
        _start_run(xs_scr.at[slot], src, xg_ref, gbase_ref[t, e], n, sems.at[slot])
        return src + n

    lax.fori_loop(0, N_EXPERTS, expert_body, 0)

    @pl.when(t == n_tiles - 1)
    def _finish():
        zero_scr[...] = jnp.zeros_like(zero_scr)

        def gap_body(e, total):
            n = ev_ref[PLAN_GAPN, e] * RUN_ALIGN
            _start_run(zero_scr, 0, xg_ref, ev_ref[PLAN_GAPSTART, e], n, sems.at[2],
                       src_advances=False)
            return total + n

        gap_rows = lax.fori_loop(0, N_EXPERTS, gap_body, 0)
        _wait_rows(zero_scr, xg_ref, sems.at[2], gap_rows)
        _wait_rows(xs_scr.at[slot], xg_ref, sems.at[slot], tchunks_ref[t, 0] * RUN_ALIGN)

        @pl.when(t >= 1)
        def _drain_other():
            _wait_rows(xs_scr.at[1 - slot], xg_ref, sems.at[1 - slot],
                       tchunks_ref[t - 1, 0] * RUN_ALIGN)


def _moe_dispatch(x1, gates, pc, gbase, tchunks, ev, n_rows):
    n_tiles = x1.shape[0] // MOE_TILE
    smem = pl.BlockSpec(memory_space=pltpu.SMEM)
    return pl.pallas_call(
        _dispatch_kernel,
        grid=(n_tiles,),
        in_specs=[
            smem, smem, smem, smem,
            pl.BlockSpec((MOE_TILE, D_MODEL), lambda t: (t, 0)),
            pl.BlockSpec((N_EXPERTS, MOE_TILE), lambda t: (0, t)),
        ],
        out_specs=pl.BlockSpec(memory_space=pl.ANY),
        out_shape=jax.ShapeDtypeStruct((n_rows, D_MODEL), BF16),
        scratch_shapes=[
            pltpu.VMEM((2, TILE_SLOTS, D_MODEL), BF16),
            pltpu.VMEM((COPY_ROWS, D_MODEL), BF16),
            pltpu.SemaphoreType.DMA((3,)),
        ],
        compiler_params=pltpu.CompilerParams(
            dimension_semantics=("arbitrary",), vmem_limit_bytes=VMEM_LIMIT_BYTES),
        name="moe_dispatch",
    )(pc, gbase, tchunks, ev, x1, gates)


def _experts_kernel(ev_ref, wg_ref, wu_ref, wd_ref, xg_ref, yg_ref,
                    x_scr, y_scr, wg_scr, wu_scr, wd_scr, sem_in, sem_out):
    e = pl.program_id(0)
    n_blocks = ev_ref[PLAN_NBLOCKS, 0]
    first_block = ev_ref[PLAN_ESTART, e]
    blocks_here = ev_ref[PLAN_ENUM, e]

    def x_copy(g, slot):
        rows = pl.ds(pl.multiple_of(g * EXPERT_BLOCK, EXPERT_BLOCK), EXPERT_BLOCK)
        return pltpu.make_async_copy(xg_ref.at[rows, :], x_scr.at[slot], sem_in.at[slot])

    def y_copy(g, slot):
        rows = pl.ds(pl.multiple_of(g * EXPERT_BLOCK, EXPERT_BLOCK), EXPERT_BLOCK)
        return pltpu.make_async_copy(y_scr.at[slot], yg_ref.at[rows, :], sem_out.at[slot])

    @pl.when((e == 0) & (n_blocks > 0))
    def _first_fetch():
        x_copy(0, 0).start()

    @pl.when(blocks_here > 0)
    def _load_weights():
        wg_scr[...] = wg_ref[...].astype(BF16)
        wu_scr[...] = wu_ref[...].astype(BF16)
        wd_scr[...] = wd_ref[...].astype(BF16)

    def block_body(i, c):
        g = first_block + i
        slot = g % 2
        x_copy(g, slot).wait()

        @pl.when(g + 1 < n_blocks)
        def _prefetch():
            x_copy(g + 1, 1 - slot).start()

        @pl.when(g >= 2)
        def _free_out_buffer():
            y_copy(g - 2, slot).wait()

        h = _swiglu(x_scr[slot], wg_scr[...], wu_scr[...])
        y_scr[slot] = _dot(h.astype(BF16), wd_scr[...]).astype(BF16)
        y_copy(g, slot).start()
        return c

    lax.fori_loop(0, blocks_here, block_body, 0)

    @pl.when(e == pl.num_programs(0) - 1)
    def _drain():
        for back in (1, 2):
            @pl.when(n_blocks >= back)
            def _wait_out():
                g = n_blocks - back
                y_copy(g, g % 2).wait()


def _moe_experts(ev, xg, wg, wu, wd):
    def w_map(e, ev_ref):
        return (e, 0, 0)

    grid_spec = pltpu.PrefetchScalarGridSpec(
        num_scalar_prefetch=1,
        grid=(N_EXPERTS,),
        in_specs=[
            pl.BlockSpec((None, D_MODEL, EXPERT_DIM), w_map),
            pl.BlockSpec((None, D_MODEL, EXPERT_DIM), w_map),
            pl.BlockSpec((None, EXPERT_DIM, D_MODEL), w_map),
            pl.BlockSpec(memory_space=pl.ANY),
        ],
        out_specs=pl.BlockSpec(memory_space=pl.ANY),
        scratch_shapes=[
            pltpu.VMEM((2, EXPERT_BLOCK, D_MODEL), BF16),
            pltpu.VMEM((2, EXPERT_BLOCK, D_MODEL), BF16),
            pltpu.VMEM((D_MODEL, EXPERT_DIM), BF16),
            pltpu.VMEM((D_MODEL, EXPERT_DIM), BF16),
            pltpu.VMEM((EXPERT_DIM, D_MODEL), BF16),
            pltpu.SemaphoreType.DMA((2,)),
            pltpu.SemaphoreType.DMA((2,)),
        ],
    )
    return pl.pallas_call(
        _experts_kernel,
        grid_spec=grid_spec,
        out_shape=jax.ShapeDtypeStruct(xg.shape, BF16),
        compiler_params=pltpu.CompilerParams(
            dimension_semantics=("arbitrary",), vmem_limit_bytes=VMEM_LIMIT_BYTES),
        name="moe_experts",
    )(ev, wg, wu, wd, xg)


def _combine_kernel(pc_ref, gbase_ref, tchunks_ref,
                    x_ref, gates_ref, yg_ref, wgs_ref, wus_ref, wds_ref, ln2g_ref, ln2b_ref,
                    out_a_ref, out_b_ref, ys_scr, sems, *, tiles_a):
    t = pl.program_id(0)
    n_tiles = pl.num_programs(0)
    slot = t % 2

    def start_gather(tile, dst_slot):
        def expert_body(e, dst):
            n = pc_ref[tile, e]
            _start_run(yg_ref, gbase_ref[tile, e], ys_scr.at[dst_slot], dst, n,
                       sems.at[dst_slot])
            return dst + n

        lax.fori_loop(0, N_EXPERTS, expert_body, 0)

    @pl.when(t == 0)
    def _prime():
        ys_scr[...] = jnp.zeros_like(ys_scr)
        start_gather(0, 0)

    @pl.when(t + 1 < n_tiles)
    def _prefetch():
        start_gather(t + 1, 1 - slot)

    gates = gates_ref[...]
    sel = gates > 0.0
    sel_bf = _bf(sel)
    rankp = jnp.where(sel, _rank_in_run(sel_bf), -1.0)
    pc_row, off_row = _run_layout_rows(sel_bf)
    pc_col, off_col = _run_layout_cols(jnp.where(sel, 1.0, 0.0))
    pad = jnp.zeros((128 - N_EXPERTS, MOE_TILE), F32)
    gates_t = jnp.concatenate([gates, pad], axis=0).T.astype(BF16)
    rankp_t = jnp.concatenate([rankp, pad], axis=0).T.astype(BF16)
    s_id = lax.broadcasted_iota(jnp.int32, (N_EXPERTS, TILE_SLOTS), 1).astype(F32)
    owner = _bf((s_id >= off_col[:, :1]) & (s_id < off_col[:, :1] + pc_col[:, :1]))
    owner = jnp.concatenate([owner, jnp.zeros((128 - N_EXPERTS, TILE_SLOTS), BF16)], axis=0)
    rank_of_slot = _dot(rankp_t, owner)
    gate_of_slot = _dot(gates_t, owner)
    o16 = jnp.concatenate([off_row * (1.0 / RUN_ALIGN), jnp.zeros((16, 128 - N_EXPERTS), F32)],
                          axis=1).astype(BF16)
    off_of_slot = _dot(o16, owner)[:1] * RUN_ALIGN
    s_row = lax.broadcasted_iota(jnp.int32, (1, TILE_SLOTS), 1).astype(F32)
    weights = jnp.where(rank_of_slot + off_of_slot == s_row, gate_of_slot, 0.0).astype(BF16)

    x = x_ref[...]
    xb = x.astype(BF16)
    hs = _swiglu(xb, wgs_ref[...].astype(BF16), wus_ref[...].astype(BF16))
    shared = _dot(hs.astype(BF16), wds_ref[...].astype(BF16))

    _wait_rows(yg_ref, ys_scr.at[slot], sems.at[slot], tchunks_ref[t, 0] * RUN_ALIGN)
    routed = _dot(weights, ys_scr[slot])
    y = _layer_norm(ALPHA * x + (routed + shared), ln2g_ref[...], ln2b_ref[...])

    @pl.when(t < tiles_a)
    def _store_a():
        out_a_ref[...] = y

    @pl.when(t >= tiles_a)
    def _store_b():
        out_b_ref[...] = y


def _moe_combine(x1, gates, yg, pc, gbase, tchunks, wgs, wus, wds, ln2g, ln2b, rows_a):
    n_tok = x1.shape[0]
    tiles_a = rows_a // MOE_TILE
    smem = pl.BlockSpec(memory_space=pltpu.SMEM)
    return pl.pallas_call(
        functools.partial(_combine_kernel, tiles_a=tiles_a),
        grid=(n_tok // MOE_TILE,),
        in_specs=[
            smem, smem, smem,
            pl.BlockSpec((MOE_TILE, D_MODEL), lambda t: (t, 0)),
            pl.BlockSpec((N_EXPERTS, MOE_TILE), lambda t: (0, t)),
            pl.BlockSpec(memory_space=pl.ANY),
            _const_spec((D_MODEL, SHARED_DIM)),
            _const_spec((D_MODEL, SHARED_DIM)),
            _const_spec((SHARED_DIM, D_MODEL)),
            _const_spec((1, D_MODEL)),
            _const_spec((1, D_MODEL)),
        ],
        out_specs=[
            pl.BlockSpec((MOE_TILE, D_MODEL), lambda t: (jnp.minimum(t, tiles_a - 1), 0)),
            pl.BlockSpec((MOE_TILE, D_MODEL), lambda t: (jnp.maximum(t - tiles_a, 0), 0)),
        ],
        out_shape=[
            jax.ShapeDtypeStruct((rows_a, D_MODEL), F32),
            jax.ShapeDtypeStruct((n_tok - rows_a, D_MODEL), F32),
        ],
        scratch_shapes=[
            pltpu.VMEM((2, TILE_SLOTS, D_MODEL), BF16),
            pltpu.SemaphoreType.DMA((2,)),
        ],
        compiler_params=pltpu.CompilerParams(
            dimension_semantics=("arbitrary",), vmem_limit_bytes=VMEM_LIMIT_BYTES),
        name="moe_combine",
    )(pc, gbase, tchunks, x1, gates, yg, wgs, wus, wds, ln2g, ln2b)


def _moe(x1, rows_a, wrt, rbias, wg, wu, wd, wgs, wus, wds, ln2g, ln2b):
    n_tiles = x1.shape[0] // MOE_TILE
    gates, pc3 = _moe_route(x1, wrt, rbias)
    pc = pc3.reshape(n_tiles, 128)
    gbase, tchunks, ev = _moe_plan(pc)
    n_rows = _max_blocks(n_tiles) * EXPERT_BLOCK
    xg = _moe_dispatch(x1, gates, pc, gbase, tchunks, ev, n_rows)
    yg = _moe_experts(ev, xg, wg, wu, wd)
    return _moe_combine(x1, gates, yg, pc, gbase, tchunks, wgs, wus, wds, ln2g, ln2b, rows_a)


def kernel(x_prompt, x_sample, cache_win_k, cache_win_v, rel_bias_table, w_in, b_in, attn_sinks,
           sg_ln_g, sg_ln_b, sg_w, sg_b, w_proj_a, w_proj_b, w_out, ln1_g, ln1_b, w_router,
           router_bias, w_gate_e, w_up_e, w_down_e, w_gate_s, w_up_s, w_down_s, ln2_g, ln2_b):
    assert DEPTH == 1 and w_in.shape[0] == 1
    batch, seq, _ = x_prompt.shape
    nb, ds, _ = x_sample.shape
    win = cache_win_k.shape[2]

    w_in_bf = w_in[0].astype(BF16)
    wpa = w_proj_a[0].astype(BF16)
    wpb = w_proj_b[0].astype(BF16)
    wout = w_out[0].astype(BF16)
    b_in2 = b_in[0].reshape(1, IN_W)
    sgg = sg_ln_g[0].reshape(1, SG_WIDTH)
    sgb = sg_ln_b[0].reshape(1, SG_WIDTH)
    sgw = sg_w[0]
    sgbias = sg_b[0].reshape(N_SG_GROUPS, CHUNK, 1)
    ln1g = ln1_g[0].reshape(1, D_MODEL)
    ln1b = ln1_b[0].reshape(1, D_MODEL)
    sinks = attn_sinks[0]

    x1_p, wk_p, wv_p = _mixer_prompt(
        x_prompt, nb * ds, rel_bias_table, sinks, w_in_bf, b_in2, sgg, sgb, sgw, sgbias, wpa, wpb,
        wout, ln1g, ln1b)
    x1, wk_s, wv_s, cvs = _mixer_sample(
        x_sample, cache_win_k[0].reshape(nb, win, KV_W), cache_win_v[0].reshape(nb, win, KV_W),
        x1_p, rel_bias_table, sinks, w_in_bf, b_in2, sgg, sgb, sgw, sgbias, wpa, wpb, wout,
        ln1g, ln1b)

    y_p, y_s = _moe(x1, batch * seq, wrt=w_router[0].T,
                    rbias=router_bias[0].reshape(N_EXPERTS, 1),
                    wg=w_gate_e[0], wu=w_up_e[0], wd=w_down_e[0],
                    wgs=w_gate_s[0], wus=w_up_s[0], wds=w_down_s[0],
                    ln2g=ln2_g[0].reshape(1, D_MODEL), ln2b=ln2_b[0].reshape(1, D_MODEL))
    y_p = y_p.reshape(batch, seq, D_MODEL)
    y_s = y_s.reshape(nb, ds, D_MODEL)

    kv_shape = (1, -1, WINDOW, N_KV_HEADS, HEAD_DIM)
    return (y_p, y_s,
            wk_p.reshape(kv_shape), wv_p.reshape(kv_shape),
            wk_s.reshape(1, nb, win, N_KV_HEADS, HEAD_DIM),
            wv_s.reshape(1, nb, win, N_KV_HEADS, HEAD_DIM),
            cvs.reshape(1, nb, ds, N_SG_GROUPS, SG_GROUP_DIM))
```

```python
import functools
import math

import jax
import jax.numpy as jnp
import numpy as np
from jax import lax
from jax.experimental import pallas as pl
from jax.experimental.pallas import tpu as pltpu

F32 = jnp.float32
BF16 = jnp.bfloat16

D_MODEL = 1024
DEPTH = 1
HEAD_DIM = 64
N_Q_HEADS = 16
N_KV_HEADS = 2
Q_PER_KV = N_Q_HEADS // N_KV_HEADS
WINDOW = 128
ATTN_SCALE = HEAD_DIM ** -0.5
NEG_INF = -1e30
N_BUCKETS = 32
BUCKET_MAX_EXACT = 16
BUCKET_MAX_DIST = 128
CHUNK = 128
N_SG_GROUPS = 4
SG_GROUP_DIM = 128
SG_WIDTH = N_SG_GROUPS * SG_GROUP_DIM
Q_W = N_Q_HEADS * HEAD_DIM
KV_W = N_KV_HEADS * HEAD_DIM
Q_END = Q_W
K_END = Q_END + KV_W
V_END = K_END + KV_W
U_END = V_END + SG_WIDTH
VS_END = U_END + SG_WIDTH
GA_END = VS_END + D_MODEL
IN_W = GA_END + D_MODEL
N_EXPERTS = 64
TOP_K = 8
N_EXPERT_GROUPS = 8
EXPERTS_PER_GROUP = N_EXPERTS // N_EXPERT_GROUPS
TOPK_GROUPS = 4
EXPERT_DIM = 256
SHARED_DIM = 256
ROUTED_SCALE = 2.5
ALPHA = (2 * DEPTH) ** 0.25
LN_EPS = 1e-5

VMEM_LIMIT_BYTES = 56 * 1024 * 1024

PROMPT_STEP = 512
SAMPLE_SEQS_PER_STEP = 32
MOE_TILE = 256
ROUTE_TILES = 2
RUN_ALIGN = 16
COPY_ROWS = 4 * RUN_ALIGN
EXPERT_BLOCK = 1024
TILE_SLOTS = -(-(MOE_TILE * TOP_K + N_EXPERTS * (RUN_ALIGN - 1)) // 512) * 512


def _t5_bucket_np(dist):
    d = np.maximum(dist, 0)
    ratio = np.maximum(d, 1).astype(np.float32) / np.float32(BUCKET_MAX_EXACT)
    large = BUCKET_MAX_EXACT + (
        np.log(ratio) / np.float32(math.log(BUCKET_MAX_DIST / BUCKET_MAX_EXACT))
        * np.float32(N_BUCKETS - BUCKET_MAX_EXACT)).astype(np.int32)
    large = np.minimum(large, N_BUCKETS - 1)
    return np.where(d < BUCKET_MAX_EXACT, d, large).astype(np.int32)


def _layer_norm(x, g, b):
    mu = jnp.mean(x, -1, keepdims=True)
    xc = x - mu
    var = jnp.mean(xc * xc, -1, keepdims=True)
    return xc * lax.rsqrt(var + LN_EPS) * g + b


def _gelu(x):
    return jax.nn.gelu(x)


def _dot(a, b):
    return jnp.dot(a, b, preferred_element_type=F32)


def _dot_nt(a, b):
    return lax.dot_general(a, b, (((1,), (1,)), ((), ())), preferred_element_type=F32)


def _project(xb, w_in_ref, b_in_ref, lo, hi):
    return _dot(xb, w_in_ref[:, lo:hi]) + b_in_ref[:, lo:hi]


def _expand_bias(bucket, table_ref, head):
    acc = jnp.zeros(bucket.shape, F32)
    for b in range(N_BUCKETS):
        acc = jnp.where(bucket == b, table_ref[b, head], acc)
    return acc


def _merge_and_norm(x, a_bf, s_bf, ga, gb, wpa_ref, wpb_ref, wout_ref, g_ref, b_ref):
    pa = _dot(a_bf, wpa_ref[...])
    pb = _dot(s_bf, wpb_ref[...])
    hpre = jax.nn.sigmoid(ga) * pa + jax.nn.sigmoid(gb) * pb
    h = _dot(hpre.astype(BF16), wout_ref[...])
    return _layer_norm(ALPHA * x + h, g_ref[...], b_ref[...])


def _mixer_prompt_kernel(table_ref, bucket_ref, sink_ref, x_ref, w_in_ref, b_in_ref,
                         sgg_ref, sgb_ref, sgw_ref, sgbias_ref, wpa_ref, wpb_ref, wout_ref,
                         ln1g_ref, ln1b_ref,
                         x1_ref, wk_ref, wv_ref,
                         bias_scr, tril_scr, kprev_scr, vprev_scr, a_scr, s_scr):
    b_idx = pl.program_id(0)
    n_idx = pl.program_id(1)
    n_blocks = PROMPT_STEP // WINDOW

    @pl.when((b_idx == 0) & (n_idx == 0))
    def _init_tables():
        bucket = bucket_ref[...]
        for h in range(N_Q_HEADS):
            g, r = divmod(h, Q_PER_KV)
            pair, parity = divmod(r, 2)
            bias_scr[g, pair, :, parity * 2 * WINDOW:(parity + 1) * 2 * WINDOW] = (
                _expand_bias(bucket, table_ref, h))
        row = lax.broadcasted_iota(jnp.int32, (CHUNK, CHUNK), 0)
        col = lax.broadcasted_iota(jnp.int32, (CHUNK, CHUNK), 1)
        for g in range(N_SG_GROUPS):
            tril_scr[g] = jnp.where(row >= col, sgw_ref[g], 0.0).astype(BF16)

    @pl.when(n_idx == 0)
    def _reset_carry():
        kprev_scr[...] = jnp.zeros_like(kprev_scr)
        vprev_scr[...] = jnp.zeros_like(vprev_scr)

    x = x_ref[...]
    xb = x.astype(BF16)
    q_bf = (_project(xb, w_in_ref, b_in_ref, 0, Q_END) * ATTN_SCALE).astype(BF16)
    k = _project(xb, w_in_ref, b_in_ref, Q_END, K_END)
    v = _project(xb, w_in_ref, b_in_ref, K_END, V_END)

    @pl.when(n_idx == pl.num_programs(1) - 1)
    def _emit_window():
        wk_ref[...] = k[PROMPT_STEP - WINDOW:, :]
        wv_ref[...] = v[PROMPT_STEP - WINDOW:, :]

    low = lax.broadcasted_iota(jnp.int32, (PROMPT_STEP, KV_W), 1) < HEAD_DIM

    def lane_halves(t):
        t_sw = pltpu.roll(t, HEAD_DIM, axis=1)
        zero = jnp.zeros_like(t)
        return [[jnp.where(low, t, zero).astype(BF16), jnp.where(low, zero, t_sw).astype(BF16)],
                [jnp.where(low, t_sw, zero).astype(BF16), jnp.where(low, zero, t).astype(BF16)]]

    k_half = lane_halves(k)
    v_half = lane_halves(v)

    row = lax.broadcasted_iota(jnp.int32, (WINDOW, 4 * WINDOW), 0)
    col = lax.broadcasted_iota(jnp.int32, (WINDOW, 4 * WINDOW), 1) % (2 * WINDOW)
    dist = row + WINDOW - col
    band_valid = (dist >= 0) & (dist <= WINDOW)
    first_valid = band_valid & ((col >= WINDOW) | (n_idx > 0))
    lane_low = lax.broadcasted_iota(jnp.int32, (Q_PER_KV // 2, WINDOW, 2 * HEAD_DIM), 2) < HEAD_DIM
    ones_rows = lax.broadcasted_iota(jnp.int32, (4 * WINDOW, 2 * HEAD_DIM), 0) < 2 * WINDOW
    ones_cols = lax.broadcasted_iota(jnp.int32, (4 * WINDOW, 2 * HEAD_DIM), 1) < HEAD_DIM
    sum_block = _bf(ones_rows == ones_cols)

    for j in range(n_blocks):
        r0, r1 = j * WINDOW, (j + 1) * WINDOW
        valid = first_valid if j == 0 else band_valid
        for g in range(N_KV_HEADS):
            def band(cur, prev_scr):
                parts = []
                for s in range(2):
                    prev = prev_scr[2 * g + s] if j == 0 else cur[g][s][r0 - WINDOW:r0]
                    parts += [prev, cur[g][s][r0:r1]]
                return jnp.concatenate(parts, axis=0)

            kd = band(k_half, kprev_scr)
            vd = jnp.concatenate([band(v_half, vprev_scr), sum_block], axis=1)
            q0 = g * Q_PER_KV * HEAD_DIM
            qp = jnp.concatenate(
                [q_bf[r0:r1, q0 + pr * 2 * HEAD_DIM:q0 + (pr + 1) * 2 * HEAD_DIM]
                 for pr in range(Q_PER_KV // 2)], axis=0)
            logits = _dot_nt(qp, kd).reshape(Q_PER_KV // 2, WINDOW, 4 * WINDOW)
            logits = jnp.where(valid[None], logits + bias_scr[g], NEG_INF)
            probs, sink_terms = [], []
            for s in range(2):
                l_s = logits[:, :, s * 2 * WINDOW:(s + 1) * 2 * WINDOW]
                sink = sink_ref[s, g]
                m = jnp.maximum(jnp.max(l_s, -1, keepdims=True), sink)
                probs.append(jnp.exp(l_s - m))
                sink_terms.append(jnp.broadcast_to(jnp.exp(sink - m), lane_low.shape))
            p = jnp.concatenate(probs, axis=-1).reshape(4 * WINDOW, 4 * WINDOW).astype(BF16)
            out = _dot(p, vd).reshape(Q_PER_KV // 2, WINDOW, 4 * HEAD_DIM)
            den = out[:, :, 2 * HEAD_DIM:] + jnp.where(lane_low, sink_terms[0], sink_terms[1])
            o = out[:, :, :2 * HEAD_DIM] / den
            for pr in range(Q_PER_KV // 2):
                a_scr[r0:r1, q0 + pr * 2 * HEAD_DIM:q0 + (pr + 1) * 2 * HEAD_DIM] = (
                    o[pr].astype(BF16))

    for g in range(N_KV_HEADS):
        for s in range(2):
            kprev_scr[2 * g + s] = k_half[g][s][PROMPT_STEP - WINDOW:]
            vprev_scr[2 * g + s] = v_half[g][s][PROMPT_STEP - WINDOW:]

    u = _gelu(_project(xb, w_in_ref, b_in_ref, V_END, U_END))
    vs = _gelu(_project(xb, w_in_ref, b_in_ref, U_END, VS_END))
    vs_bf = _layer_norm(vs, sgg_ref[...], sgb_ref[...]).astype(BF16)
    for j in range(n_blocks):
        r0, r1 = j * WINDOW, (j + 1) * WINDOW
        for g in range(N_SG_GROUPS):
            c0, c1 = g * SG_GROUP_DIM, (g + 1) * SG_GROUP_DIM
            sg = _dot(tril_scr[g], vs_bf[r0:r1, c0:c1]) + sgbias_ref[g]
            s_scr[r0:r1, c0:c1] = (u[r0:r1, c0:c1] * sg).astype(BF16)

    ga = _project(xb, w_in_ref, b_in_ref, VS_END, GA_END)
    gb = _project(xb, w_in_ref, b_in_ref, GA_END, IN_W)
    x1_ref[...] = _merge_and_norm(x, a_scr[...], s_scr[...], ga, gb, wpa_ref, wpb_ref,
                                  wout_ref, ln1g_ref, ln1b_ref)


def _const_spec(shape):
    zeros = (0,) * len(shape)
    return pl.BlockSpec(shape, lambda *_: zeros, pipeline_mode=pl.Buffered(1))


def _mixer_prompt(x, extra_rows, table, sinks, w_in_bf, b_in, sgg, sgb, sgw, sgbias, wpa, wpb,
                  wout, ln1g, ln1b):
    batch, seq, _ = x.shape
    n_steps = seq // PROMPT_STEP
    dist = np.arange(WINDOW)[:, None] + WINDOW - np.arange(2 * WINDOW)[None, :]
    bucket = jnp.asarray(_t5_bucket_np(dist))
    sink_pairs = jnp.transpose(sinks.reshape(N_KV_HEADS, Q_PER_KV // 2, 2), (2, 0, 1)).reshape(
        2, N_KV_HEADS, Q_PER_KV // 2, 1, 1)
    smem = pl.BlockSpec(memory_space=pltpu.SMEM)
    in_specs = [
        smem,
        _const_spec((WINDOW, 2 * WINDOW)),
        _const_spec((2, N_KV_HEADS, Q_PER_KV // 2, 1, 1)),
        pl.BlockSpec((None, PROMPT_STEP, D_MODEL), lambda b, n: (b, n, 0)),
        _const_spec((D_MODEL, IN_W)),
        _const_spec((1, IN_W)),
        _const_spec((1, SG_WIDTH)),
        _const_spec((1, SG_WIDTH)),
        _const_spec((N_SG_GROUPS, CHUNK, CHUNK)),
        _const_spec((N_SG_GROUPS, CHUNK, 1)),
        _const_spec((Q_W, D_MODEL)),
        _const_spec((SG_WIDTH, D_MODEL)),
        _const_spec((D_MODEL, D_MODEL)),
        _const_spec((1, D_MODEL)),
        _const_spec((1, D_MODEL)),
    ]
    out_specs = [
        pl.BlockSpec((PROMPT_STEP, D_MODEL), lambda b, n: (b * n_steps + n, 0)),
        pl.BlockSpec((None, WINDOW, KV_W), lambda b, n: (b, 0, 0)),
        pl.BlockSpec((None, WINDOW, KV_W), lambda b, n: (b, 0, 0)),
    ]
    out_shape = [
        jax.ShapeDtypeStruct((batch * seq + extra_rows, D_MODEL), F32),
        jax.ShapeDtypeStruct((batch, WINDOW, KV_W), F32),
        jax.ShapeDtypeStruct((batch, WINDOW, KV_W), F32),
    ]
    scratch = [
        pltpu.VMEM((N_KV_HEADS, Q_PER_KV // 2, WINDOW, 4 * WINDOW), F32),
        pltpu.VMEM((N_SG_GROUPS, CHUNK, CHUNK), BF16),
        pltpu.VMEM((2 * N_KV_HEADS, WINDOW, KV_W), BF16),
        pltpu.VMEM((2 * N_KV_HEADS, WINDOW, KV_W), BF16),
        pltpu.VMEM((PROMPT_STEP, Q_W), BF16),
        pltpu.VMEM((PROMPT_STEP, SG_WIDTH), BF16),
    ]
    return pl.pallas_call(
        _mixer_prompt_kernel,
        grid=(batch, n_steps),
        in_specs=in_specs,
        out_specs=out_specs,
        out_shape=out_shape,
        scratch_shapes=scratch,
        compiler_params=pltpu.CompilerParams(
            dimension_semantics=("arbitrary", "arbitrary"),
            vmem_limit_bytes=VMEM_LIMIT_BYTES),
        name="mixer_prompt",
    )(table, bucket, sink_pairs, x, w_in_bf, b_in, sgg, sgb, sgw, sgbias, wpa, wpb, wout,
      ln1g, ln1b)


def _mixer_sample_kernel(table_ref, bucket_c_ref, bucket_n_ref, sink_ref, x_ref, ck_ref, cv_ref,
                         w_in_ref, b_in_ref, sgg_ref, sgb_ref, sgw8_ref, sgbias8_ref,
                         wpa_ref, wpb_ref, wout_ref, ln1g_ref, ln1b_ref, x1_in_ref,
                         x1_ref, wk_ref, wv_ref, cvs_ref,
                         bias_c_scr, bias_n_scr, a_scr, s_scr):
    del x1_in_ref
    nseq, ds = SAMPLE_SEQS_PER_STEP, x_ref.shape[1]
    rows = nseq * ds
    win = ck_ref.shape[1]
    qrows = Q_PER_KV * ds

    @pl.when(pl.program_id(0) == 0)
    def _init_tables():
        bc = bucket_c_ref[...]
        bn = bucket_n_ref[...]
        for h in range(N_Q_HEADS):
            g, r = divmod(h, Q_PER_KV)
            bias_c_scr[g, r * ds:(r + 1) * ds, :] = _expand_bias(bc, table_ref, h)
            bias_n_scr[g, r * ds:(r + 1) * ds, :] = _expand_bias(bn, table_ref, h)

    x = x_ref[...].reshape(rows, D_MODEL)
    xb = x.astype(BF16)
    q = _project(xb, w_in_ref, b_in_ref, 0, Q_END) * ATTN_SCALE
    k = _project(xb, w_in_ref, b_in_ref, Q_END, K_END)
    v = _project(xb, w_in_ref, b_in_ref, K_END, V_END)
    ck = ck_ref[...]
    cv = cv_ref[...]
    wk_ref[:, :win - ds, :] = ck[:, ds:, :]
    wk_ref[:, win - ds:, :] = k.reshape(nseq, ds, KV_W)
    wv_ref[:, :win - ds, :] = cv[:, ds:, :]
    wv_ref[:, win - ds:, :] = v.reshape(nseq, ds, KV_W)
    nk = wk_ref[...]
    nv = wv_ref[...]

    q3 = q.reshape(nseq, ds, Q_W)
    t_q = lax.broadcasted_iota(jnp.int32, (qrows, win), 0) % ds
    col = lax.broadcasted_iota(jnp.int32, (qrows, win), 1)
    dist_c = t_q + win - col
    valid_c = (dist_c >= 0) & (dist_c <= WINDOW)
    dist_n = t_q - (col - (win - ds))
    valid_n = (col >= win - ds) & (dist_n >= 0) & (dist_n <= WINDOW)

    for g in range(N_KV_HEADS):
        c0, c1 = g * HEAD_DIM, (g + 1) * HEAD_DIM
        h0 = g * Q_PER_KV
        qs = jnp.concatenate(
            [q3[:, :, (h0 + r) * HEAD_DIM:(h0 + r + 1) * HEAD_DIM] for r in range(Q_PER_KV)],
            axis=1).astype(BF16)
        kc = ck[:, :, c0:c1].astype(BF16)
        vc = cv[:, :, c0:c1].astype(BF16)
        kn = nk[:, :, c0:c1].astype(BF16)
        vn = nv[:, :, c0:c1].astype(BF16)
        lc = jnp.einsum('bqd,bkd->bqk', qs, kc, preferred_element_type=F32)
        ln = jnp.einsum('bqd,bkd->bqk', qs, kn, preferred_element_type=F32)
        lc = jnp.where(valid_c[None], lc + bias_c_scr[g][None], NEG_INF)
        ln = jnp.where(valid_n[None], ln + bias_n_scr[g][None], NEG_INF)
        sink = sink_ref[g]
        m = jnp.maximum(jnp.maximum(jnp.max(lc, -1, keepdims=True),
                                    jnp.max(ln, -1, keepdims=True)), sink[None])
        pc = jnp.exp(lc - m)
        pn = jnp.exp(ln - m)
        den = (jnp.sum(pc, -1, keepdims=True) + jnp.sum(pn, -1, keepdims=True)
               + jnp.exp(sink[None] - m))
        o = (jnp.einsum('bqk,bkd->bqd', pc.astype(BF16), vc, preferred_element_type=F32)
             + jnp.einsum('bqk,bkd->bqd', pn.astype(BF16), vn, preferred_element_type=F32))
        o = o / den
        for r in range(Q_PER_KV):
            a_scr[:, :, (h0 + r) * HEAD_DIM:(h0 + r + 1) * HEAD_DIM] = (
                o[:, r * ds:(r + 1) * ds, :])

    u = _gelu(_project(xb, w_in_ref, b_in_ref, V_END, U_END))
    vs = _gelu(_project(xb, w_in_ref, b_in_ref, U_END, VS_END))
    vs_ln = _layer_norm(vs, sgg_ref[...], sgb_ref[...])
    cvs_ref[...] = vs_ln.reshape(nseq, ds, SG_WIDTH)
    vq = vs_ln.astype(BF16).astype(F32).reshape(nseq, ds, SG_WIDTH)
    u3 = u.reshape(nseq, ds, SG_WIDTH)
    i_row = lax.broadcasted_iota(jnp.int32, (ds, 1), 0)
    for g in range(N_SG_GROUPS):
        c0, c1 = g * SG_GROUP_DIM, (g + 1) * SG_GROUP_DIM
        acc = jnp.broadcast_to(sgbias8_ref[g][None], (nseq, ds, SG_GROUP_DIM))
        for j in range(ds):
            w_col = jnp.where(i_row >= j, sgw8_ref[g, j], 0.0)
            w_col = w_col.astype(BF16).astype(F32)
            acc = acc + w_col[None] * vq[:, j:j + 1, c0:c1]
        s_scr[:, :, c0:c1] = u3[:, :, c0:c1] * acc

    ga = _project(xb, w_in_ref, b_in_ref, VS_END, GA_END)
    gb = _project(xb, w_in_ref, b_in_ref, GA_END, IN_W)
    x1 = _merge_and_norm(x, a_scr[...].reshape(rows, Q_W).astype(BF16),
                         s_scr[...].reshape(rows, SG_WIDTH).astype(BF16),
                         ga, gb, wpa_ref, wpb_ref, wout_ref, ln1g_ref, ln1b_ref)
    x1_ref[...] = x1


def _mixer_sample(x, cache_k, cache_v, x1_all, table, sinks, w_in_bf, b_in, sgg, sgb, sgw,
                  sgbias, wpa, wpb, wout, ln1g, ln1b):
    nb, ds, _ = x.shape
    first_block = (x1_all.shape[0] - nb * ds) // (SAMPLE_SEQS_PER_STEP * ds)
    win = cache_k.shape[1]
    nseq = SAMPLE_SEQS_PER_STEP
    qrows = Q_PER_KV * ds
    t = np.arange(ds)[:, None]
    bucket_c = jnp.asarray(_t5_bucket_np(t + win - np.arange(win)[None, :]))
    bucket_n = jnp.asarray(_t5_bucket_np(t - (np.arange(win)[None, :] - (win - ds))))
    sink_rows = jnp.repeat(sinks.reshape(N_KV_HEADS, Q_PER_KV), ds, axis=1).reshape(
        N_KV_HEADS, qrows, 1)
    sgw8 = jnp.transpose(sgw[:, :ds, :ds], (0, 2, 1))[..., None]
    sgbias8 = sgbias[:, :ds, :]
    smem = pl.BlockSpec(memory_space=pltpu.SMEM)
    in_specs = [
        smem,
        _const_spec((ds, win)),
        _const_spec((ds, win)),
        _const_spec((N_KV_HEADS, qrows, 1)),
        pl.BlockSpec((nseq, ds, D_MODEL), lambda i: (i, 0, 0)),
        pl.BlockSpec((nseq, win, KV_W), lambda i: (i, 0, 0)),
        pl.BlockSpec((nseq, win, KV_W), lambda i: (i, 0, 0)),
        _const_spec((D_MODEL, IN_W)),
        _const_spec((1, IN_W)),
        _const_spec((1, SG_WIDTH)),
        _const_spec((1, SG_WIDTH)),
        _const_spec((N_SG_GROUPS, ds, ds, 1)),
        _const_spec((N_SG_GROUPS, ds, 1)),
        _const_spec((Q_W, D_MODEL)),
        _const_spec((SG_WIDTH, D_MODEL)),
        _const_spec((D_MODEL, D_MODEL)),
        _const_spec((1, D_MODEL)),
        _const_spec((1, D_MODEL)),
        pl.BlockSpec(memory_space=pl.ANY),
    ]
    out_specs = [
        pl.BlockSpec((nseq * ds, D_MODEL), lambda i: (first_block + i, 0)),
        pl.BlockSpec((nseq, win, KV_W), lambda i: (i, 0, 0)),
        pl.BlockSpec((nseq, win, KV_W), lambda i: (i, 0, 0)),
        pl.BlockSpec((nseq, ds, SG_WIDTH), lambda i: (i, 0, 0)),
    ]
    out_shape = [
        jax.ShapeDtypeStruct(x1_all.shape, F32),
        jax.ShapeDtypeStruct((nb, win, KV_W), F32),
        jax.ShapeDtypeStruct((nb, win, KV_W), F32),
        jax.ShapeDtypeStruct((nb, ds, SG_WIDTH), F32),
    ]
    scratch = [
        pltpu.VMEM((N_KV_HEADS, qrows, win), F32),
        pltpu.VMEM((N_KV_HEADS, qrows, win), F32),
        pltpu.VMEM((nseq, ds, Q_W), F32),
        pltpu.VMEM((nseq, ds, SG_WIDTH), F32),
    ]
    return pl.pallas_call(
        _mixer_sample_kernel,
        grid=(nb // nseq,),
        in_specs=in_specs,
        out_specs=out_specs,
        out_shape=out_shape,
        scratch_shapes=scratch,
        input_output_aliases={len(in_specs) - 1: 0},
        compiler_params=pltpu.CompilerParams(
            dimension_semantics=("arbitrary",),
            vmem_limit_bytes=VMEM_LIMIT_BYTES),
        name="mixer_sample",
    )(table, bucket_c, bucket_n, sink_rows, x, cache_k, cache_v, w_in_bf, b_in, sgg, sgb,
      sgw8, sgbias8, wpa, wpb, wout, ln1g, ln1b, x1_all)


def _route(xb, wrt_ref, rbias_ref):
    n = xb.shape[0]
    logits = _dot_nt(wrt_ref[...].astype(BF16), xb)
    scores = jax.nn.sigmoid(logits)
    sel = scores + rbias_ref[...]
    shape3 = (N_EXPERT_GROUPS, EXPERTS_PER_GROUP, n)
    scores3 = scores.reshape(shape3)
    sel3 = sel.reshape(shape3)
    i_in = lax.broadcasted_iota(jnp.int32, shape3, 1)
    g_id = lax.broadcasted_iota(jnp.int32, shape3, 0)
    e_id = g_id * EXPERTS_PER_GROUP + i_in
    neg = -jnp.inf

    m1 = jnp.max(sel3, axis=1, keepdims=True)
    first = jnp.min(jnp.where(sel3 == m1, i_in, EXPERTS_PER_GROUP), axis=1, keepdims=True)
    m2 = jnp.max(jnp.where(i_in == first, neg, sel3), axis=1, keepdims=True)
    gscore = m1 + m2

    gsel = jnp.zeros(gscore.shape, jnp.bool_)
    gid1 = lax.broadcasted_iota(jnp.int32, gscore.shape, 0)
    for _ in range(TOPK_GROUPS):
        m = jnp.max(gscore, axis=0, keepdims=True)
        pick = jnp.min(jnp.where(gscore == m, gid1, N_EXPERT_GROUPS), axis=0, keepdims=True)
        chosen = gid1 == pick
        gsel = gsel | chosen
        gscore = jnp.where(chosen, neg, gscore)
    val = jnp.where(gsel, sel3, NEG_INF)

    esel = jnp.zeros(shape3, jnp.bool_)
    for _ in range(TOP_K):
        m = jnp.max(jnp.max(val, axis=0, keepdims=True), axis=1, keepdims=True)
        cand = jnp.where(val == m, e_id, N_EXPERTS)
        pick = jnp.min(jnp.min(cand, axis=0, keepdims=True), axis=1, keepdims=True)
        chosen = e_id == pick
        esel = esel | chosen
        val = jnp.where(chosen, neg, val)
    w_sel = jnp.where(esel, scores3, 0.0)
    total = jnp.sum(jnp.sum(w_sel, axis=0, keepdims=True), axis=1, keepdims=True)
    gates = w_sel / total * ROUTED_SCALE
    return gates.reshape(N_EXPERTS, n)


def _swiglu(xb, wg, wu):
    return jax.nn.silu(_dot(xb, wg)) * _dot(xb, wu)


def _bf(mask):
    return jnp.where(mask, 1.0, 0.0).astype(BF16)


def _round_up_run(count):
    return jnp.ceil(count * (1.0 / RUN_ALIGN)) * RUN_ALIGN


def _run_layout_rows(sel_bf):
    n = sel_bf.shape[1]
    ones = jnp.ones((16, n), BF16)
    pc = _round_up_run(_dot_nt(ones, sel_bf))
    lower = lax.broadcasted_iota(jnp.int32, (N_EXPERTS, N_EXPERTS), 0)
    upper = lax.broadcasted_iota(jnp.int32, (N_EXPERTS, N_EXPERTS), 1)
    off = _dot(pc.astype(BF16), _bf(lower < upper))
    return pc, off


def _run_layout_cols(sel_f32):
    cnt = jnp.sum(sel_f32, axis=1, keepdims=True)
    pc = jnp.broadcast_to(_round_up_run(cnt), (N_EXPERTS, 128))
    row = lax.broadcasted_iota(jnp.int32, (N_EXPERTS, N_EXPERTS), 0)
    col = lax.broadcasted_iota(jnp.int32, (N_EXPERTS, N_EXPERTS), 1)
    off = _dot(_bf(col < row), pc.astype(BF16))
    return pc, off


def _rank_in_run(sel_bf):
    n = sel_bf.shape[1]
    m_id = lax.broadcasted_iota(jnp.int32, (n, n), 0)
    n_id = lax.broadcasted_iota(jnp.int32, (n, n), 1)
    return _dot(sel_bf, _bf(m_id < n_id))


def _route_kernel(x_ref, wrt_ref, rbias_ref, gates_ref, pc_ref):
    xb = x_ref[...].astype(BF16)
    gates = _route(xb, wrt_ref, rbias_ref)
    gates_ref[...] = gates
    pad = jnp.zeros((128 - N_EXPERTS, MOE_TILE), BF16)
    ones = jnp.ones((16, MOE_TILE), BF16)
    for i in range(ROUTE_TILES):
        sel_bf = _bf(gates[:, i * MOE_TILE:(i + 1) * MOE_TILE] > 0.0)
        cnt = _dot_nt(ones, jnp.concatenate([sel_bf, pad], axis=0))
        pc_ref[i] = _round_up_run(cnt)[:1].astype(jnp.int32)


def _moe_route(x1, wrt, rbias):
    n_tok = x1.shape[0]
    n_tiles = n_tok // MOE_TILE
    rows = ROUTE_TILES * MOE_TILE
    return pl.pallas_call(
        _route_kernel,
        grid=(n_tok // rows,),
        in_specs=[
            pl.BlockSpec((rows, D_MODEL), lambda t: (t, 0)),
            _const_spec((N_EXPERTS, D_MODEL)),
            _const_spec((N_EXPERTS, 1)),
        ],
        out_specs=[
            pl.BlockSpec((N_EXPERTS, rows), lambda t: (0, t)),
            pl.BlockSpec((ROUTE_TILES, 1, 128), lambda t: (t, 0, 0)),
        ],
        out_shape=[
            jax.ShapeDtypeStruct((N_EXPERTS, n_tok), F32),
            jax.ShapeDtypeStruct((n_tiles, 1, 128), jnp.int32),
        ],
        compiler_params=pltpu.CompilerParams(dimension_semantics=("arbitrary",)),
        name="moe_route",
    )(x1, wrt, rbias)


PLAN_ESTART, PLAN_ENUM, PLAN_GAPSTART, PLAN_GAPN, PLAN_NBLOCKS = range(5)


def _plan_kernel(pc_ref, gbase_ref, tchunks_ref, ev_ref):
    n_tiles = pc_ref.shape[0]
    pad = jnp.zeros((128 - n_tiles, 128), F32)
    q = jnp.concatenate([pc_ref[...].astype(F32), pad], axis=0) * (1.0 / RUN_ALIGN)
    qb = q.astype(BF16)
    r = lax.broadcasted_iota(jnp.int32, (128, 128), 0)
    c = lax.broadcasted_iota(jnp.int32, (128, 128), 1)
    ones = jnp.ones((128, 128), BF16)
    before = _dot(_bf(c < r), qb) * RUN_ALIGN
    rows_e = _dot(ones[:16], qb) * RUN_ALIGN
    nb = jnp.ceil(rows_e * (1.0 / EXPERT_BLOCK))
    estart = _dot(nb.astype(BF16), _bf(r < c))
    g0 = estart * EXPERT_BLOCK
    gbase_ref[...] = (g0[:1] + before).astype(jnp.int32)
    tchunks_ref[...] = _dot(qb, ones).astype(jnp.int32)
    n_blocks = _dot(nb.astype(BF16), ones)
    row = lax.broadcasted_iota(jnp.int32, (8, 128), 0)
    ev = jnp.zeros((8, 128), F32)
    for i, val in ((PLAN_ESTART, estart), (PLAN_ENUM, nb), (PLAN_GAPSTART, g0 + rows_e),
                   (PLAN_GAPN, (nb * EXPERT_BLOCK - rows_e) * (1.0 / RUN_ALIGN)),
                   (PLAN_NBLOCKS, n_blocks)):
        ev = jnp.where(row == i, val[:8], ev)
    ev_ref[...] = ev.astype(jnp.int32)


def _max_blocks(n_tiles):
    per_tile = MOE_TILE * TOP_K + N_EXPERTS * (RUN_ALIGN - 1)
    rows = n_tiles * per_tile + N_EXPERTS * (EXPERT_BLOCK - RUN_ALIGN)
    return -(-rows // EXPERT_BLOCK)


def _moe_plan(pc):
    assert pc.shape[0] <= 128 and MOE_TILE // RUN_ALIGN < 256
    assert pc.shape[0] * MOE_TILE // EXPERT_BLOCK + 1 < 256
    i32 = jnp.int32
    return pl.pallas_call(
        _plan_kernel,
        out_shape=[
            jax.ShapeDtypeStruct((128, 128), i32),
            jax.ShapeDtypeStruct((128, 128), i32),
            jax.ShapeDtypeStruct((8, 128), i32),
        ],
        name="moe_plan",
    )(pc)


def _aligned(row):
    return row if isinstance(row, int) else pl.multiple_of(row, RUN_ALIGN)


def _rows_copy(src_ref, src_row, dst_ref, dst_row, rows, sem):
    return pltpu.make_async_copy(
        src_ref.at[pl.ds(_aligned(src_row), rows), :],
        dst_ref.at[pl.ds(_aligned(dst_row), rows), :],
        sem)


def _start_run(src_ref, src_row, dst_ref, dst_row, n_rows, sem, src_advances=True):
    def src_at(done):
        return src_row + done if src_advances else src_row

    def big_body(k, c):
        _rows_copy(src_ref, src_at(k * COPY_ROWS), dst_ref, dst_row + k * COPY_ROWS,
                   COPY_ROWS, sem).start()
        return c

    n_big = n_rows // COPY_ROWS
    lax.fori_loop(0, n_big, big_body, 0)
    done = n_big * COPY_ROWS
    mid = n_rows & (2 * RUN_ALIGN)

    @pl.when(mid != 0)
    def _mid():
        _rows_copy(src_ref, src_at(done), dst_ref, dst_row + done, 2 * RUN_ALIGN, sem).start()

    @pl.when((n_rows & RUN_ALIGN) != 0)
    def _small():
        _rows_copy(src_ref, src_at(done + mid), dst_ref, dst_row + done + mid, RUN_ALIGN,
                   sem).start()


def _wait_rows(src_ref, dst_ref, sem, n_rows):
    def big_body(i, c):
        _rows_copy(src_ref, 0, dst_ref, 0, COPY_ROWS, sem).wait()
        return c

    def small_body(i, c):
        _rows_copy(src_ref, 0, dst_ref, 0, RUN_ALIGN, sem).wait()
        return c

    lax.fori_loop(0, n_rows // COPY_ROWS, big_body, 0)
    lax.fori_loop(0, (n_rows % COPY_ROWS) // RUN_ALIGN, small_body, 0)


def _dispatch_kernel(pc_ref, gbase_ref, tchunks_ref, ev_ref,
                     x_ref, gates_ref, xg_ref, xs_scr, zero_scr, sems):
    t = pl.program_id(0)
    n_tiles = pl.num_programs(0)
    slot = t % 2

    @pl.when(t >= 2)
    def _drain_slot():
        _wait_rows(xs_scr.at[slot], xg_ref, sems.at[slot], tchunks_ref[t - 2, 0] * RUN_ALIGN)

    gates = gates_ref[...]
    sel = gates > 0.0
    sel_bf = _bf(sel)
    rankp = jnp.where(sel, _rank_in_run(sel_bf), -1.0).astype(BF16)
    pc_row, off_row = _run_layout_rows(sel_bf)
    pc_col, off_col = _run_layout_cols(jnp.where(sel, 1.0, 0.0))
    s_id = lax.broadcasted_iota(jnp.int32, (TILE_SLOTS, N_EXPERTS), 0).astype(F32)
    owner = _bf((s_id >= off_row[:1]) & (s_id < off_row[:1] + pc_row[:1]))
    rank_of_slot = _dot(owner, rankp)
    off_of_slot = _dot(owner, (off_col * (1.0 / RUN_ALIGN)).astype(BF16)) * RUN_ALIGN
    s_lane = lax.broadcasted_iota(jnp.int32, (TILE_SLOTS, 128), 0).astype(F32)
    onehot = jnp.concatenate(
        [_bf(rank_of_slot[:, c * 128:(c + 1) * 128] + off_of_slot == s_lane)
         for c in range(MOE_TILE // 128)], axis=1)
    xb = x_ref[...].astype(BF16)
    for c in range(TILE_SLOTS // 512):
        xs_scr[slot, c * 512:(c + 1) * 512, :] = _dot(
            onehot[c * 512:(c + 1) * 512, :], xb).astype(BF16)

    def expert_body(e, src):
        n = pc_ref[t, e]
        _start_run(xs_scr.at[slot], src, xg_ref, gbase_ref[t, e], n, sems.at[slot])
        return src + n

    lax.fori_loop(0, N_EXPERTS, expert_body, 0)

    @pl.when(t == n_tiles - 1)
    def _finish():
        zero_scr[...] = jnp.zeros_like(zero_scr)

        def gap_body(e, total):
            n = ev_ref[PLAN_GAPN, e] * RUN_ALIGN
            _start_run(zero_scr, 0, xg_ref, ev_ref[PLAN_GAPSTART, e], n, sems.at[2],
                       src_advances=False)
            return total + n

        gap_rows = lax.fori_loop(0, N_EXPERTS, gap_body, 0)
        _wait_rows(zero_scr, xg_ref, sems.at[2], gap_rows)
        _wait_rows(xs_scr.at[slot], xg_ref, sems.at[slot], tchunks_ref[t, 0] * RUN_ALIGN)

        @pl.when(t >= 1)
        def _drain_other():
            _wait_rows(xs_scr.at[1 - slot], xg_ref, sems.at[1 - slot],
                       tchunks_ref[t - 1, 0] * RUN_ALIGN)


def _moe_dispatch(x1, gates, pc, gbase, tchunks, ev, n_rows):
    n_tiles = x1.shape[0] // MOE_TILE
    smem = pl.BlockSpec(memory_space=pltpu.SMEM)
    return pl.pallas_call(
        _dispatch_kernel,
        grid=(n_tiles,),
        in_specs=[
            smem, smem, smem, smem,
            pl.BlockSpec((MOE_TILE, D_MODEL), lambda t: (t, 0)),
            pl.BlockSpec((N_EXPERTS, MOE_TILE), lambda t: (0, t)),
        ],
        out_specs=pl.BlockSpec(memory_space=pl.ANY),
        out_shape=jax.ShapeDtypeStruct((n_rows, D_MODEL), BF16),
        scratch_shapes=[
            pltpu.VMEM((2, TILE_SLOTS, D_MODEL), BF16),
            pltpu.VMEM((COPY_ROWS, D_MODEL), BF16),
            pltpu.SemaphoreType.DMA((3,)),
        ],
        compiler_params=pltpu.CompilerParams(
            dimension_semantics=("arbitrary",), vmem_limit_bytes=VMEM_LIMIT_BYTES),
        name="moe_dispatch",
    )(pc, gbase, tchunks, ev, x1, gates)


def _experts_kernel(ev_ref, wg_ref, wu_ref, wd_ref, xg_ref, yg_ref,
                    x_scr, y_scr, wg_scr, wu_scr, wd_scr, sem_in, sem_out):
    e = pl.program_id(0)
    n_blocks = ev_ref[PLAN_NBLOCKS, 0]
    first_block = ev_ref[PLAN_ESTART, e]
    blocks_here = ev_ref[PLAN_ENUM, e]

    def x_copy(g, slot):
        rows = pl.ds(pl.multiple_of(g * EXPERT_BLOCK, EXPERT_BLOCK), EXPERT_BLOCK)
        return pltpu.make_async_copy(xg_ref.at[rows, :], x_scr.at[slot], sem_in.at[slot])

    def y_copy(g, slot):
        rows = pl.ds(pl.multiple_of(g * EXPERT_BLOCK, EXPERT_BLOCK), EXPERT_BLOCK)
        return pltpu.make_async_copy(y_scr.at[slot], yg_ref.at[rows, :], sem_out.at[slot])

    @pl.when((e == 0) & (n_blocks > 0))
    def _first_fetch():
        x_copy(0, 0).start()

    @pl.when(blocks_here > 0)
    def _load_weights():
        wg_scr[...] = wg_ref[...].astype(BF16)
        wu_scr[...] = wu_ref[...].astype(BF16)
        wd_scr[...] = wd_ref[...].astype(BF16)

    def block_body(i, c):
        g = first_block + i
        slot = g % 2
        x_copy(g, slot).wait()

        @pl.when(g + 1 < n_blocks)
        def _prefetch():
            x_copy(g + 1, 1 - slot).start()

        @pl.when(g >= 2)
        def _free_out_buffer():
            y_copy(g - 2, slot).wait()

        h = _swiglu(x_scr[slot], wg_scr[...], wu_scr[...])
        y_scr[slot] = _dot(h.astype(BF16), wd_scr[...]).astype(BF16)
        y_copy(g, slot).start()
        return c

    lax.fori_loop(0, blocks_here, block_body, 0)

    @pl.when(e == pl.num_programs(0) - 1)
    def _drain():
        for back in (1, 2):
            @pl.when(n_blocks >= back)
            def _wait_out():
                g = n_blocks - back
                y_copy(g, g % 2).wait()


def _moe_experts(ev, xg, wg, wu, wd):
    def w_map(e, ev_ref):
        return (e, 0, 0)

    grid_spec = pltpu.PrefetchScalarGridSpec(
        num_scalar_prefetch=1,
        grid=(N_EXPERTS,),
        in_specs=[
            pl.BlockSpec((None, D_MODEL, EXPERT_DIM), w_map),
            pl.BlockSpec((None, D_MODEL, EXPERT_DIM), w_map),
            pl.BlockSpec((None, EXPERT_DIM, D_MODEL), w_map),
            pl.BlockSpec(memory_space=pl.ANY),
        ],
        out_specs=pl.BlockSpec(memory_space=pl.ANY),
        scratch_shapes=[
            pltpu.VMEM((2, EXPERT_BLOCK, D_MODEL), BF16),
            pltpu.VMEM((2, EXPERT_BLOCK, D_MODEL), BF16),
            pltpu.VMEM((D_MODEL, EXPERT_DIM), BF16),
            pltpu.VMEM((D_MODEL, EXPERT_DIM), BF16),
            pltpu.VMEM((EXPERT_DIM, D_MODEL), BF16),
            pltpu.SemaphoreType.DMA((2,)),
            pltpu.SemaphoreType.DMA((2,)),
        ],
    )
    return pl.pallas_call(
        _experts_kernel,
        grid_spec=grid_spec,
        out_shape=jax.ShapeDtypeStruct(xg.shape, BF16),
        compiler_params=pltpu.CompilerParams(
            dimension_semantics=("arbitrary",), vmem_limit_bytes=VMEM_LIMIT_BYTES),
        name="moe_experts",
    )(ev, wg, wu, wd, xg)


def _combine_kernel(pc_ref, gbase_ref, tchunks_ref,
                    x_ref, gates_ref, yg_ref, wgs_ref, wus_ref, wds_ref, ln2g_ref, ln2b_ref,
                    out_a_ref, out_b_ref, ys_scr, sems, *, tiles_a):
    t = pl.program_id(0)
    n_tiles = pl.num_programs(0)
    slot = t % 2

    def start_gather(tile, dst_slot):
        def expert_body(e, dst):
            n = pc_ref[tile, e]
            _start_run(yg_ref, gbase_ref[tile, e], ys_scr.at[dst_slot], dst, n,
                       sems.at[dst_slot])
            return dst + n

        lax.fori_loop(0, N_EXPERTS, expert_body, 0)

    @pl.when(t == 0)
    def _prime():
        ys_scr[...] = jnp.zeros_like(ys_scr)
        start_gather(0, 0)

    @pl.when(t + 1 < n_tiles)
    def _prefetch():
        start_gather(t + 1, 1 - slot)

    gates = gates_ref[...]
    sel = gates > 0.0
    sel_bf = _bf(sel)
    rankp = jnp.where(sel, _rank_in_run(sel_bf), -1.0)
    pc_row, off_row = _run_layout_rows(sel_bf)
    pc_col, off_col = _run_layout_cols(jnp.where(sel, 1.0, 0.0))
    pad = jnp.zeros((128 - N_EXPERTS, MOE_TILE), F32)
    gates_t = jnp.concatenate([gates, pad], axis=0).T.astype(BF16)
    rankp_t = jnp.concatenate([rankp, pad], axis=0).T.astype(BF16)
    s_id = lax.broadcasted_iota(jnp.int32, (N_EXPERTS, TILE_SLOTS), 1).astype(F32)
    owner = _bf((s_id >= off_col[:, :1]) & (s_id < off_col[:, :1] + pc_col[:, :1]))
    owner = jnp.concatenate([owner, jnp.zeros((128 - N_EXPERTS, TILE_SLOTS), BF16)], axis=0)
    rank_of_slot = _dot(rankp_t, owner)
    gate_of_slot = _dot(gates_t, owner)
    o16 = jnp.concatenate([off_row * (1.0 / RUN_ALIGN), jnp.zeros((16, 128 - N_EXPERTS), F32)],
                          axis=1).astype(BF16)
    off_of_slot = _dot(o16, owner)[:1] * RUN_ALIGN
    s_row = lax.broadcasted_iota(jnp.int32, (1, TILE_SLOTS), 1).astype(F32)
    weights = jnp.where(rank_of_slot + off_of_slot == s_row, gate_of_slot, 0.0).astype(BF16)

    x = x_ref[...]
    xb = x.astype(BF16)
    hs = _swiglu(xb, wgs_ref[...].astype(BF16), wus_ref[...].astype(BF16))
    shared = _dot(hs.astype(BF16), wds_ref[...].astype(BF16))

    _wait_rows(yg_ref, ys_scr.at[slot], sems.at[slot], tchunks_ref[t, 0] * RUN_ALIGN)
    routed = _dot(weights, ys_scr[slot])
    y = _layer_norm(ALPHA * x + (routed + shared), ln2g_ref[...], ln2b_ref[...])

    @pl.when(t < tiles_a)
    def _store_a():
        out_a_ref[...] = y

    @pl.when(t >= tiles_a)
    def _store_b():
        out_b_ref[...] = y


def _moe_combine(x1, gates, yg, pc, gbase, tchunks, wgs, wus, wds, ln2g, ln2b, rows_a):
    n_tok = x1.shape[0]
    tiles_a = rows_a // MOE_TILE
    smem = pl.BlockSpec(memory_space=pltpu.SMEM)
    return pl.pallas_call(
        functools.partial(_combine_kernel, tiles_a=tiles_a),
        grid=(n_tok // MOE_TILE,),
        in_specs=[
            smem, smem, smem,
            pl.BlockSpec((MOE_TILE, D_MODEL), lambda t: (t, 0)),
            pl.BlockSpec((N_EXPERTS, MOE_TILE), lambda t: (0, t)),
            pl.BlockSpec(memory_space=pl.ANY),
            _const_spec((D_MODEL, SHARED_DIM)),
            _const_spec((D_MODEL, SHARED_DIM)),
            _const_spec((SHARED_DIM, D_MODEL)),
            _const_spec((1, D_MODEL)),
            _const_spec((1, D_MODEL)),
        ],
        out_specs=[
            pl.BlockSpec((MOE_TILE, D_MODEL), lambda t: (jnp.minimum(t, tiles_a - 1), 0)),
            pl.BlockSpec((MOE_TILE, D_MODEL), lambda t: (jnp.maximum(t - tiles_a, 0), 0)),
        ],
        out_shape=[
            jax.ShapeDtypeStruct((rows_a, D_MODEL), F32),
            jax.ShapeDtypeStruct((n_tok - rows_a, D_MODEL), F32),
        ],
        scratch_shapes=[
            pltpu.VMEM((2, TILE_SLOTS, D_MODEL), BF16),
            pltpu.SemaphoreType.DMA((2,)),
        ],
        compiler_params=pltpu.CompilerParams(
            dimension_semantics=("arbitrary",), vmem_limit_bytes=VMEM_LIMIT_BYTES),
        name="moe_combine",
    )(pc, gbase, tchunks, x1, gates, yg, wgs, wus, wds, ln2g, ln2b)


def _moe(x1, rows_a, wrt, rbias, wg, wu, wd, wgs, wus, wds, ln2g, ln2b):
    n_tiles = x1.shape[0] // MOE_TILE
    gates, pc3 = _moe_route(x1, wrt, rbias)
    pc = pc3.reshape(n_tiles, 128)
    gbase, tchunks, ev = _moe_plan(pc)
    n_rows = _max_blocks(n_tiles) * EXPERT_BLOCK
    xg = _moe_dispatch(x1, gates, pc, gbase, tchunks, ev, n_rows)
    yg = _moe_experts(ev, xg, wg, wu, wd)
    return _moe_combine(x1, gates, yg, pc, gbase, tchunks, wgs, wus, wds, ln2g, ln2b, rows_a)


def kernel(x_prompt, x_sample, cache_win_k, cache_win_v, rel_bias_table, w_in, b_in, attn_sinks,
           sg_ln_g, sg_ln_b, sg_w, sg_b, w_proj_a, w_proj_b, w_out, ln1_g, ln1_b, w_router,
           router_bias, w_gate_e, w_up_e, w_down_e, w_gate_s, w_up_s, w_down_s, ln2_g, ln2_b):
    assert DEPTH == 1 and w_in.shape[0] == 1
    batch, seq, _ = x_prompt.shape
    nb, ds, _ = x_sample.shape
    win = cache_win_k.shape[2]

    w_in_bf = w_in[0].astype(BF16)
    wpa = w_proj_a[0].astype(BF16)
    wpb = w_proj_b[0].astype(BF16)
    wout = w_out[0].astype(BF16)
    b_in2 = b_in[0].reshape(1, IN_W)
    sgg = sg_ln_g[0].reshape(1, SG_WIDTH)
    sgb = sg_ln_b[0].reshape(1, SG_WIDTH)
    sgw = sg_w[0]
    sgbias = sg_b[0].reshape(N_SG_GROUPS, CHUNK, 1)
    ln1g = ln1_g[0].reshape(1, D_MODEL)
    ln1b = ln1_b[0].reshape(1, D_MODEL)
    sinks = attn_sinks[0]

    x1_p, wk_p, wv_p = _mixer_prompt(
        x_prompt, nb * ds, rel_bias_table, sinks, w_in_bf, b_in2, sgg, sgb, sgw, sgbias, wpa, wpb,
        wout, ln1g, ln1b)
    x1, wk_s, wv_s, cvs = _mixer_sample(
        x_sample, cache_win_k[0].reshape(nb, win, KV_W), cache_win_v[0].reshape(nb, win, KV_W),
        x1_p, rel_bias_table, sinks, w_in_bf, b_in2, sgg, sgb, sgw, sgbias, wpa, wpb, wout,
        ln1g, ln1b)

    y_p, y_s = _moe(x1, batch * seq, wrt=w_router[0].T,
                    rbias=router_bias[0].reshape(N_EXPERTS, 1),
                    wg=w_gate_e[0], wu=w_up_e[0], wd=w_down_e[0],
                    wgs=w_gate_s[0], wus=w_up_s[0], wds=w_down_s[0],
                    ln2g=ln2_g[0].reshape(1, D_MODEL), ln2b=ln2_b[0].reshape(1, D_MODEL))
    y_p = y_p.reshape(batch, seq, D_MODEL)
    y_s = y_s.reshape(nb, ds, D_MODEL)

    kv_shape = (1, -1, WINDOW, N_KV_HEADS, HEAD_DIM)
    return (y_p, y_s,
            wk_p.reshape(kv_shape), wv_p.reshape(kv_shape),
            wk_s.reshape(1, nb, win, N_KV_HEADS, HEAD_DIM),
            wv_s.reshape(1, nb, win, N_KV_HEADS, HEAD_DIM),
            cvs.reshape(1, nb, ds, N_SG_GROUPS, SG_GROUP_DIM))
```

```python
import functools
import math

import jax
import jax.numpy as jnp
import numpy as np
from jax import lax
from jax.experimental import pallas as pl
from jax.experimental.pallas import tpu as pltpu

F32 = jnp.float32
BF16 = jnp.bfloat16

D_MODEL = 1024
DEPTH = 1
HEAD_DIM = 64
N_Q_HEADS = 16
N_KV_HEADS = 2
Q_PER_KV = N_Q_HEADS // N_KV_HEADS
WINDOW = 128
ATTN_SCALE = HEAD_DIM ** -0.5
NEG_INF = -1e30
N_BUCKETS = 32
BUCKET_MAX_EXACT = 16
BUCKET_MAX_DIST = 128
CHUNK = 128
N_SG_GROUPS = 4
SG_GROUP_DIM = 128
SG_WIDTH = N_SG_GROUPS * SG_GROUP_DIM
Q_W = N_Q_HEADS * HEAD_DIM
KV_W = N_KV_HEADS * HEAD_DIM
Q_END = Q_W
K_END = Q_END + KV_W
V_END = K_END + KV_W
U_END = V_END + SG_WIDTH
VS_END = U_END + SG_WIDTH
GA_END = VS_END + D_MODEL
IN_W = GA_END + D_MODEL
N_EXPERTS = 64
TOP_K = 8
N_EXPERT_GROUPS = 8
EXPERTS_PER_GROUP = N_EXPERTS // N_EXPERT_GROUPS
TOPK_GROUPS = 4
EXPERT_DIM = 256
SHARED_DIM = 256
ROUTED_SCALE = 2.5
ALPHA = (2 * DEPTH) ** 0.25
LN_EPS = 1e-5

VMEM_LIMIT_BYTES = 56 * 1024 * 1024

PROMPT_STEP = 512
SAMPLE_SEQS_PER_STEP = 32
MOE_TILE = 256
ROUTE_TILES = 4
RUN_ALIGN = 16
COPY_ROWS = 4 * RUN_ALIGN
EXPERT_BLOCK = 1024
TILE_SLOTS = -(-(MOE_TILE * TOP_K + N_EXPERTS * (RUN_ALIGN - 1)) // 512) * 512


def _t5_bucket_np(dist):
    d = np.maximum(dist, 0)
    ratio = np.maximum(d, 1).astype(np.float32) / np.float32(BUCKET_MAX_EXACT)
    large = BUCKET_MAX_EXACT + (
        np.log(ratio) / np.float32(math.log(BUCKET_MAX_DIST / BUCKET_MAX_EXACT))
        * np.float32(N_BUCKETS - BUCKET_MAX_EXACT)).astype(np.int32)
    large = np.minimum(large, N_BUCKETS - 1)
    return np.where(d < BUCKET_MAX_EXACT, d, large).astype(np.int32)


def _layer_norm(x, g, b):
    mu = jnp.mean(x, -1, keepdims=True)
    xc = x - mu
    var = jnp.mean(xc * xc, -1, keepdims=True)
    return xc * lax.rsqrt(var + LN_EPS) * g + b


def _gelu(x):
    return jax.nn.gelu(x)


def _dot(a, b):
    return jnp.dot(a, b, preferred_element_type=F32)


def _dot_nt(a, b):
    return lax.dot_general(a, b, (((1,), (1,)), ((), ())), preferred_element_type=F32)


def _project(xb, w_in_ref, b_in_ref, lo, hi):
    return _dot(xb, w_in_ref[:, lo:hi]) + b_in_ref[:, lo:hi]


def _expand_bias(bucket, table_ref, head):
    acc = jnp.zeros(bucket.shape, F32)
    for b in range(N_BUCKETS):
        acc = jnp.where(bucket == b, table_ref[b, head], acc)
    return acc


def _merge_and_norm(x, a_bf, s_bf, ga, gb, wpa_ref, wpb_ref, wout_ref, g_ref, b_ref):
    pa = _dot(a_bf, wpa_ref[...])
    pb = _dot(s_bf, wpb_ref[...])
    hpre = jax.nn.sigmoid(ga) * pa + jax.nn.sigmoid(gb) * pb
    h = _dot(hpre.astype(BF16), wout_ref[...])
    return _layer_norm(ALPHA * x + h, g_ref[...], b_ref[...])


def _mixer_prompt_kernel(table_ref, bucket_ref, sink_ref, x_ref, w_in_ref, b_in_ref,
                         sgg_ref, sgb_ref, sgw_ref, sgbias_ref, wpa_ref, wpb_ref, wout_ref,
                         ln1g_ref, ln1b_ref,
                         x1_ref, wk_ref, wv_ref,
                         bias_scr, tril_scr, kprev_scr, vprev_scr, a_scr, s_scr):
    b_idx = pl.program_id(0)
    n_idx = pl.program_id(1)
    n_blocks = PROMPT_STEP // WINDOW

    @pl.when((b_idx == 0) & (n_idx == 0))
    def _init_tables():
        bucket = bucket_ref[...]
        for h in range(N_Q_HEADS):
            g, r = divmod(h, Q_PER_KV)
            pair, parity = divmod(r, 2)
            bias_scr[g, pair, :, parity * 2 * WINDOW:(parity + 1) * 2 * WINDOW] = (
                _expand_bias(bucket, table_ref, h))
        row = lax.broadcasted_iota(jnp.int32, (CHUNK, CHUNK), 0)
        col = lax.broadcasted_iota(jnp.int32, (CHUNK, CHUNK), 1)
        for g in range(N_SG_GROUPS):
            tril_scr[g] = jnp.where(row >= col, sgw_ref[g], 0.0).astype(BF16)

    @pl.when(n_idx == 0)
    def _reset_carry():
        kprev_scr[...] = jnp.zeros_like(kprev_scr)
        vprev_scr[...] = jnp.zeros_like(vprev_scr)

    x = x_ref[...]
    xb = x.astype(BF16)
    q_bf = (_project(xb, w_in_ref, b_in_ref, 0, Q_END) * ATTN_SCALE).astype(BF16)
    k = _project(xb, w_in_ref, b_in_ref, Q_END, K_END)
    v = _project(xb, w_in_ref, b_in_ref, K_END, V_END)

    @pl.when(n_idx == pl.num_programs(1) - 1)
    def _emit_window():
        wk_ref[...] = k[PROMPT_STEP - WINDOW:, :]
        wv_ref[...] = v[PROMPT_STEP - WINDOW:, :]

    low = lax.broadcasted_iota(jnp.int32, (PROMPT_STEP, KV_W), 1) < HEAD_DIM

    def lane_halves(t):
        t_sw = pltpu.roll(t, HEAD_DIM, axis=1)
        zero = jnp.zeros_like(t)
        return [[jnp.where(low, t, zero).astype(BF16), jnp.where(low, zero, t_sw).astype(BF16)],
                [jnp.where(low, t_sw, zero).astype(BF16), jnp.where(low, zero, t).astype(BF16)]]

    k_half = lane_halves(k)
    v_half = lane_halves(v)

    row = lax.broadcasted_iota(jnp.int32, (WINDOW, 4 * WINDOW), 0)
    col = lax.broadcasted_iota(jnp.int32, (WINDOW, 4 * WINDOW), 1) % (2 * WINDOW)
    dist = row + WINDOW - col
    band_valid = (dist >= 0) & (dist <= WINDOW)
    first_valid = band_valid & ((col >= WINDOW) | (n_idx > 0))
    lane_low = lax.broadcasted_iota(jnp.int32, (Q_PER_KV // 2, WINDOW, 2 * HEAD_DIM), 2) < HEAD_DIM
    ones_rows = lax.broadcasted_iota(jnp.int32, (4 * WINDOW, 2 * HEAD_DIM), 0) < 2 * WINDOW
    ones_cols = lax.broadcasted_iota(jnp.int32, (4 * WINDOW, 2 * HEAD_DIM), 1) < HEAD_DIM
    sum_block = _bf(ones_rows == ones_cols)

    for j in range(n_blocks):
        r0, r1 = j * WINDOW, (j + 1) * WINDOW
        valid = first_valid if j == 0 else band_valid
        for g in range(N_KV_HEADS):
            def band(cur, prev_scr):
                parts = []
                for s in range(2):
                    prev = prev_scr[2 * g + s] if j == 0 else cur[g][s][r0 - WINDOW:r0]
                    parts += [prev, cur[g][s][r0:r1]]
                return jnp.concatenate(parts, axis=0)

            kd = band(k_half, kprev_scr)
            vd = jnp.concatenate([band(v_half, vprev_scr), sum_block], axis=1)
            q0 = g * Q_PER_KV * HEAD_DIM
            qp = jnp.concatenate(
                [q_bf[r0:r1, q0 + pr * 2 * HEAD_DIM:q0 + (pr + 1) * 2 * HEAD_DIM]
                 for pr in range(Q_PER_KV // 2)], axis=0)
            logits = _dot_nt(qp, kd).reshape(Q_PER_KV // 2, WINDOW, 4 * WINDOW)
            logits = jnp.where(valid[None], logits + bias_scr[g], NEG_INF)
            probs, sink_terms = [], []
            for s in range(2):
                l_s = logits[:, :, s * 2 * WINDOW:(s + 1) * 2 * WINDOW]
                sink = sink_ref[s, g]
                m = jnp.maximum(jnp.max(l_s, -1, keepdims=True), sink)
                probs.append(jnp.exp(l_s - m))
                sink_terms.append(jnp.broadcast_to(jnp.exp(sink - m), lane_low.shape))
            p = jnp.concatenate(probs, axis=-1).reshape(4 * WINDOW, 4 * WINDOW).astype(BF16)
            out = _dot(p, vd).reshape(Q_PER_KV // 2, WINDOW, 4 * HEAD_DIM)
            den = out[:, :, 2 * HEAD_DIM:] + jnp.where(lane_low, sink_terms[0], sink_terms[1])
            o = out[:, :, :2 * HEAD_DIM] / den
            for pr in range(Q_PER_KV // 2):
                a_scr[r0:r1, q0 + pr * 2 * HEAD_DIM:q0 + (pr + 1) * 2 * HEAD_DIM] = (
                    o[pr].astype(BF16))

    for g in range(N_KV_HEADS):
        for s in range(2):
            kprev_scr[2 * g + s] = k_half[g][s][PROMPT_STEP - WINDOW:]
            vprev_scr[2 * g + s] = v_half[g][s][PROMPT_STEP - WINDOW:]

    u = _gelu(_project(xb, w_in_ref, b_in_ref, V_END, U_END))
    vs = _gelu(_project(xb, w_in_ref, b_in_ref, U_END, VS_END))
    vs_bf = _layer_norm(vs, sgg_ref[...], sgb_ref[...]).astype(BF16)
    for j in range(n_blocks):
        r0, r1 = j * WINDOW, (j + 1) * WINDOW
        for g in range(N_SG_GROUPS):
            c0, c1 = g * SG_GROUP_DIM, (g + 1) * SG_GROUP_DIM
            sg = _dot(tril_scr[g], vs_bf[r0:r1, c0:c1]) + sgbias_ref[g]
            s_scr[r0:r1, c0:c1] = (u[r0:r1, c0:c1] * sg).astype(BF16)

    ga = _project(xb, w_in_ref, b_in_ref, VS_END, GA_END)
    gb = _project(xb, w_in_ref, b_in_ref, GA_END, IN_W)
    x1_ref[...] = _merge_and_norm(x, a_scr[...], s_scr[...], ga, gb, wpa_ref, wpb_ref,
                                  wout_ref, ln1g_ref, ln1b_ref)


def _const_spec(shape):
    zeros = (0,) * len(shape)
    return pl.BlockSpec(shape, lambda *_: zeros, pipeline_mode=pl.Buffered(1))


def _mixer_prompt(x, extra_rows, table, sinks, w_in_bf, b_in, sgg, sgb, sgw, sgbias, wpa, wpb,
                  wout, ln1g, ln1b):
    batch, seq, _ = x.shape
    n_steps = seq // PROMPT_STEP
    dist = np.arange(WINDOW)[:, None] + WINDOW - np.arange(2 * WINDOW)[None, :]
    bucket = jnp.asarray(_t5_bucket_np(dist))
    sink_pairs = jnp.transpose(sinks.reshape(N_KV_HEADS, Q_PER_KV // 2, 2), (2, 0, 1)).reshape(
        2, N_KV_HEADS, Q_PER_KV // 2, 1, 1)
    smem = pl.BlockSpec(memory_space=pltpu.SMEM)
    in_specs = [
        smem,
        _const_spec((WINDOW, 2 * WINDOW)),
        _const_spec((2, N_KV_HEADS, Q_PER_KV // 2, 1, 1)),
        pl.BlockSpec((None, PROMPT_STEP, D_MODEL), lambda b, n: (b, n, 0)),
        _const_spec((D_MODEL, IN_W)),
        _const_spec((1, IN_W)),
        _const_spec((1, SG_WIDTH)),
        _const_spec((1, SG_WIDTH)),
        _const_spec((N_SG_GROUPS, CHUNK, CHUNK)),
        _const_spec((N_SG_GROUPS, CHUNK, 1)),
        _const_spec((Q_W, D_MODEL)),
        _const_spec((SG_WIDTH, D_MODEL)),
        _const_spec((D_MODEL, D_MODEL)),
        _const_spec((1, D_MODEL)),
        _const_spec((1, D_MODEL)),
    ]
    out_specs = [
        pl.BlockSpec((PROMPT_STEP, D_MODEL), lambda b, n: (b * n_steps + n, 0)),
        pl.BlockSpec((None, WINDOW, KV_W), lambda b, n: (b, 0, 0)),
        pl.BlockSpec((None, WINDOW, KV_W), lambda b, n: (b, 0, 0)),
    ]
    out_shape = [
        jax.ShapeDtypeStruct((batch * seq + extra_rows, D_MODEL), F32),
        jax.ShapeDtypeStruct((batch, WINDOW, KV_W), F32),
        jax.ShapeDtypeStruct((batch, WINDOW, KV_W), F32),
    ]
    scratch = [
        pltpu.VMEM((N_KV_HEADS, Q_PER_KV // 2, WINDOW, 4 * WINDOW), F32),
        pltpu.VMEM((N_SG_GROUPS, CHUNK, CHUNK), BF16),
        pltpu.VMEM((2 * N_KV_HEADS, WINDOW, KV_W), BF16),
        pltpu.VMEM((2 * N_KV_HEADS, WINDOW, KV_W), BF16),
        pltpu.VMEM((PROMPT_STEP, Q_W), BF16),
        pltpu.VMEM((PROMPT_STEP, SG_WIDTH), BF16),
    ]
    return pl.pallas_call(
        _mixer_prompt_kernel,
        grid=(batch, n_steps),
        in_specs=in_specs,
        out_specs=out_specs,
        out_shape=out_shape,
        scratch_shapes=scratch,
        compiler_params=pltpu.CompilerParams(
            dimension_semantics=("arbitrary", "arbitrary"),
            vmem_limit_bytes=VMEM_LIMIT_BYTES),
        name="mixer_prompt",
    )(table, bucket, sink_pairs, x, w_in_bf, b_in, sgg, sgb, sgw, sgbias, wpa, wpb, wout,
      ln1g, ln1b)


def _mixer_sample_kernel(table_ref, bucket_c_ref, bucket_n_ref, sink_ref, x_ref, ck_ref, cv_ref,
                         w_in_ref, b_in_ref, sgg_ref, sgb_ref, sgw8_ref, sgbias8_ref,
                         wpa_ref, wpb_ref, wout_ref, ln1g_ref, ln1b_ref, x1_in_ref,
                         x1_ref, wk_ref, wv_ref, cvs_ref,
                         bias_c_scr, bias_n_scr, a_scr, s_scr):
    del x1_in_ref
    nseq, ds = SAMPLE_SEQS_PER_STEP, x_ref.shape[1]
    rows = nseq * ds
    win = ck_ref.shape[1]
    qrows = Q_PER_KV * ds

    @pl.when(pl.program_id(0) == 0)
    def _init_tables():
        bc = bucket_c_ref[...]
        bn = bucket_n_ref[...]
        for h in range(N_Q_HEADS):
            g, r = divmod(h, Q_PER_KV)
            bias_c_scr[g, r * ds:(r + 1) * ds, :] = _expand_bias(bc, table_ref, h)
            bias_n_scr[g, r * ds:(r + 1) * ds, :] = _expand_bias(bn, table_ref, h)

    x = x_ref[...].reshape(rows, D_MODEL)
    xb = x.astype(BF16)
    q = _project(xb, w_in_ref, b_in_ref, 0, Q_END) * ATTN_SCALE
    k = _project(xb, w_in_ref, b_in_ref, Q_END, K_END)
    v = _project(xb, w_in_ref, b_in_ref, K_END, V_END)
    ck = ck_ref[...]
    cv = cv_ref[...]
    wk_ref[:, :win - ds, :] = ck[:, ds:, :]
    wk_ref[:, win - ds:, :] = k.reshape(nseq, ds, KV_W)
    wv_ref[:, :win - ds, :] = cv[:, ds:, :]
    wv_ref[:, win - ds:, :] = v.reshape(nseq, ds, KV_W)
    nk = wk_ref[...]
    nv = wv_ref[...]

    q3 = q.reshape(nseq, ds, Q_W)
    t_q = lax.broadcasted_iota(jnp.int32, (qrows, win), 0) % ds
    col = lax.broadcasted_iota(jnp.int32, (qrows, win), 1)
    dist_c = t_q + win - col
    valid_c = (dist_c >= 0) & (dist_c <= WINDOW)
    dist_n = t_q - (col - (win - ds))
    valid_n = (col >= win - ds) & (dist_n >= 0) & (dist_n <= WINDOW)

    for g in range(N_KV_HEADS):
        c0, c1 = g * HEAD_DIM, (g + 1) * HEAD_DIM
        h0 = g * Q_PER_KV
        qs = jnp.concatenate(
            [q3[:, :, (h0 + r) * HEAD_DIM:(h0 + r + 1) * HEAD_DIM] for r in range(Q_PER_KV)],
            axis=1).astype(BF16)
        kc = ck[:, :, c0:c1].astype(BF16)
        vc = cv[:, :, c0:c1].astype(BF16)
        kn = nk[:, :, c0:c1].astype(BF16)
        vn = nv[:, :, c0:c1].astype(BF16)
        lc = jnp.einsum('bqd,bkd->bqk', qs, kc, preferred_element_type=F32)
        ln = jnp.einsum('bqd,bkd->bqk', qs, kn, preferred_element_type=F32)
        lc = jnp.where(valid_c[None], lc + bias_c_scr[g][None], NEG_INF)
        ln = jnp.where(valid_n[None], ln + bias_n_scr[g][None], NEG_INF)
        sink = sink_ref[g]
        m = jnp.maximum(jnp.maximum(jnp.max(lc, -1, keepdims=True),
                                    jnp.max(ln, -1, keepdims=True)), sink[None])
        pc = jnp.exp(lc - m)
        pn = jnp.exp(ln - m)
        den = (jnp.sum(pc, -1, keepdims=True) + jnp.sum(pn, -1, keepdims=True)
               + jnp.exp(sink[None] - m))
        o = (jnp.einsum('bqk,bkd->bqd', pc.astype(BF16), vc, preferred_element_type=F32)
             + jnp.einsum('bqk,bkd->bqd', pn.astype(BF16), vn, preferred_element_type=F32))
        o = o / den
        for r in range(Q_PER_KV):
            a_scr[:, :, (h0 + r) * HEAD_DIM:(h0 + r + 1) * HEAD_DIM] = (
                o[:, r * ds:(r + 1) * ds, :])

    u = _gelu(_project(xb, w_in_ref, b_in_ref, V_END, U_END))
    vs = _gelu(_project(xb, w_in_ref, b_in_ref, U_END, VS_END))
    vs_ln = _layer_norm(vs, sgg_ref[...], sgb_ref[...])
    cvs_ref[...] = vs_ln.reshape(nseq, ds, SG_WIDTH)
    vq = vs_ln.astype(BF16).astype(F32).reshape(nseq, ds, SG_WIDTH)
    u3 = u.reshape(nseq, ds, SG_WIDTH)
    i_row = lax.broadcasted_iota(jnp.int32, (ds, 1), 0)
    for g in range(N_SG_GROUPS):
        c0, c1 = g * SG_GROUP_DIM, (g + 1) * SG_GROUP_DIM
        acc = jnp.broadcast_to(sgbias8_ref[g][None], (nseq, ds, SG_GROUP_DIM))
        for j in range(ds):
            w_col = jnp.where(i_row >= j, sgw8_ref[g, j], 0.0)
            w_col = w_col.astype(BF16).astype(F32)
            acc = acc + w_col[None] * vq[:, j:j + 1, c0:c1]
        s_scr[:, :, c0:c1] = u3[:, :, c0:c1] * acc

    ga = _project(xb, w_in_ref, b_in_ref, VS_END, GA_END)
    gb = _project(xb, w_in_ref, b_in_ref, GA_END, IN_W)
    x1 = _merge_and_norm(x, a_scr[...].reshape(rows, Q_W).astype(BF16),
                         s_scr[...].reshape(rows, SG_WIDTH).astype(BF16),
                         ga, gb, wpa_ref, wpb_ref, wout_ref, ln1g_ref, ln1b_ref)
    x1_ref[...] = x1


def _mixer_sample(x, cache_k, cache_v, x1_all, table, sinks, w_in_bf, b_in, sgg, sgb, sgw,
                  sgbias, wpa, wpb, wout, ln1g, ln1b):
    nb, ds, _ = x.shape
    first_block = (x1_all.shape[0] - nb * ds) // (SAMPLE_SEQS_PER_STEP * ds)
    win = cache_k.shape[1]
    nseq = SAMPLE_SEQS_PER_STEP
    qrows = Q_PER_KV * ds
    t = np.arange(ds)[:, None]
    bucket_c = jnp.asarray(_t5_bucket_np(t + win - np.arange(win)[None, :]))
    bucket_n = jnp.asarray(_t5_bucket_np(t - (np.arange(win)[None, :] - (win - ds))))
    sink_rows = jnp.repeat(sinks.reshape(N_KV_HEADS, Q_PER_KV), ds, axis=1).reshape(
        N_KV_HEADS, qrows, 1)
    sgw8 = jnp.transpose(sgw[:, :ds, :ds], (0, 2, 1))[..., None]
    sgbias8 = sgbias[:, :ds, :]
    smem = pl.BlockSpec(memory_space=pltpu.SMEM)
    in_specs = [
        smem,
        _const_spec((ds, win)),
        _const_spec((ds, win)),
        _const_spec((N_KV_HEADS, qrows, 1)),
        pl.BlockSpec((nseq, ds, D_MODEL), lambda i: (i, 0, 0)),
        pl.BlockSpec((nseq, win, KV_W), lambda i: (i, 0, 0)),
        pl.BlockSpec((nseq, win, KV_W), lambda i: (i, 0, 0)),
        _const_spec((D_MODEL, IN_W)),
        _const_spec((1, IN_W)),
        _const_spec((1, SG_WIDTH)),
        _const_spec((1, SG_WIDTH)),
        _const_spec((N_SG_GROUPS, ds, ds, 1)),
        _const_spec((N_SG_GROUPS, ds, 1)),
        _const_spec((Q_W, D_MODEL)),
        _const_spec((SG_WIDTH, D_MODEL)),
        _const_spec((D_MODEL, D_MODEL)),
        _const_spec((1, D_MODEL)),
        _const_spec((1, D_MODEL)),
        pl.BlockSpec(memory_space=pl.ANY),
    ]
    out_specs = [
        pl.BlockSpec((nseq * ds, D_MODEL), lambda i: (first_block + i, 0)),
        pl.BlockSpec((nseq, win, KV_W), lambda i: (i, 0, 0)),
        pl.BlockSpec((nseq, win, KV_W), lambda i: (i, 0, 0)),
        pl.BlockSpec((nseq, ds, SG_WIDTH), lambda i: (i, 0, 0)),
    ]
    out_shape = [
        jax.ShapeDtypeStruct(x1_all.shape, F32),
        jax.ShapeDtypeStruct((nb, win, KV_W), F32),
        jax.ShapeDtypeStruct((nb, win, KV_W), F32),
        jax.ShapeDtypeStruct((nb, ds, SG_WIDTH), F32),
    ]
    scratch = [
        pltpu.VMEM((N_KV_HEADS, qrows, win), F32),
        pltpu.VMEM((N_KV_HEADS, qrows, win), F32),
        pltpu.VMEM((nseq, ds, Q_W), F32),
        pltpu.VMEM((nseq, ds, SG_WIDTH), F32),
    ]
    return pl.pallas_call(
        _mixer_sample_kernel,
        grid=(nb // nseq,),
        in_specs=in_specs,
        out_specs=out_specs,
        out_shape=out_shape,
        scratch_shapes=scratch,
        input_output_aliases={len(in_specs) - 1: 0},
        compiler_params=pltpu.CompilerParams(
            dimension_semantics=("arbitrary",),
            vmem_limit_bytes=VMEM_LIMIT_BYTES),
        name="mixer_sample",
    )(table, bucket_c, bucket_n, sink_rows, x, cache_k, cache_v, w_in_bf, b_in, sgg, sgb,
      sgw8, sgbias8, wpa, wpb, wout, ln1g, ln1b, x1_all)


def _route(xb, wrt_ref, rbias_ref):
    n = xb.shape[0]
    logits = _dot_nt(wrt_ref[...].astype(BF16), xb)
    scores = jax.nn.sigmoid(logits)
    sel = scores + rbias_ref[...]
    shape3 = (N_EXPERT_GROUPS, EXPERTS_PER_GROUP, n)
    scores3 = scores.reshape(shape3)
    sel3 = sel.reshape(shape3)
    i_in = lax.broadcasted_iota(jnp.int32, shape3, 1)
    g_id = lax.broadcasted_iota(jnp.int32, shape3, 0)
    e_id = g_id * EXPERTS_PER_GROUP + i_in
    neg = -jnp.inf

    m1 = jnp.max(sel3, axis=1, keepdims=True)
    first = jnp.min(jnp.where(sel3 == m1, i_in, EXPERTS_PER_GROUP), axis=1, keepdims=True)
    m2 = jnp.max(jnp.where(i_in == first, neg, sel3), axis=1, keepdims=True)
    gscore = m1 + m2

    gsel = jnp.zeros(gscore.shape, jnp.bool_)
    gid1 = lax.broadcasted_iota(jnp.int32, gscore.shape, 0)
    for _ in range(TOPK_GROUPS):
        m = jnp.max(gscore, axis=0, keepdims=True)
        pick = jnp.min(jnp.where(gscore == m, gid1, N_EXPERT_GROUPS), axis=0, keepdims=True)
        chosen = gid1 == pick
        gsel = gsel | chosen
        gscore = jnp.where(chosen, neg, gscore)
    val = jnp.where(gsel, sel3, NEG_INF)

    esel = jnp.zeros(shape3, jnp.bool_)
    for _ in range(TOP_K):
        m = jnp.max(jnp.max(val, axis=0, keepdims=True), axis=1, keepdims=True)
        cand = jnp.where(val == m, e_id, N_EXPERTS)
        pick = jnp.min(jnp.min(cand, axis=0, keepdims=True), axis=1, keepdims=True)
        chosen = e_id == pick
        esel = esel | chosen
        val = jnp.where(chosen, neg, val)
    w_sel = jnp.where(esel, scores3, 0.0)
    total = jnp.sum(jnp.sum(w_sel, axis=0, keepdims=True), axis=1, keepdims=True)
    gates = w_sel / total * ROUTED_SCALE
    return gates.reshape(N_EXPERTS, n)


def _swiglu(xb, wg, wu):
    return jax.nn.silu(_dot(xb, wg)) * _dot(xb, wu)


def _bf(mask):
    return jnp.where(mask, 1.0, 0.0).astype(BF16)


def _round_up_run(count):
    return jnp.ceil(count * (1.0 / RUN_ALIGN)) * RUN_ALIGN


def _run_layout_rows(sel_bf):
    n = sel_bf.shape[1]
    ones = jnp.ones((16, n), BF16)
    pc = _round_up_run(_dot_nt(ones, sel_bf))
    lower = lax.broadcasted_iota(jnp.int32, (N_EXPERTS, N_EXPERTS), 0)
    upper = lax.broadcasted_iota(jnp.int32, (N_EXPERTS, N_EXPERTS), 1)
    off = _dot(pc.astype(BF16), _bf(lower < upper))
    return pc, off


def _run_layout_cols(sel_f32):
    cnt = jnp.sum(sel_f32, axis=1, keepdims=True)
    pc = jnp.broadcast_to(_round_up_run(cnt), (N_EXPERTS, 128))
    row = lax.broadcasted_iota(jnp.int32, (N_EXPERTS, N_EXPERTS), 0)
    col = lax.broadcasted_iota(jnp.int32, (N_EXPERTS, N_EXPERTS), 1)
    off = _dot(_bf(col < row), pc.astype(BF16))
    return pc, off


def _rank_in_run(sel_bf):
    n = sel_bf.shape[1]
    m_id = lax.broadcasted_iota(jnp.int32, (n, n), 0)
    n_id = lax.broadcasted_iota(jnp.int32, (n, n), 1)
    return _dot(sel_bf, _bf(m_id < n_id))


def _route_kernel(x_ref, wrt_ref, rbias_ref, gates_ref, pc_ref):
    xb = x_ref[...].astype(BF16)
    gates = _route(xb, wrt_ref, rbias_ref)
    gates_ref[...] = gates
    pad = jnp.zeros((128 - N_EXPERTS, MOE_TILE), BF16)
    ones = jnp.ones((16, MOE_TILE), BF16)
    for i in range(ROUTE_TILES):
        sel_bf = _bf(gates[:, i * MOE_TILE:(i + 1) * MOE_TILE] > 0.0)
        cnt = _dot_nt(ones, jnp.concatenate([sel_bf, pad], axis=0))
        pc_ref[i] = _round_up_run(cnt)[:1].astype(jnp.int32)


def _moe_route(x1, wrt, rbias):
    n_tok = x1.shape[0]
    n_tiles = n_tok // MOE_TILE
    rows = ROUTE_TILES * MOE_TILE
    return pl.pallas_call(
        _route_kernel,
        grid=(n_tok // rows,),
        in_specs=[
            pl.BlockSpec((rows, D_MODEL), lambda t: (t, 0)),
            _const_spec((N_EXPERTS, D_MODEL)),
            _const_spec((N_EXPERTS, 1)),
        ],
        out_specs=[
            pl.BlockSpec((N_EXPERTS, rows), lambda t: (0, t)),
            pl.BlockSpec((ROUTE_TILES, 1, 128), lambda t: (t, 0, 0)),
        ],
        out_shape=[
            jax.ShapeDtypeStruct((N_EXPERTS, n_tok), F32),
            jax.ShapeDtypeStruct((n_tiles, 1, 128), jnp.int32),
        ],
        compiler_params=pltpu.CompilerParams(dimension_semantics=("arbitrary",)),
        name="moe_route",
    )(x1, wrt, rbias)


PLAN_ESTART, PLAN_ENUM, PLAN_GAPSTART, PLAN_GAPN, PLAN_NBLOCKS = range(5)


def _plan_kernel(pc_ref, gbase_ref, tchunks_ref, ev_ref):
    n_tiles = pc_ref.shape[0]
    pad = jnp.zeros((128 - n_tiles, 128), F32)
    q = jnp.concatenate([pc_ref[...].astype(F32), pad], axis=0) * (1.0 / RUN_ALIGN)
    qb = q.astype(BF16)
    r = lax.broadcasted_iota(jnp.int32, (128, 128), 0)
    c = lax.broadcasted_iota(jnp.int32, (128, 128), 1)
    ones = jnp.ones((128, 128), BF16)
    before = _dot(_bf(c < r), qb) * RUN_ALIGN
    rows_e = _dot(ones[:16], qb) * RUN_ALIGN
    nb = jnp.ceil(rows_e * (1.0 / EXPERT_BLOCK))
    estart = _dot(nb.astype(BF16), _bf(r < c))
    g0 = estart * EXPERT_BLOCK
    gbase_ref[...] = (g0[:1] + before).astype(jnp.int32)
    tchunks_ref[...] = _dot(qb, ones).astype(jnp.int32)
    n_blocks = _dot(nb.astype(BF16), ones)
    row = lax.broadcasted_iota(jnp.int32, (8, 128), 0)
    ev = jnp.zeros((8, 128), F32)
    for i, val in ((PLAN_ESTART, estart), (PLAN_ENUM, nb), (PLAN_GAPSTART, g0 + rows_e),
                   (PLAN_GAPN, (nb * EXPERT_BLOCK - rows_e) * (1.0 / RUN_ALIGN)),
                   (PLAN_NBLOCKS, n_blocks)):
        ev = jnp.where(row == i, val[:8], ev)
    ev_ref[...] = ev.astype(jnp.int32)


def _max_blocks(n_tiles):
    per_tile = MOE_TILE * TOP_K + N_EXPERTS * (RUN_ALIGN - 1)
    rows = n_tiles * per_tile + N_EXPERTS * (EXPERT_BLOCK - RUN_ALIGN)
    return -(-rows // EXPERT_BLOCK)


def _moe_plan(pc):
    assert pc.shape[0] <= 128 and MOE_TILE // RUN_ALIGN < 256
    assert pc.shape[0] * MOE_TILE // EXPERT_BLOCK + 1 < 256
    i32 = jnp.int32
    return pl.pallas_call(
        _plan_kernel,
        out_shape=[
            jax.ShapeDtypeStruct((128, 128), i32),
            jax.ShapeDtypeStruct((128, 128), i32),
            jax.ShapeDtypeStruct((8, 128), i32),
        ],
        name="moe_plan",
    )(pc)


def _aligned(row):
    return row if isinstance(row, int) else pl.multiple_of(row, RUN_ALIGN)


def _rows_copy(src_ref, src_row, dst_ref, dst_row, rows, sem):
    return pltpu.make_async_copy(
        src_ref.at[pl.ds(_aligned(src_row), rows), :],
        dst_ref.at[pl.ds(_aligned(dst_row), rows), :],
        sem)


def _start_run(src_ref, src_row, dst_ref, dst_row, n_rows, sem, src_advances=True):
    def src_at(done):
        return src_row + done if src_advances else src_row

    def big_body(k, c):
        _rows_copy(src_ref, src_at(k * COPY_ROWS), dst_ref, dst_row + k * COPY_ROWS,
                   COPY_ROWS, sem).start()
        return c

    n_big = n_rows // COPY_ROWS
    lax.fori_loop(0, n_big, big_body, 0)
    done = n_big * COPY_ROWS
    mid = n_rows & (2 * RUN_ALIGN)

    @pl.when(mid != 0)
    def _mid():
        _rows_copy(src_ref, src_at(done), dst_ref, dst_row + done, 2 * RUN_ALIGN, sem).start()

    @pl.when((n_rows & RUN_ALIGN) != 0)
    def _small():
        _rows_copy(src_ref, src_at(done + mid), dst_ref, dst_row + done + mid, RUN_ALIGN,
                   sem).start()


def _wait_rows(src_ref, dst_ref, sem, n_rows):
    def big_body(i, c):
        _rows_copy(src_ref, 0, dst_ref, 0, COPY_ROWS, sem).wait()
        return c

    def small_body(i, c):
        _rows_copy(src_ref, 0, dst_ref, 0, RUN_ALIGN, sem).wait()
        return c

    lax.fori_loop(0, n_rows // COPY_ROWS, big_body, 0)
    lax.fori_loop(0, (n_rows % COPY_ROWS) // RUN_ALIGN, small_body, 0)


def _dispatch_kernel(pc_ref, gbase_ref, tchunks_ref, ev_ref,
                     x_ref, gates_ref, xg_ref, xs_scr, zero_scr, sems):
    t = pl.program_id(0)
    n_tiles = pl.num_programs(0)
    slot = t % 2

    @pl.when(t >= 2)
    def _drain_slot():
        _wait_rows(xs_scr.at[slot], xg_ref, sems.at[slot], tchunks_ref[t - 2, 0] * RUN_ALIGN)

    gates = gates_ref[...]
    sel = gates > 0.0
    sel_bf = _bf(sel)
    rankp = jnp.where(sel, _rank_in_run(sel_bf), -1.0).astype(BF16)
    pc_row, off_row = _run_layout_rows(sel_bf)
    pc_col, off_col = _run_layout_cols(jnp.where(sel, 1.0, 0.0))
    s_id = lax.broadcasted_iota(jnp.int32, (TILE_SLOTS, N_EXPERTS), 0).astype(F32)
    owner = _bf((s_id >= off_row[:1]) & (s_id < off_row[:1] + pc_row[:1]))
    rank_of_slot = _dot(owner, rankp)
    off_of_slot = _dot(owner, (off_col * (1.0 / RUN_ALIGN)).astype(BF16)) * RUN_ALIGN
    s_lane = lax.broadcasted_iota(jnp.int32, (TILE_SLOTS, 128), 0).astype(F32)
    onehot = jnp.concatenate(
        [_bf(rank_of_slot[:, c * 128:(c + 1) * 128] + off_of_slot == s_lane)
         for c in range(MOE_TILE // 128)], axis=1)
    xb = x_ref[...].astype(BF16)
    for c in range(TILE_SLOTS // 512):
        xs_scr[slot, c * 512:(c + 1) * 512, :] = _dot(
            onehot[c * 512:(c + 1) * 512, :], xb).astype(BF16)

    def expert_body(e, src):
        n = pc_ref[t, e]
        _start_run(xs_scr.at[slot], src, xg_ref, gbase_ref[t, e], n, sems.at[slot])
        return src + n

    lax.fori_loop(0, N_EXPERTS, expert_body, 0)

    @pl.when(t == n_tiles - 1)
    def _finish():
        zero_scr[...] = jnp.zeros_like(zero_scr)

        def gap_body(e, total):
            n = ev_ref[PLAN_GAPN, e] * RUN_ALIGN
            _start_run(zero_scr, 0, xg_ref, ev_ref[PLAN_GAPSTART, e], n, sems.at[2],
                       src_advances=False)
            return total + n

        gap_rows = lax.fori_loop(0, N_EXPERTS, gap_body, 0)
        _wait_rows(zero_scr, xg_ref, sems.at[2], gap_rows)
        _wait_rows(xs_scr.at[slot], xg_ref, sems.at[slot], tchunks_ref[t, 0] * RUN_ALIGN)

        @pl.when(t >= 1)
        def _drain_other():
            _wait_rows(xs_scr.at[1 - slot], xg_ref, sems.at[1 - slot],
                       tchunks_ref[t - 1, 0] * RUN_ALIGN)


def _moe_dispatch(x1, gates, pc, gbase, tchunks, ev, n_rows):
    n_tiles = x1.shape[0] // MOE_TILE
    smem = pl.BlockSpec(memory_space=pltpu.SMEM)
    return pl.pallas_call(
        _dispatch_kernel,
        grid=(n_tiles,),
        in_specs=[
            smem, smem, smem, smem,
            pl.BlockSpec((MOE_TILE, D_MODEL), lambda t: (t, 0)),
            pl.BlockSpec((N_EXPERTS, MOE_TILE), lambda t: (0, t)),
        ],
        out_specs=pl.BlockSpec(memory_space=pl.ANY),
        out_shape=jax.ShapeDtypeStruct((n_rows, D_MODEL), BF16),
        scratch_shapes=[
            pltpu.VMEM((2, TILE_SLOTS, D_MODEL), BF16),
            pltpu.VMEM((COPY_ROWS, D_MODEL), BF16),
            pltpu.SemaphoreType.DMA((3,)),
        ],
        compiler_params=pltpu.CompilerParams(
            dimension_semantics=("arbitrary",), vmem_limit_bytes=VMEM_LIMIT_BYTES),
        name="moe_dispatch",
    )(pc, gbase, tchunks, ev, x1, gates)


def _experts_kernel(ev_ref, wg_ref, wu_ref, wd_ref, xg_ref, yg_ref,
                    x_scr, y_scr, wg_scr, wu_scr, wd_scr, sem_in, sem_out):
    e = pl.program_id(0)
    n_blocks = ev_ref[PLAN_NBLOCKS, 0]
    first_block = ev_ref[PLAN_ESTART, e]
    blocks_here = ev_ref[PLAN_ENUM, e]

    def x_copy(g, slot):
        rows = pl.ds(pl.multiple_of(g * EXPERT_BLOCK, EXPERT_BLOCK), EXPERT_BLOCK)
        return pltpu.make_async_copy(xg_ref.at[rows, :], x_scr.at[slot], sem_in.at[slot])

    def y_copy(g, slot):
        rows = pl.ds(pl.multiple_of(g * EXPERT_BLOCK, EXPERT_BLOCK), EXPERT_BLOCK)
        return pltpu.make_async_copy(y_scr.at[slot], yg_ref.at[rows, :], sem_out.at[slot])

    @pl.when((e == 0) & (n_blocks > 0))
    def _first_fetch():
        x_copy(0, 0).start()

    @pl.when(blocks_here > 0)
    def _load_weights():
        wg_scr[...] = wg_ref[...].astype(BF16)
        wu_scr[...] = wu_ref[...].astype(BF16)
        wd_scr[...] = wd_ref[...].astype(BF16)

    def block_body(i, c):
        g = first_block + i
        slot = g % 2
        x_copy(g, slot).wait()

        @pl.when(g + 1 < n_blocks)
        def _prefetch():
            x_copy(g + 1, 1 - slot).start()

        @pl.when(g >= 2)
        def _free_out_buffer():
            y_copy(g - 2, slot).wait()

        h = _swiglu(x_scr[slot], wg_scr[...], wu_scr[...])
        y_scr[slot] = _dot(h.astype(BF16), wd_scr[...]).astype(BF16)
        y_copy(g, slot).start()
        return c

    lax.fori_loop(0, blocks_here, block_body, 0)

    @pl.when(e == pl.num_programs(0) - 1)
    def _drain():
        for back in (1, 2):
            @pl.when(n_blocks >= back)
            def _wait_out():
                g = n_blocks - back
                y_copy(g, g % 2).wait()


def _moe_experts(ev, xg, wg, wu, wd):
    def w_map(e, ev_ref):
        return (e, 0, 0)

    grid_spec = pltpu.PrefetchScalarGridSpec(
        num_scalar_prefetch=1,
        grid=(N_EXPERTS,),
        in_specs=[
            pl.BlockSpec((None, D_MODEL, EXPERT_DIM), w_map),
            pl.BlockSpec((None, D_MODEL, EXPERT_DIM), w_map),
            pl.BlockSpec((None, EXPERT_DIM, D_MODEL), w_map),
            pl.BlockSpec(memory_space=pl.ANY),
        ],
        out_specs=pl.BlockSpec(memory_space=pl.ANY),
        scratch_shapes=[
            pltpu.VMEM((2, EXPERT_BLOCK, D_MODEL), BF16),
            pltpu.VMEM((2, EXPERT_BLOCK, D_MODEL), BF16),
            pltpu.VMEM((D_MODEL, EXPERT_DIM), BF16),
            pltpu.VMEM((D_MODEL, EXPERT_DIM), BF16),
            pltpu.VMEM((EXPERT_DIM, D_MODEL), BF16),
            pltpu.SemaphoreType.DMA((2,)),
            pltpu.SemaphoreType.DMA((2,)),
        ],
    )
    return pl.pallas_call(
        _experts_kernel,
        grid_spec=grid_spec,
        out_shape=jax.ShapeDtypeStruct(xg.shape, BF16),
        compiler_params=pltpu.CompilerParams(
            dimension_semantics=("arbitrary",), vmem_limit_bytes=VMEM_LIMIT_BYTES),
        name="moe_experts",
    )(ev, wg, wu, wd, xg)


def _combine_kernel(pc_ref, gbase_ref, tchunks_ref,
                    x_ref, gates_ref, yg_ref, wgs_ref, wus_ref, wds_ref, ln2g_ref, ln2b_ref,
                    out_a_ref, out_b_ref, ys_scr, sems, *, tiles_a):
    t = pl.program_id(0)
    n_tiles = pl.num_programs(0)
    slot = t % 2

    def start_gather(tile, dst_slot):
        def expert_body(e, dst):
            n = pc_ref[tile, e]
            _start_run(yg_ref, gbase_ref[tile, e], ys_scr.at[dst_slot], dst, n,
                       sems.at[dst_slot])
            return dst + n

        lax.fori_loop(0, N_EXPERTS, expert_body, 0)

    @pl.when(t == 0)
    def _prime():
        ys_scr[...] = jnp.zeros_like(ys_scr)
        start_gather(0, 0)

    @pl.when(t + 1 < n_tiles)
    def _prefetch():
        start_gather(t + 1, 1 - slot)

    gates = gates_ref[...]
    sel = gates > 0.0
    sel_bf = _bf(sel)
    rankp = jnp.where(sel, _rank_in_run(sel_bf), -1.0)
    pc_row, off_row = _run_layout_rows(sel_bf)
    pc_col, off_col = _run_layout_cols(jnp.where(sel, 1.0, 0.0))
    pad = jnp.zeros((128 - N_EXPERTS, MOE_TILE), F32)
    gates_t = jnp.concatenate([gates, pad], axis=0).T.astype(BF16)
    rankp_t = jnp.concatenate([rankp, pad], axis=0).T.astype(BF16)
    s_id = lax.broadcasted_iota(jnp.int32, (N_EXPERTS, TILE_SLOTS), 1).astype(F32)
    owner = _bf((s_id >= off_col[:, :1]) & (s_id < off_col[:, :1] + pc_col[:, :1]))
    owner = jnp.concatenate([owner, jnp.zeros((128 - N_EXPERTS, TILE_SLOTS), BF16)], axis=0)
    rank_of_slot = _dot(rankp_t, owner)
    gate_of_slot = _dot(gates_t, owner)
    o16 = jnp.concatenate([off_row * (1.0 / RUN_ALIGN), jnp.zeros((16, 128 - N_EXPERTS), F32)],
                          axis=1).astype(BF16)
    off_of_slot = _dot(o16, owner)[:1] * RUN_ALIGN
    s_row = lax.broadcasted_iota(jnp.int32, (1, TILE_SLOTS), 1).astype(F32)
    weights = jnp.where(rank_of_slot + off_of_slot == s_row, gate_of_slot, 0.0).astype(BF16)

    x = x_ref[...]
    xb = x.astype(BF16)
    hs = _swiglu(xb, wgs_ref[...].astype(BF16), wus_ref[...].astype(BF16))
    shared = _dot(hs.astype(BF16), wds_ref[...].astype(BF16))

    _wait_rows(yg_ref, ys_scr.at[slot], sems.at[slot], tchunks_ref[t, 0] * RUN_ALIGN)
    routed = _dot(weights, ys_scr[slot])
    y = _layer_norm(ALPHA * x + (routed + shared), ln2g_ref[...], ln2b_ref[...])

    @pl.when(t < tiles_a)
    def _store_a():
        out_a_ref[...] = y

    @pl.when(t >= tiles_a)
    def _store_b():
        out_b_ref[...] = y


def _moe_combine(x1, gates, yg, pc, gbase, tchunks, wgs, wus, wds, ln2g, ln2b, rows_a):
    n_tok = x1.shape[0]
    tiles_a = rows_a // MOE_TILE
    smem = pl.BlockSpec(memory_space=pltpu.SMEM)
    return pl.pallas_call(
        functools.partial(_combine_kernel, tiles_a=tiles_a),
        grid=(n_tok // MOE_TILE,),
        in_specs=[
            smem, smem, smem,
            pl.BlockSpec((MOE_TILE, D_MODEL), lambda t: (t, 0)),
            pl.BlockSpec((N_EXPERTS, MOE_TILE), lambda t: (0, t)),
            pl.BlockSpec(memory_space=pl.ANY),
            _const_spec((D_MODEL, SHARED_DIM)),
            _const_spec((D_MODEL, SHARED_DIM)),
            _const_spec((SHARED_DIM, D_MODEL)),
            _const_spec((1, D_MODEL)),
            _const_spec((1, D_MODEL)),
        ],
        out_specs=[
            pl.BlockSpec((MOE_TILE, D_MODEL), lambda t: (jnp.minimum(t, tiles_a - 1), 0)),
            pl.BlockSpec((MOE_TILE, D_MODEL), lambda t: (jnp.maximum(t - tiles_a, 0), 0)),
        ],
        out_shape=[
            jax.ShapeDtypeStruct((rows_a, D_MODEL), F32),
            jax.ShapeDtypeStruct((n_tok - rows_a, D_MODEL), F32),
        ],
        scratch_shapes=[
            pltpu.VMEM((2, TILE_SLOTS, D_MODEL), BF16),
            pltpu.SemaphoreType.DMA((2,)),
        ],
        compiler_params=pltpu.CompilerParams(
            dimension_semantics=("arbitrary",), vmem_limit_bytes=VMEM_LIMIT_BYTES),
        name="moe_combine",
    )(pc, gbase, tchunks, x1, gates, yg, wgs, wus, wds, ln2g, ln2b)


def _moe(x1, rows_a, wrt, rbias, wg, wu, wd, wgs, wus, wds, ln2g, ln2b):
    n_tiles = x1.shape[0] // MOE_TILE
    gates, pc3 = _moe_route(x1, wrt, rbias)
    pc = pc3.reshape(n_tiles, 128)
    gbase, tchunks, ev = _moe_plan(pc)
    n_rows = _max_blocks(n_tiles) * EXPERT_BLOCK
    xg = _moe_dispatch(x1, gates, pc, gbase, tchunks, ev, n_rows)
    yg = _moe_experts(ev, xg, wg, wu, wd)
    return _moe_combine(x1, gates, yg, pc, gbase, tchunks, wgs, wus, wds, ln2g, ln2b, rows_a)


def kernel(x_prompt, x_sample, cache_win_k, cache_win_v, rel_bias_table, w_in, b_in, attn_sinks,
           sg_ln_g, sg_ln_b, sg_w, sg_b, w_proj_a, w_proj_b, w_out, ln1_g, ln1_b, w_router,
           router_bias, w_gate_e, w_up_e, w_down_e, w_gate_s, w_up_s, w_down_s, ln2_g, ln2_b):
    assert DEPTH == 1 and w_in.shape[0] == 1
    batch, seq, _ = x_prompt.shape
    nb, ds, _ = x_sample.shape
    win = cache_win_k.shape[2]

    w_in_bf = w_in[0].astype(BF16)
    wpa = w_proj_a[0].astype(BF16)
    wpb = w_proj_b[0].astype(BF16)
    wout = w_out[0].astype(BF16)
    b_in2 = b_in[0].reshape(1, IN_W)
    sgg = sg_ln_g[0].reshape(1, SG_WIDTH)
    sgb = sg_ln_b[0].reshape(1, SG_WIDTH)
    sgw = sg_w[0]
    sgbias = sg_b[0].reshape(N_SG_GROUPS, CHUNK, 1)
    ln1g = ln1_g[0].reshape(1, D_MODEL)
    ln1b = ln1_b[0].reshape(1, D_MODEL)
    sinks = attn_sinks[0]

    x1_p, wk_p, wv_p = _mixer_prompt(
        x_prompt, nb * ds, rel_bias_table, sinks, w_in_bf, b_in2, sgg, sgb, sgw, sgbias, wpa, wpb,
        wout, ln1g, ln1b)
    x1, wk_s, wv_s, cvs = _mixer_sample(
        x_sample, cache_win_k[0].reshape(nb, win, KV_W), cache_win_v[0].reshape(nb, win, KV_W),
        x1_p, rel_bias_table, sinks, w_in_bf, b_in2, sgg, sgb, sgw, sgbias, wpa, wpb, wout,
        ln1g, ln1b)

    y_p, y_s = _moe(x1, batch * seq, wrt=w_router[0].T,
                    rbias=router_bias[0].reshape(N_EXPERTS, 1),
                    wg=w_gate_e[0], wu=w_up_e[0], wd=w_down_e[0],
                    wgs=w_gate_s[0], wus=w_up_s[0], wds=w_down_s[0],
                    ln2g=ln2_g[0].reshape(1, D_MODEL), ln2b=ln2_b[0].reshape(1, D_MODEL))
    y_p = y_p.reshape(batch, seq, D_MODEL)
    y_s = y_s.reshape(nb, ds, D_MODEL)

    kv_shape = (1, -1, WINDOW, N_KV_HEADS, HEAD_DIM)
    return (y_p, y_s,
            wk_p.reshape(kv_shape), wv_p.reshape(kv_shape),
            wk_s.reshape(1, nb, win, N_KV_HEADS, HEAD_DIM),
            wv_s.reshape(1, nb, win, N_KV_HEADS, HEAD_DIM),
            cvs.reshape(1, nb, ds, N_SG_GROUPS, SG_GROUP_DIM))
```
